```python
import math
import jax, jax.numpy as jnp
from jax import lax
import numpy as np

D_MODEL = 4096
BATCH = 4
SEQ = 2048
DEPTH = 1
DEC_BATCH = 128
DEC_SEQ = 4
PAST_LEN = 16384
PAGE_SIZE = 128

CONV_W = 4
CHUNK = 64
DN_HEAD_DIM = 128
DN_HEADS = D_MODEL // 256
DN_WIDTH = DN_HEADS * DN_HEAD_DIM
SSM_HEAD_DIM = 64
SSM_HEADS = D_MODEL // 128
SSM_GROUPS = 8
SSM_STATE = 128
SSM_INNER = SSM_HEADS * SSM_HEAD_DIM
SSM_CONV_DIM = SSM_INNER + 2 * SSM_GROUPS * SSM_STATE
MEM_LEN = 256
CA_HEADS = 4
CA_HEAD_DIM = D_MODEL // 16
CA_WIDTH = CA_HEADS * CA_HEAD_DIM
PEER_HEADS = 8
PEER_NKEYS = 128
PEER_EXPERTS = PEER_NKEYS * PEER_NKEYS
PEER_TOPK = 16
PEER_QDIM = 256
PEER_BLOCK = 128
IN_SPLITS = (3 * DN_WIDTH, DN_WIDTH, DN_HEADS, DN_HEADS, SSM_INNER, SSM_CONV_DIM, SSM_HEADS, D_MODEL, D_MODEL)
IN_COLS = sum(IN_SPLITS)
EPS = 1e-6

kernel_name = "hybrid_deltanet_ssd_peer_decode_step"


def rmsnorm(x, g):
    xf = x.astype(jnp.float32)
    y = xf * lax.rsqrt(jnp.mean(xf * xf, axis=-1, keepdims=True) + EPS)
    return (y * g.astype(jnp.float32)).astype(x.dtype)


def l2norm(x):
    return x * lax.rsqrt(jnp.sum(x * x, axis=-1, keepdims=True) + EPS)


def causal_conv(u, buf, w, b=None):
    L = u.shape[1]
    full = jnp.concatenate([buf.astype(u.dtype), u], axis=1)
    out = sum(full[:, j:j + L] * w[j] for j in range(CONV_W))
    if b is not None:
        out = out + b
    return out, full[:, -(CONV_W - 1):]


def _to_chunks(a, c):
    Bsz, L = a.shape[:2]
    pad = (-L) % c
    a = jnp.pad(a, [(0, 0), (0, pad)] + [(0, 0)] * (a.ndim - 2))
    return a.reshape(Bsz, (L + pad) // c, c, *a.shape[2:]).swapaxes(0, 1)


def _from_chunks(a, L):
    a = a.swapaxes(0, 1)
    return a.reshape(a.shape[0], -1, *a.shape[3:])[:, :L]


def gated_delta_rule(q, k, v, g, beta, h0):
    L = q.shape[1]
    c = min(CHUNK, L)
    idx = jnp.arange(c)
    strict = idx[:, None] > idx[None, :]
    incl = idx[:, None] >= idx[None, :]
    eye = jnp.eye(c, dtype=jnp.float32)
    dv = v.shape[-1]

    def step(h, blk):
        qc, kc, vc, gb, bc = blk
        gc = jnp.cumsum(gb, axis=1)
        gT = gc.swapaxes(1, 2)
        diff = gT[..., :, None] - gT[..., None, :]
        dec_s = jnp.exp(jnp.where(strict, diff, -jnp.inf))
        dec_i = jnp.exp(jnp.where(incl, diff, -jnp.inf))
        kk = jnp.einsum('bthd,bshd->bhts', kc, kc)
        lmat = eye + bc.swapaxes(1, 2)[..., :, None] * kk * dec_s
        rhs = jnp.concatenate([(vc * bc[..., None]).swapaxes(1, 2),
                               (kc * (bc * jnp.exp(gc))[..., None]).swapaxes(1, 2)], axis=-1)
        sol = lax.linalg.triangular_solve(lmat, rhs, left_side=True, lower=True, unit_diagonal=True)
        w = sol[..., :dv] - jnp.einsum('bhck,bhkv->bhcv', sol[..., dv:], h)
        qk = jnp.einsum('bthd,bshd->bhts', qc, kc) * dec_i
        o = (jnp.einsum('bthk,bhkv->bthv', qc, h) * jnp.exp(gc)[..., None]
             + jnp.einsum('bhts,bhsv->bthv', qk, w))
        h_new = (jnp.exp(gT[..., -1])[..., None, None] * h
                 + jnp.einsum('bshk,bhsv->bhkv', kc * jnp.exp(gc[:, -1:] - gc)[..., None], w))
        return h_new, o

    xs = (_to_chunks(q, c), _to_chunks(k, c), _to_chunks(v, c), _to_chunks(g, c), _to_chunks(beta, c))
    h, o = lax.scan(step, h0, xs)
    return _from_chunks(o, L), h


def ssd_scan(x, dt, a, bm, cm, h0):
    L = x.shape[1]
    c = min(CHUNK, L)
    idx = jnp.arange(c)
    incl = idx[:, None] >= idx[None, :]

    def step(h, blk):
        xc, dtc, bc, cc = blk
        gc = jnp.cumsum(dtc * a, axis=1)
        gT = jnp.moveaxis(gc, 1, -1)
        lmat = jnp.exp(jnp.where(incl, gT[..., :, None] - gT[..., None, :], -jnp.inf))
        cb = jnp.einsum('btgn,bsgn->bgts', cc, bc)
        xdt = xc * dtc[..., None]
        y = (jnp.einsum('bgrts,bsgrp->btgrp', cb[:, :, None] * lmat, xdt)
             + jnp.einsum('btgn,bgrpn->btgrp', cc, h) * jnp.exp(gc)[..., None])
        h_new = (jnp.exp(gc[:, -1])[..., None, None] * h
                 + jnp.einsum('bsgn,bsgrp->bgrpn', bc, xdt * jnp.exp(gc[:, -1:] - gc)[..., None]))
        return h_new, y

    xs = (_to_chunks(x, c), _to_chunks(dt, c), _to_chunks(bm, c), _to_chunks(cm, c))
    h, y = lax.scan(step, h0, xs)
    return _from_chunks(y, L), h


def peer_ffn(h, w_q, keys, u_tab, v_tab):
    T = h.shape[0]
    pad = (-T) % PEER_BLOCK
    hb = jnp.pad(h, ((0, pad), (0, 0))).reshape(-1, PEER_BLOCK, D_MODEL)

    def block(hblk):
        q = (hblk @ w_q).reshape(PEER_BLOCK, PEER_HEADS, 2, PEER_QDIM // 2).astype(jnp.float32)
        s = jnp.einsum('thcd,hcnd->thcn', q, keys.astype(jnp.float32))
        s1, i1 = lax.top_k(s[:, :, 0], PEER_TOPK)
        s2, i2 = lax.top_k(s[:, :, 1], PEER_TOPK)
        cand_s = (s1[..., :, None] + s2[..., None, :]).reshape(PEER_BLOCK, PEER_HEADS, PEER_TOPK * PEER_TOPK)
        cand_i = (i1[..., :, None] * PEER_NKEYS + i2[..., None, :]).reshape(PEER_BLOCK, PEER_HEADS, PEER_TOPK * PEER_TOPK)
        top_s, pos = lax.top_k(cand_s, PEER_TOPK)
        idx = jnp.take_along_axis(cand_i, pos, axis=-1).reshape(PEER_BLOCK, PEER_HEADS * PEER_TOPK)
        gate = jax.nn.softmax(top_s, axis=-1).reshape(PEER_BLOCK, PEER_HEADS * PEER_TOPK)
        u = u_tab[idx]
        vv = v_tab[idx]
        act = jax.nn.gelu(jnp.einsum('tkd,td->tk', u, hblk).astype(jnp.float32)) * gate
        return jnp.einsum('tk,tkd->td', act.astype(h.dtype), vv)

    out = lax.map(block, hb)
    return out.reshape(-1, D_MODEL)[:T]


def trunk_layer(x, mem_k, mem_v, dn_conv0, dn_h0, ssm_conv0, ssm_h0,
                norm_mix_g, w_in, dn_conv_w, dn_a_log, dn_dt_bias, dn_norm_g, dn_w_out,
                ssm_conv_w, ssm_conv_b, ssm_a_log, ssm_dt_bias, ssm_d, ssm_norm_g, ssm_w_out,
                w_o, norm_ca_g, ca_w_q, ca_w_o, norm_ffn_g, peer_w_q, peer_keys, peer_u, peer_v):
    f32 = jnp.float32
    Bsz, L, _ = x.shape
    hn = rmsnorm(x, norm_mix_g)
    proj = hn @ w_in
    offs = [sum(IN_SPLITS[:i + 1]) for i in range(len(IN_SPLITS) - 1)]
    dn_qkv, dn_z, dn_b, dn_a, ssm_z, ssm_xbc, ssm_dt, gate_dn, gate_ssm = jnp.split(proj, offs, axis=-1)

    qkv, dn_conv1 = causal_conv(dn_qkv, dn_conv0, dn_conv_w)
    qkv = jax.nn.silu(qkv).astype(f32)
    q, k, v = jnp.split(qkv, 3, axis=-1)
    q = l2norm(q.reshape(Bsz, L, DN_HEADS, DN_HEAD_DIM)) * (DN_HEAD_DIM ** -0.5)
    k = l2norm(k.reshape(Bsz, L, DN_HEADS, DN_HEAD_DIM))
    v = v.reshape(Bsz, L, DN_HEADS, DN_HEAD_DIM)
    beta = jax.nn.sigmoid(dn_b.astype(f32))
    g = -jnp.exp(dn_a_log.astype(f32)) * jax.nn.softplus(dn_a.astype(f32) + dn_dt_bias.astype(f32))
    o, dn_h1 = gated_delta_rule(q, k, v, g, beta, dn_h0.astype(f32))
    o = rmsnorm(o, dn_norm_g) * jax.nn.silu(dn_z.astype(f32).reshape(Bsz, L, DN_HEADS, DN_HEAD_DIM))
    o_dn = o.reshape(Bsz, L, DN_WIDTH).astype(x.dtype) @ dn_w_out

    xbc, ssm_conv1 = causal_conv(ssm_xbc, ssm_conv0, ssm_conv_w, ssm_conv_b)
    xbc = jax.nn.silu(xbc).astype(f32)
    xs, bm, cm = jnp.split(xbc, [SSM_INNER, SSM_INNER + SSM_GROUPS * SSM_STATE], axis=-1)
    R = SSM_HEADS // SSM_GROUPS
    xs = xs.reshape(Bsz, L, SSM_GROUPS, R, SSM_HEAD_DIM)
    bm = bm.reshape(Bsz, L, SSM_GROUPS, SSM_STATE)
    cm = cm.reshape(Bsz, L, SSM_GROUPS, SSM_STATE)
    dt = jax.nn.softplus(ssm_dt.astype(f32) + ssm_dt_bias.astype(f32)).reshape(Bsz, L, SSM_GROUPS, R)
    a = -jnp.exp(ssm_a_log.astype(f32)).reshape(SSM_GROUPS, R)
    h0 = ssm_h0.astype(f32).reshape(Bsz, SSM_GROUPS, R, SSM_HEAD_DIM, SSM_STATE)
    y, ssm_h1 = ssd_scan(xs, dt, a, bm, cm, h0)
    y = y + ssm_d.astype(f32).reshape(SSM_GROUPS, R)[..., None] * xs
    y = rmsnorm(y.reshape(Bsz, L, SSM_INNER) * jax.nn.silu(ssm_z.astype(f32)), ssm_norm_g)
    o_ssm = y.astype(x.dtype) @ ssm_w_out
    ssm_h1 = ssm_h1.reshape(Bsz, SSM_HEADS, SSM_HEAD_DIM, SSM_STATE)

    mixed = jax.nn.sigmoid(gate_dn) * o_dn + jax.nn.sigmoid(gate_ssm) * o_ssm
    x = x + mixed @ w_o

    hc = rmsnorm(x, norm_ca_g)
    qc = (hc @ ca_w_q).reshape(Bsz, L, CA_HEADS, CA_HEAD_DIM)
    s = jnp.einsum('blhd,bmhd->bhlm', qc, mem_k).astype(f32) * (CA_HEAD_DIM ** -0.5)
    p = jax.nn.softmax(s, axis=-1).astype(x.dtype)
    oc = jnp.einsum('bhlm,bmhd->blhd', p, mem_v).reshape(Bsz, L, CA_WIDTH)
    x = x + oc @ ca_w_o

    hf = rmsnorm(x, norm_ffn_g).reshape(Bsz * L, D_MODEL)
    x = x + peer_ffn(hf, peer_w_q, peer_keys, peer_u, peer_v).reshape(Bsz, L, D_MODEL)
    return x, dn_conv1, dn_h1, ssm_conv1, ssm_h1


def setup_inputs(seed: int = 0) -> dict:
    key = jax.random.key(seed)
    ks = list(jax.random.split(key, 40))
    f32 = jnp.float32
    D = D_MODEL

    def nrm(shape, scale):
        return jax.random.normal(ks.pop(), shape, f32) * scale

    def gain(shape):
        return 1.0 + 0.02 * jax.random.normal(ks.pop(), shape, f32)

    def a_log(n):
        return jnp.log(jax.random.uniform(ks.pop(), (DEPTH, n), f32, 1.0, 16.0))

    def dt_bias(n):
        dt = jnp.exp(jax.random.uniform(ks.pop(), (DEPTH, n), f32, math.log(1e-3), math.log(1e-1)))
        return dt + jnp.log(-jnp.expm1(-dt))

    return {
        'x_prompt': nrm((BATCH, SEQ, D), 1.0),
        'x_sample': nrm((DEC_BATCH, DEC_SEQ, D), 1.0),
        'cache_mem_k': nrm((DEPTH, DEC_BATCH, MEM_LEN, CA_HEADS, CA_HEAD_DIM), 1.0),
        'cache_mem_v': nrm((DEPTH, DEC_BATCH, MEM_LEN, CA_HEADS, CA_HEAD_DIM), 1.0),
        'state_dn_conv': nrm((DEPTH, DEC_BATCH, CONV_W - 1, 3 * DN_WIDTH), 1.0),
        'state_dn_rec': nrm((DEPTH, DEC_BATCH, DN_HEADS, DN_HEAD_DIM, DN_HEAD_DIM), 0.05),
        'state_ssm_conv': nrm((DEPTH, DEC_BATCH, CONV_W - 1, SSM_CONV_DIM), 1.0),
        'state_ssm_rec': nrm((DEPTH, DEC_BATCH, SSM_HEADS, SSM_HEAD_DIM, SSM_STATE), 0.05),
        'mem_prompt': nrm((BATCH, MEM_LEN, D), 1.0),
        'norm_mix_g': gain((DEPTH, D)),
        'w_in': nrm((DEPTH, D, IN_COLS), D ** -0.5),
        'dn_conv_w': nrm((DEPTH, CONV_W, 3 * DN_WIDTH), CONV_W ** -0.5),
        'dn_a_log': a_log(DN_HEADS),
        'dn_dt_bias': dt_bias(DN_HEADS),
        'dn_norm_g': gain((DEPTH, DN_HEAD_DIM)),
        'dn_w_out': nrm((DEPTH, DN_WIDTH, D), DN_WIDTH ** -0.5),
        'ssm_conv_w': nrm((DEPTH, CONV_W, SSM_CONV_DIM), CONV_W ** -0.5),
        'ssm_conv_b': nrm((DEPTH, SSM_CONV_DIM), 0.02),
        'ssm_a_log': a_log(SSM_HEADS),
        'ssm_dt_bias': dt_bias(SSM_HEADS),
        'ssm_d': 1.0 + nrm((DEPTH, SSM_HEADS), 0.1),
        'ssm_norm_g': gain((DEPTH, SSM_INNER)),
        'ssm_w_out': nrm((DEPTH, SSM_INNER, D), SSM_INNER ** -0.5),
        'w_o': nrm((DEPTH, D, D), D ** -0.5),
        'norm_ca_g': gain((DEPTH, D)),
        'ca_w_q': nrm((DEPTH, D, CA_WIDTH), D ** -0.5),
        'ca_w_k': nrm((DEPTH, D, CA_WIDTH), D ** -0.5),
        'ca_w_v': nrm((DEPTH, D, CA_WIDTH), D ** -0.5),
        'ca_w_o': nrm((DEPTH, CA_WIDTH, D), CA_WIDTH ** -0.5),
        'norm_ffn_g': gain((DEPTH, D)),
        'peer_w_q': nrm((DEPTH, D, PEER_HEADS * PEER_QDIM), D ** -0.5),
        'peer_keys': nrm((DEPTH, PEER_HEADS, 2, PEER_NKEYS, PEER_QDIM // 2), (PEER_QDIM // 2) ** -0.5),
        'peer_u': nrm((DEPTH, PEER_EXPERTS, D), D ** -0.5),
        'peer_v': nrm((DEPTH, PEER_EXPERTS, D), PEER_HEADS ** -0.5),
        'final_norm_g': gain((D,)),
    }


def reference(x_prompt, x_sample, cache_mem_k, cache_mem_v, state_dn_conv, state_dn_rec,
              state_ssm_conv, state_ssm_rec, mem_prompt,
              norm_mix_g, w_in, dn_conv_w, dn_a_log, dn_dt_bias, dn_norm_g, dn_w_out,
              ssm_conv_w, ssm_conv_b, ssm_a_log, ssm_dt_bias, ssm_d, ssm_norm_g, ssm_w_out,
              w_o, norm_ca_g, ca_w_q, ca_w_k, ca_w_v, ca_w_o,
              norm_ffn_g, peer_w_q, peer_keys, peer_u, peer_v, final_norm_g):
    Bp = x_prompt.shape[0]
    yp, ys = x_prompt, x_sample
    p_mk, p_mv, p_dc, p_dh, p_sc, p_sh = [], [], [], [], [], []
    s_dc, s_dh, s_sc, s_sh = [], [], [], []
    for l in range(DEPTH):
        lw = (norm_mix_g[l], w_in[l], dn_conv_w[l], dn_a_log[l], dn_dt_bias[l], dn_norm_g[l], dn_w_out[l],
              ssm_conv_w[l], ssm_conv_b[l], ssm_a_log[l], ssm_dt_bias[l], ssm_d[l], ssm_norm_g[l], ssm_w_out[l],
              w_o[l], norm_ca_g[l], ca_w_q[l], ca_w_o[l], norm_ffn_g[l], peer_w_q[l], peer_keys[l],
              peer_u[l], peer_v[l])
        mk = (mem_prompt @ ca_w_k[l]).reshape(Bp, MEM_LEN, CA_HEADS, CA_HEAD_DIM)
        mv = (mem_prompt @ ca_w_v[l]).reshape(Bp, MEM_LEN, CA_HEADS, CA_HEAD_DIM)
        z_dc = jnp.zeros((Bp, CONV_W - 1, 3 * DN_WIDTH), x_prompt.dtype)
        z_dh = jnp.zeros((Bp, DN_HEADS, DN_HEAD_DIM, DN_HEAD_DIM), jnp.float32)
        z_sc = jnp.zeros((Bp, CONV_W - 1, SSM_CONV_DIM), x_prompt.dtype)
        z_sh = jnp.zeros((Bp, SSM_HEADS, SSM_HEAD_DIM, SSM_STATE), jnp.float32)
        yp, a1, a2, a3, a4 = trunk_layer(yp, mk, mv, z_dc, z_dh, z_sc, z_sh, *lw)
        p_mk.append(mk); p_mv.append(mv)
        p_dc.append(a1); p_dh.append(a2); p_sc.append(a3); p_sh.append(a4)
        ys, b1, b2, b3, b4 = trunk_layer(ys, cache_mem_k[l], cache_mem_v[l], state_dn_conv[l], state_dn_rec[l],
                                         state_ssm_conv[l], state_ssm_rec[l], *lw)
        s_dc.append(b1); s_dh.append(b2); s_sc.append(b3); s_sh.append(b4)
    y_prompt = rmsnorm(yp, final_norm_g)
    y_sample = rmsnorm(ys, final_norm_g)
    return (y_prompt, y_sample,
            jnp.stack(p_mk), jnp.stack(p_mv), jnp.stack(p_dc), jnp.stack(p_dh), jnp.stack(p_sc), jnp.stack(p_sh),
            jnp.stack(s_dc), jnp.stack(s_dh), jnp.stack(s_sc), jnp.stack(s_sh))
```

```python
import functools

import jax
import jax.numpy as jnp
from jax import lax
from jax.experimental import pallas as pl
from jax.experimental.pallas import tpu as pltpu

F32 = jnp.float32
BF16 = jnp.bfloat16

EPS = 1e-6
CONV_W = 4
CHUNK = 64
PEER_TOPK = 16
LANES = 128
SUBLANES = 8
NEG = -1e30
MIB = 1 << 20


def _params(sem, vmem_mib):
    return pltpu.CompilerParams(dimension_semantics=sem, vmem_limit_bytes=vmem_mib * MIB)


def _tile(n, pref):
    t = min(pref, n)
    while n % t:
        t //= 2
    return t


def _sigmoid(x):
    return 1.0 / (1.0 + jnp.exp(-x))


def _silu(x):
    return x * _sigmoid(x)


def _softplus(x):
    return jnp.maximum(x, 0.0) + jnp.log1p(jnp.exp(-jnp.abs(x)))


def _gelu_tanh(x):
    return x * (0.5 * (1.0 + jnp.tanh(0.7978845608028654 * (x + 0.044715 * (x * x * x)))))


def _bdot(a, b):
    return jnp.dot(a.astype(BF16), b.astype(BF16), preferred_element_type=F32)


def _bdot_nt(a, b):
    return lax.dot_general(a.astype(BF16), b.astype(BF16), (((1,), (1,)), ((), ())),
                           preferred_element_type=F32)


def _bdot_tn(a, b):
    return lax.dot_general(a.astype(BF16), b.astype(BF16), (((0,), (0,)), ((), ())),
                           preferred_element_type=F32)


def _rmsnorm_body(x_ref, g_ref, o_ref):
    x = x_ref[...]
    y = x * lax.rsqrt(jnp.mean(x * x, axis=-1, keepdims=True) + EPS)
    o_ref[...] = (y * g_ref[...]).astype(o_ref.dtype)


def _addnorm_body(x_ref, p_ref, g_ref, o_ref):
    x = x_ref[...] + p_ref[...]
    y = x * lax.rsqrt(jnp.mean(x * x, axis=-1, keepdims=True) + EPS)
    o_ref[...] = (y * g_ref[...]).astype(o_ref.dtype)


def _add_body(x_ref, p_ref, o_ref):
    o_ref[...] = x_ref[...] + p_ref[...]


def rmsnorm(x, g, out_dtype, tm=256):
    m, d = x.shape
    tm = _tile(m, tm)
    row = pl.BlockSpec((tm, d), lambda i: (i, 0))
    return pl.pallas_call(
        _rmsnorm_body, grid=(m // tm,),
        in_specs=[row, pl.BlockSpec((1, d), lambda i: (0, 0))],
        out_specs=row, out_shape=jax.ShapeDtypeStruct((m, d), out_dtype),
        compiler_params=_params(("parallel",), 32), name="rmsnorm",
    )(x, g.reshape(1, d))


def add_rows(x, p, g=None, tm=256):
    m, d = x.shape
    tm = _tile(m, tm)
    row = pl.BlockSpec((tm, d), lambda i: (i, 0))
    if g is None:
        body, ins, args = _add_body, [row, row], (x, p)
    else:
        body, ins, args = _addnorm_body, [row, row, pl.BlockSpec((1, d), lambda i: (0, 0))], (x, p, g.reshape(1, d))
    return pl.pallas_call(
        body, grid=(m // tm,), in_specs=ins, out_specs=row,
        out_shape=jax.ShapeDtypeStruct((m, d), F32),
        compiler_params=_params(("parallel",), 40), name="add_norm",
    )(*args)


def _mm_body(a_ref, w_ref, *rest, has_res):
    o_ref = rest[-1]
    acc = jnp.dot(a_ref[...].astype(BF16), w_ref[...], preferred_element_type=F32)
    if has_res:
        acc = rest[0][...] + acc
    o_ref[...] = acc.astype(o_ref.dtype)


def matmul(a, w, res=None, out_dtype=F32, tm=512, tn=1024):
    m, k = a.shape
    n = w.shape[1]
    tm, tn = _tile(m, tm), _tile(n, tn)
    ins = [pl.BlockSpec((tm, k), lambda j, i: (i, 0)), pl.BlockSpec((k, tn), lambda j, i: (0, j))]
    args = [a, w]
    if res is not None:
        ins.append(pl.BlockSpec((tm, tn), lambda j, i: (i, j)))
        args.append(res)
    return pl.pallas_call(
        functools.partial(_mm_body, has_res=res is not None), grid=(n // tn, m // tm),
        in_specs=ins, out_specs=pl.BlockSpec((tm, tn), lambda j, i: (i, j)),
        out_shape=jax.ShapeDtypeStruct((m, n), out_dtype),
        compiler_params=_params(("parallel", "parallel"), 48), name="matmul",
    )(*args)


def _mix_body(a1_ref, w1_ref, a2_ref, w2_ref, g1_ref, g2_ref, o_ref):
    y1 = jnp.dot(a1_ref[...], w1_ref[...], preferred_element_type=F32)
    y2 = jnp.dot(a2_ref[...], w2_ref[...], preferred_element_type=F32)
    o_ref[...] = (_sigmoid(g1_ref[...]) * y1 + _sigmoid(g2_ref[...]) * y2).astype(o_ref.dtype)


def gated_mix(a1, w1, a2, w2, gates, tm=512, tn=1024):
    m, k1 = a1.shape
    k2 = a2.shape[1]
    n = w1.shape[1]
    tm, tn = _tile(m, tm), _tile(n, tn)
    nj = n // tn
    return pl.pallas_call(
        _mix_body, grid=(nj, m // tm),
        in_specs=[pl.BlockSpec((tm, k1), lambda j, i: (i, 0)), pl.BlockSpec((k1, tn), lambda j, i: (0, j)),
                  pl.BlockSpec((tm, k2), lambda j, i: (i, 0)), pl.BlockSpec((k2, tn), lambda j, i: (0, j)),
                  pl.BlockSpec((tm, tn), lambda j, i: (i, j)), pl.BlockSpec((tm, tn), lambda j, i: (i, j + nj))],
        out_specs=pl.BlockSpec((tm, tn), lambda j, i: (i, j)),
        out_shape=jax.ShapeDtypeStruct((m, n), BF16),
        compiler_params=_params(("parallel", "parallel"), 48), name="gated_mix",
    )(a1, w1, a2, w2, gates, gates)


def _conv_body(x_ref, prev_ref, st_ref, w_ref, b_ref, o_ref, buf, *, rb):
    i = pl.program_id(1)
    buf[SUBLANES:SUBLANES + rb, :] = x_ref[...]

    @pl.when(i == 0)
    def _():
        buf[SUBLANES - (CONV_W - 1):SUBLANES, :] = st_ref[0]

    @pl.when(i > 0)
    def _():
        buf[0:SUBLANES, :] = prev_ref[...]

    base = SUBLANES - (CONV_W - 1)
    acc = buf[base:base + rb, :] * w_ref[0:1, :]
    for j in range(1, CONV_W):
        acc = acc + buf[base + j:base + j + rb, :] * w_ref[j:j + 1, :]
    acc = acc + b_ref[...]
    o_ref[...] = _silu(acc)


def conv_silu(x2d, row0, nb, seqlen, col0, ncol, state, w, b, rb):
    rb = _tile(seqlen, rb)
    cb = 512 if (ncol % 512 == 0 and col0 % 512 == 0) else LANES
    nrb, ncb = seqlen // rb, ncol // cb
    r0, p0, c0 = row0 // rb, row0 // SUBLANES, col0 // cb
    rs = rb // SUBLANES
    return pl.pallas_call(
        functools.partial(_conv_body, rb=rb), grid=(nb, nrb, ncb),
        in_specs=[
            pl.BlockSpec((rb, cb), lambda s, i, j: (r0 + s * nrb + i, c0 + j)),
            pl.BlockSpec((SUBLANES, cb), lambda s, i, j: (jnp.maximum(p0 + (s * nrb + i) * rs - 1, 0), c0 + j)),
            pl.BlockSpec((1, CONV_W - 1, cb), lambda s, i, j: (s, 0, j)),
            pl.BlockSpec((CONV_W, cb), lambda s, i, j: (0, j)),
            pl.BlockSpec((1, cb), lambda s, i, j: (0, j)),
        ],
        out_specs=pl.BlockSpec((rb, cb), lambda s, i, j: (s * nrb + i, j)),
        out_shape=jax.ShapeDtypeStruct((nb * seqlen, ncol), F32),
        scratch_shapes=[pltpu.VMEM((rb + SUBLANES, cb), F32)],
        compiler_params=_params(("parallel", "parallel", "parallel"), 32), name="conv_silu",
    )(x2d, x2d, state, w, b.reshape(1, ncol))


def _chunk_cumsum(g, c):
    tt = lax.broadcasted_iota(jnp.int32, (c, c), 0)
    ss = lax.broadcasted_iota(jnp.int32, (c, c), 1)
    tril = jnp.where(tt >= ss, 1.0, 0.0).astype(BF16)
    hi = g.astype(BF16)
    r1 = g - hi.astype(F32)
    mid = r1.astype(BF16)
    lo = (r1 - mid.astype(F32)).astype(BF16)
    dot = functools.partial(jnp.dot, preferred_element_type=F32)
    return dot(tril, hi) + dot(tril, mid) + dot(tril, lo)


def _rows_as_lanes(x, c):
    if c < LANES:
        x = jnp.concatenate([x, jnp.zeros((LANES - c, LANES), F32)], axis=0)
    return x.T


def _dn_body(q_ref, k_ref, v_ref, z_ref, sm_ref, h0_ref, pa_ref, ng_ref, o_ref, ht_ref, h_scr,
             *, c, valid, nh, dk, nsq):
    ci = pl.program_id(1)

    @pl.when(ci == 0)
    def _():
        h_scr[...] = h0_ref[0]

    sm = sm_ref[...]
    pa = pa_ref[...]
    beta_all = _sigmoid(sm)
    g_all = -jnp.exp(pa[0:1]) * _softplus(sm + pa[1:2])
    if valid < c:
        live = lax.broadcasted_iota(jnp.int32, (c, LANES), 0) < valid
        beta_all = jnp.where(live, beta_all, 0.0)
        g_all = jnp.where(live, g_all, 0.0)
    gc_all = _chunk_cumsum(g_all, c)
    gct = _rows_as_lanes(gc_all, c)
    eg_all = jnp.exp(gc_all)
    tt = lax.broadcasted_iota(jnp.int32, (c, c), 0)
    ss = lax.broadcasted_iota(jnp.int32, (c, c), 1)
    strict = tt > ss
    incl = tt >= ss
    scale = dk ** -0.5

    for h in range(nh):
        sl = slice(h * dk, (h + 1) * dk)
        q = q_ref[:, sl]
        k = k_ref[:, sl]
        v = v_ref[:, sl]
        q = q * lax.rsqrt(jnp.sum(q * q, axis=-1, keepdims=True) + EPS) * scale
        k = k * lax.rsqrt(jnp.sum(k * k, axis=-1, keepdims=True) + EPS)
        gcol = gc_all[:, nh + h:nh + h + 1]
        grow = gct[nh + h:nh + h + 1, 0:c]
        diff = gcol - grow
        dec_s = jnp.exp(jnp.where(strict, diff, NEG))
        dec_i = jnp.exp(jnp.where(incl, diff, NEG))
        bcol = beta_all[:, h:h + 1]
        egc = eg_all[:, nh + h:nh + h + 1]
        pw = -(bcol * _bdot_nt(k, k) * dec_s)
        qm = pw
        for _ in range(nsq):
            pw = _bdot(pw, pw)
            qm = qm + pw + _bdot(qm, pw)
        rhs = jnp.concatenate([v * bcol, k * (bcol * egc)], axis=1)
        sol = rhs + _bdot(qm, rhs)
        hh = h_scr[h]
        w = sol[:, :dk] - _bdot(sol[:, dk:], hh)
        qk = _bdot_nt(q, k) * dec_i
        o = _bdot(q, hh) * egc + _bdot(qk, w)
        glast = gc_all[c - 1:c, nh + h:nh + h + 1]
        h_scr[h] = jnp.exp(glast) * hh + _bdot_tn(k * jnp.exp(glast - gcol), w)
        o = o * lax.rsqrt(jnp.mean(o * o, axis=-1, keepdims=True) + EPS) * ng_ref[...]
        o_ref[:, sl] = (o * _silu(z_ref[:, sl])).astype(o_ref.dtype)

    @pl.when(ci == pl.num_programs(1) - 1)
    def _():
        ht_ref[0] = h_scr[...]


def deltanet(conv, seg_a, seg_d, row0, nb, seqlen, c, valid, h0, pa, ng, nh, dk):
    w = nh * dk
    nc = seqlen // c
    r0 = row0 // c
    nsq = 0
    while (2 << nsq) < valid:
        nsq += 1
    loc = lambda s, i: (s * nc + i, 0)
    glob = lambda s, i: (r0 + s * nc + i, 0)
    st = pl.BlockSpec((1, nh, dk, dk), lambda s, i: (s, 0, 0, 0))
    return pl.pallas_call(
        functools.partial(_dn_body, c=c, valid=valid, nh=nh, dk=dk, nsq=nsq), grid=(nb, nc),
        in_specs=[pl.BlockSpec((c, w), loc),
                  pl.BlockSpec((c, w), lambda s, i: (s * nc + i, 1)),
                  pl.BlockSpec((c, w), lambda s, i: (s * nc + i, 2)),
                  pl.BlockSpec((c, w), lambda s, i: (r0 + s * nc + i, 3)),
                  pl.BlockSpec((c, LANES), glob),
                  st,
                  pl.BlockSpec((SUBLANES, LANES), lambda s, i: (0, 0)),
                  pl.BlockSpec((1, dk), lambda s, i: (0, 0))],
        out_specs=[pl.BlockSpec((c, w), loc), st],
        out_shape=[jax.ShapeDtypeStruct((nb * seqlen, w), BF16),
                   jax.ShapeDtypeStruct((nb, nh, dk, dk), F32)],
        scratch_shapes=[pltpu.VMEM((nh, dk, dk), F32)],
        compiler_params=_params(("parallel", "arbitrary"), 32), name="deltanet",
    )(conv, conv, conv, seg_a, seg_d, h0, pa, ng.reshape(1, dk))


def _ssd_body(x_ref, b_ref, c_ref, z_ref, sm_ref, h0_ref, pa_ref, ng_ref, y_ref, ht_ref, h_scr, y_scr,
              *, c, valid, ng, nr, hp, ns, off):
    ci = pl.program_id(1)

    @pl.when(ci == 0)
    def _():
        h_scr[...] = h0_ref[0]

    gw = nr * hp
    shift = hp.bit_length() - 1
    sm = sm_ref[...]
    pa = pa_ref[...]
    dt_all = _softplus(sm + pa[1:2])
    if valid < c:
        dt_all = jnp.where(lax.broadcasted_iota(jnp.int32, (c, LANES), 0) < valid, dt_all, 0.0)
    gc_all = _chunk_cumsum(dt_all * (-jnp.exp(pa[0:1])), c)
    gct = _rows_as_lanes(gc_all, c)
    eg_all = jnp.exp(gc_all)
    tt = lax.broadcasted_iota(jnp.int32, (c, c), 0)
    ss = lax.broadcasted_iota(jnp.int32, (c, c), 1)
    incl = tt >= ss
    lane_head = lax.broadcasted_iota(jnp.int32, (c, gw), 1) >> shift
    row_head = lax.broadcasted_iota(jnp.int32, (gw, ns), 0) >> shift

    for g in range(ng):
        bg = b_ref[:, g * ns:(g + 1) * ns]
        cg = c_ref[:, g * ns:(g + 1) * ns]
        xg = x_ref[:, g * gw:(g + 1) * gw]
        hg = h_scr[g]
        cb = _bdot_nt(cg, bg)
        dtb = jnp.zeros((c, gw), F32)
        egb = jnp.zeros((c, gw), F32)
        kdb = jnp.zeros((c, gw), F32)
        skip = jnp.zeros((1, gw), F32)
        hdec = jnp.zeros((gw, ns), F32)
        for r in range(nr):
            col = off + g * nr + r
            gcol = gc_all[:, col:col + 1]
            glast = gc_all[c - 1:c, col:col + 1]
            seg = lane_head == r
            dtb = jnp.where(seg, dt_all[:, col:col + 1], dtb)
            egb = jnp.where(seg, eg_all[:, col:col + 1], egb)
            kdb = jnp.where(seg, jnp.exp(glast - gcol), kdb)
            skip = jnp.where(seg[0:1], pa[2:3, col:col + 1], skip)
            hdec = jnp.where(row_head == r, jnp.exp(glast), hdec)
        xdt = xg * dtb
        yg = _bdot_nt(cg, hg) * egb + skip * xg
        for r in range(nr):
            col = off + g * nr + r
            lm = jnp.exp(jnp.where(incl, gc_all[:, col:col + 1] - gct[col:col + 1, 0:c], NEG))
            yg = yg + _bdot(cb * lm, jnp.where(lane_head == r, xdt, 0.0))
        h_scr[g] = hdec * hg + _bdot_tn(xdt * kdb, bg)
        y_scr[:, g * gw:(g + 1) * gw] = yg * _silu(z_ref[:, g * gw:(g + 1) * gw])

    y = y_scr[...]
    y = y * lax.rsqrt(jnp.mean(y * y, axis=-1, keepdims=True) + EPS) * ng_ref[...]
    y_ref[...] = y.astype(y_ref.dtype)

    @pl.when(ci == pl.num_programs(1) - 1)
    def _():
        ht_ref[0] = h_scr[...]


def ssd(conv, seg_b, seg_d, row0, nb, seqlen, c, valid, h0, pa, ngain, ng, nr, hp, ns, off):
    inner = ng * nr * hp
    gn = ng * ns
    nc = seqlen // c
    r0 = row0 // c
    loc = lambda s, i: (s * nc + i, 0)
    st = pl.BlockSpec((1, ng, nr * hp, ns), lambda s, i: (s, 0, 0, 0))
    return pl.pallas_call(
        functools.partial(_ssd_body, c=c, valid=valid, ng=ng, nr=nr, hp=hp, ns=ns, off=off), grid=(nb, nc),
        in_specs=[pl.BlockSpec((c, inner), loc),
                  pl.BlockSpec((c, gn), lambda s, i: (s * nc + i, inner // gn)),
                  pl.BlockSpec((c, gn), lambda s, i: (s * nc + i, inner // gn + 1)),
                  pl.BlockSpec((c, inner), lambda s, i: (r0 + s * nc + i, 0)),
                  pl.BlockSpec((c, LANES), lambda s, i: (r0 + s * nc + i, 0)),
                  st,
                  pl.BlockSpec((SUBLANES, LANES), lambda s, i: (0, 0)),
                  pl.BlockSpec((1, inner), lambda s, i: (0, 0))],
        out_specs=[pl.BlockSpec((c, inner), loc), st],
        out_shape=[jax.ShapeDtypeStruct((nb * seqlen, inner), BF16),
                   jax.ShapeDtypeStruct((nb, ng, nr * hp, ns), F32)],
        scratch_shapes=[pltpu.VMEM((ng, nr * hp, ns), F32), pltpu.VMEM((c, inner), F32)],
        compiler_params=_params(("parallel", "arbitrary"), 32), name="ssd",
    )(conv, conv, conv, seg_b, seg_d, h0, pa, ngain.reshape(1, inner))


def _ca_body(q_ref, k_ref, v_ref, o_ref, *, nh, dh):
    scale = dh ** -0.5
    for h in range(nh):
        sl = slice(h * dh, (h + 1) * dh)
        s = _bdot_nt(q_ref[:, sl], k_ref[0, :, sl]) * scale
        e = jnp.exp(s - jnp.max(s, axis=-1, keepdims=True))
        p = e / jnp.sum(e, axis=-1, keepdims=True)
        o_ref[:, sl] = _bdot(p, v_ref[0, :, sl]).astype(o_ref.dtype)


def cross_attention(q2d, row0, nb, seqlen, mem_k, mem_v, nh, dh, tl=512):
    tl = _tile(seqlen, tl)
    nl = seqlen // tl
    r0 = row0 // tl
    mem, wd = mem_k.shape[1], nh * dh
    kv = pl.BlockSpec((1, mem, wd), lambda s, i: (s, 0, 0))
    return pl.pallas_call(
        functools.partial(_ca_body, nh=nh, dh=dh), grid=(nb, nl),
        in_specs=[pl.BlockSpec((tl, wd), lambda s, i: (r0 + s * nl + i, 0)), kv, kv],
        out_specs=pl.BlockSpec((tl, wd), lambda s, i: (s * nl + i, 0)),
        out_shape=jax.ShapeDtypeStruct((nb * seqlen, wd), BF16),
        compiler_params=_params(("parallel", "parallel"), 32), name="cross_attention",
    )(q2d, mem_k, mem_v)


def _extract_top(x, n):
    out = []
    for it in range(n):
        m = jnp.max(x, axis=0, keepdims=True)
        out.append(m)
        if it + 1 < n:
            x = jnp.where(x == m, -jnp.inf, x)
    return out


def _peer_route_body(q_ref, keys_ref, s1_ref, e1_ref, s2_ref, e2_ref, thr_ref, *, nh, dq, topk):
    for h in range(nh):
        s1 = _bdot_nt(keys_ref[2 * h], q_ref[:, (2 * h) * dq:(2 * h + 1) * dq])
        s2 = _bdot_nt(keys_ref[2 * h + 1], q_ref[:, (2 * h + 1) * dq:(2 * h + 2) * dq])
        v1 = _extract_top(s1, topk)
        v2 = _extract_top(s2, topk)
        v2s = jnp.concatenate(v2, axis=0)
        cand = jnp.concatenate([v1[a] + v2s for a in range(topk)], axis=0)
        tops = _extract_top(cand, topk)
        zsum = jnp.exp(tops[0] - tops[0])
        for a in range(1, topk):
            zsum = zsum + jnp.exp(tops[a] - tops[0])
        s1_ref[h] = s1
        s2_ref[h] = s2
        e1_ref[h] = jnp.exp(s1 - v1[0]) / zsum
        e2_ref[h] = jnp.exp(s2 - v2[0])
        thr_ref[h:h + 1, :] = tops[topk - 1]


def peer_route(q, keys, nh, nk, dq, tm=256):
    m = q.shape[0]
    tm = _tile(m, tm)
    big = pl.BlockSpec((nh, nk, tm), lambda i: (0, 0, i))
    shape = jax.ShapeDtypeStruct((nh, nk, m), F32)
    return pl.pallas_call(
        functools.partial(_peer_route_body, nh=nh, dq=dq, topk=PEER_TOPK), grid=(m // tm,),
        in_specs=[pl.BlockSpec((tm, 2 * nh * dq), lambda i: (i, 0)),
                  pl.BlockSpec((2 * nh, nk, dq), lambda i: (0, 0, 0))],
        out_specs=[big, big, big, big, pl.BlockSpec((nh, tm), lambda i: (0, i))],
        out_shape=[shape, shape, shape, shape, jax.ShapeDtypeStruct((nh, m), F32)],
        compiler_params=_params(("parallel",), 32), name="peer_route",
    )(q, keys)


def _peer_body(hf_ref, u_ref, v_ref, s1_ref, e1_ref, s2_ref, e2_ref, thr_ref, o_ref, act_scr, *, nh, nk, nrow):
    ei = pl.program_id(1)
    tm = hf_ref.shape[0]
    st = _bdot_nt(hf_ref[...], u_ref[...]).T
    for r in range(nrow):
        for tb in range(tm // LANES):
            ls = slice(tb * LANES, (tb + 1) * LANES)
            gate = jnp.zeros((nk, LANES), F32)
            for h in range(nh):
                hit = (s2_ref[h, :, ls] + s1_ref[r, h:h + 1, ls]) >= thr_ref[h:h + 1, ls]
                gate = gate + jnp.where(hit, e2_ref[h, :, ls] * e1_ref[r, h:h + 1, ls], 0.0)
            act_scr[r * nk:(r + 1) * nk, ls] = _gelu_tanh(st[r * nk:(r + 1) * nk, ls]) * gate
    contrib = jnp.dot(act_scr[...].T.astype(BF16), v_ref[...], preferred_element_type=F32)

    @pl.when(ei == 0)
    def _():
        o_ref[...] = contrib

    @pl.when(ei > 0)
    def _():
        o_ref[...] += contrib


def peer_experts(hf, u, v, s1g, e1g, s2, e2, thr, nh, nk, tm=512, nrow=2):
    m, d = hf.shape
    tm = _tile(m, tm)
    et = nrow * nk
    return pl.pallas_call(
        functools.partial(_peer_body, nh=nh, nk=nk, nrow=nrow), grid=(m // tm, nk // nrow),
        in_specs=[pl.BlockSpec((tm, d), lambda t, e: (t, 0)),
                  pl.BlockSpec((et, d), lambda t, e: (e, 0)),
                  pl.BlockSpec((et, d), lambda t, e: (e, 0)),
                  pl.BlockSpec((nrow, nh, tm), lambda t, e: (e, 0, t)),
                  pl.BlockSpec((nrow, nh, tm), lambda t, e: (e, 0, t)),
                  pl.BlockSpec((nh, nk, tm), lambda t, e: (0, 0, t)),
                  pl.BlockSpec((nh, nk, tm), lambda t, e: (0, 0, t)),
                  pl.BlockSpec((nh, tm), lambda t, e: (0, t))],
        out_specs=pl.BlockSpec((tm, d), lambda t, e: (t, 0)),
        out_shape=jax.ShapeDtypeStruct((m, d), F32),
        scratch_shapes=[pltpu.VMEM((et, tm), F32)],
        compiler_params=_params(("parallel", "arbitrary"), 56), name="peer_experts",
    )(hf, u, v, s1g, e1g, s2, e2, thr)


def _lane_row(nrows, pieces):
    out = jnp.zeros((SUBLANES, LANES), F32)
    for row, off, vec in pieces:
        out = out.at[row, off:off + vec.shape[0]].set(vec.astype(F32))
    return out


def kernel(x_prompt, x_sample, cache_mem_k, cache_mem_v, state_dn_conv, state_dn_rec, state_ssm_conv,
           state_ssm_rec, mem_prompt, norm_mix_g, w_in, dn_conv_w, dn_a_log, dn_dt_bias, dn_norm_g, dn_w_out,
           ssm_conv_w, ssm_conv_b, ssm_a_log, ssm_dt_bias, ssm_d, ssm_norm_g, ssm_w_out, w_o, norm_ca_g,
           ca_w_q, ca_w_k, ca_w_v, ca_w_o, norm_ffn_g, peer_w_q, peer_keys, peer_u, peer_v, final_norm_g):
    depth = w_in.shape[0]
    bp, lp, d = x_prompt.shape
    bs, ls, _ = x_sample.shape
    _, _, dnh, dk, _ = state_dn_rec.shape
    _, _, sh, hp, ns = state_ssm_rec.shape
    dnw = dnh * dk
    inner = sh * hp
    sconv = state_ssm_conv.shape[-1]
    sg = (sconv - inner) // (2 * ns)
    sr = sh // sg
    _, _, mem, cah, cad = cache_mem_k.shape
    caw = cah * cad
    _, pnh, _, nk, dq = peer_keys.shape
    lpad = -(-ls // SUBLANES) * SUBLANES
    assert lp % CHUNK == 0 and lpad <= CHUNK and dk == LANES and ns == LANES and nk == LANES and dq == LANES
    assert 2 * dnh + sh <= LANES and hp & (hp - 1) == 0
    mp, ms = bp * lp, bs * lpad
    ssm_off = 2 * dnh

    xs = jnp.pad(x_sample, ((0, 0), (0, lpad - ls), (0, 0)))
    x = jnp.concatenate([x_prompt.reshape(mp, d), xs.reshape(ms, d)], axis=0)

    outs = [[] for _ in range(10)]
    for l in range(depth):
        wl = w_in[l]
        o1 = 3 * dnw + dnw
        o2 = o1 + 2 * dnh
        o3 = o2 + inner
        o4 = o3 + sconv
        o5 = o4 + sh
        w_a = wl[:, :o1].astype(BF16)
        w_b = wl[:, o2:o4].astype(BF16)
        w_c = wl[:, o5:].astype(BF16)
        w_d = jnp.concatenate([wl[:, o1:o2], wl[:, o4:o5],
                               jnp.zeros((d, LANES - 2 * dnh - sh), F32)], axis=1).astype(BF16)

        hn = rmsnorm(x, norm_mix_g[l], BF16)
        seg_a = matmul(hn, w_a)
        seg_b = matmul(hn, w_b)
        seg_c = matmul(hn, w_c)
        seg_d = matmul(hn, w_d)

        dn_pa = _lane_row(SUBLANES, [(0, dnh, dn_a_log[l]), (1, dnh, dn_dt_bias[l])])
        ssm_pa = _lane_row(SUBLANES, [(0, ssm_off, ssm_a_log[l]), (1, ssm_off, ssm_dt_bias[l]), (2, ssm_off, ssm_d[l])])
        groups = (
            (0, bp, lp, CHUNK, CHUNK, jnp.zeros((bp, CONV_W - 1, 3 * dnw), F32), jnp.zeros((bp, dnh, dk, dk), F32),
             jnp.zeros((bp, CONV_W - 1, sconv), F32), jnp.zeros((bp, sg, sr * hp, ns), F32), 512),
            (mp, bs, lpad, lpad, ls, state_dn_conv[l], state_dn_rec[l],
             state_ssm_conv[l], state_ssm_rec[l].reshape(bs, sg, sr * hp, ns), lpad),
        )
        o_dn, o_ssm, dn_h, ssm_h = [], [], [], []
        for row0, nb, sl, c, valid, dc0, dh0, sc0, sh0, rb in groups:
            cdn = conv_silu(seg_a, row0, nb, sl, 0, 3 * dnw, dc0, dn_conv_w[l], jnp.zeros((3 * dnw,), F32), rb)
            od, hd = deltanet(cdn, seg_a, seg_d, row0, nb, sl, c, valid, dh0, dn_pa, dn_norm_g[l], dnh, dk)
            cs = conv_silu(seg_b, row0, nb, sl, inner, sconv, sc0, ssm_conv_w[l], ssm_conv_b[l], rb)
            osd, hsd = ssd(cs, seg_b, seg_d, row0, nb, sl, c, valid, sh0, ssm_pa, ssm_norm_g[l],
                           sg, sr, hp, ns, ssm_off)
            o_dn.append(od)
            o_ssm.append(osd)
            dn_h.append(hd)
            ssm_h.append(hsd.reshape(nb, sh, hp, ns))

        mixed = gated_mix(jnp.concatenate(o_dn, axis=0), dn_w_out[l].astype(BF16),
                          jnp.concatenate(o_ssm, axis=0), ssm_w_out[l].astype(BF16), seg_c)
        x1 = matmul(mixed, w_o[l].astype(BF16), res=x)

        hc = rmsnorm(x1, norm_ca_g[l], BF16)
        qc = matmul(hc, ca_w_q[l].astype(BF16))
        memp = mem_prompt.reshape(bp * mem, d)
        mk = matmul(memp, ca_w_k[l].astype(BF16))
        mv = matmul(memp, ca_w_v[l].astype(BF16))
        oc = jnp.concatenate([
            cross_attention(qc, 0, bp, lp, mk.reshape(bp, mem, caw), mv.reshape(bp, mem, caw), cah, cad),
            cross_attention(qc, mp, bs, lpad, cache_mem_k[l].reshape(bs, mem, caw),
                            cache_mem_v[l].reshape(bs, mem, caw), cah, cad)], axis=0)
        x2 = matmul(oc, ca_w_o[l].astype(BF16), res=x1)

        hf = rmsnorm(x2, norm_ffn_g[l], BF16)
        pq = matmul(hf, peer_w_q[l].astype(BF16))
        s1, e1, s2, e2, thr = peer_route(pq, peer_keys[l].reshape(2 * pnh, nk, dq).astype(BF16), pnh, nk, dq)
        po = peer_experts(hf, peer_u[l].astype(BF16), peer_v[l].astype(BF16),
                          s1.transpose(1, 0, 2), e1.transpose(1, 0, 2), s2, e2, thr, pnh, nk)
        x = add_rows(x2, po, final_norm_g if l == depth - 1 else None)

        def tail(seg, row0, nb, sl, nvalid, lo, hi, state0):
            u = seg[row0:row0 + nb * sl].reshape(nb, sl, -1)[:, max(nvalid - (CONV_W - 1), 0):nvalid, lo:hi]
            if nvalid < CONV_W - 1:
                u = jnp.concatenate([state0, u], axis=1)[:, -(CONV_W - 1):]
            return u

        outs[0].append(mk.reshape(bp, mem, cah, cad))
        outs[1].append(mv.reshape(bp, mem, cah, cad))
        outs[2].append(tail(seg_a, 0, bp, lp, lp, 0, 3 * dnw, groups[0][5]))
        outs[3].append(dn_h[0])
        outs[4].append(tail(seg_b, 0, bp, lp, lp, inner, inner + sconv, groups[0][7]))
        outs[5].append(ssm_h[0])
        outs[6].append(tail(seg_a, mp, bs, lpad, ls, 0, 3 * dnw, state_dn_conv[l]))
        outs[7].append(dn_h[1])
        outs[8].append(tail(seg_b, mp, bs, lpad, ls, inner, inner + sconv, state_ssm_conv[l]))
        outs[9].append(ssm_h[1])

    y_prompt = x[:mp].reshape(bp, lp, d)
    y_sample = x[mp:].reshape(bs, lpad, d)[:, :ls]
    return (y_prompt, y_sample) + tuple(jnp.stack(o) for o in outs)
```

```python
import functools

import jax
import jax.numpy as jnp
from jax import lax
from jax.experimental import pallas as pl
from jax.experimental.pallas import tpu as pltpu

F32 = jnp.float32
BF16 = jnp.bfloat16

EPS = 1e-6
CONV_W = 4
CHUNK = 64
PEER_TOPK = 16
LANES = 128
SUBLANES = 8
NEG = -1e30
MIB = 1 << 20


def _params(sem, vmem_mib):
    return pltpu.CompilerParams(dimension_semantics=sem, vmem_limit_bytes=vmem_mib * MIB)


def _tile(n, pref):
    t = min(pref, n)
    while n % t:
        t //= 2
    return t


def _sigmoid(x):
    return 1.0 / (1.0 + jnp.exp(-x))


def _silu(x):
    return x * _sigmoid(x)


def _softplus(x):
    return jnp.maximum(x, 0.0) + jnp.log1p(jnp.exp(-jnp.abs(x)))


def _gelu_tanh(x):
    return x * (0.5 * (1.0 + jnp.tanh(0.7978845608028654 * (x + 0.044715 * (x * x * x)))))


def _bdot(a, b):
    return jnp.dot(a.astype(BF16), b.astype(BF16), preferred_element_type=F32)


def _bdot_nt(a, b):
    return lax.dot_general(a.astype(BF16), b.astype(BF16), (((1,), (1,)), ((), ())),
                           preferred_element_type=F32)


def _bdot_tn(a, b):
    return lax.dot_general(a.astype(BF16), b.astype(BF16), (((0,), (0,)), ((), ())),
                           preferred_element_type=F32)


def _rmsnorm_body(x_ref, g_ref, o_ref):
    x = x_ref[...]
    y = x * lax.rsqrt(jnp.mean(x * x, axis=-1, keepdims=True) + EPS)
    o_ref[...] = (y * g_ref[...]).astype(o_ref.dtype)


def _addnorm_body(x_ref, pt_ref, g_ref, o_ref):
    x = x_ref[...] + pt_ref[...].T
    y = x * lax.rsqrt(jnp.mean(x * x, axis=-1, keepdims=True) + EPS)
    o_ref[...] = (y * g_ref[...]).astype(o_ref.dtype)


def _add_body(x_ref, pt_ref, o_ref):
    o_ref[...] = x_ref[...] + pt_ref[...].T


def _rmsnorm_both_body(x_ref, g_ref, o_ref, ot_ref):
    x = x_ref[...]
    y = x * lax.rsqrt(jnp.mean(x * x, axis=-1, keepdims=True) + EPS) * g_ref[...]
    o_ref[...] = y.astype(o_ref.dtype)
    ot_ref[...] = y.T.astype(ot_ref.dtype)


def rmsnorm(x, g, out_dtype, tm=256, with_transpose=False):
    m, d = x.shape
    tm = _tile(m, tm)
    row = pl.BlockSpec((tm, d), lambda i: (i, 0))
    if with_transpose:
        body, outs = _rmsnorm_both_body, [row, pl.BlockSpec((d, tm), lambda i: (0, i))]
        shapes = [jax.ShapeDtypeStruct((m, d), out_dtype), jax.ShapeDtypeStruct((d, m), out_dtype)]
    else:
        body, outs, shapes = _rmsnorm_body, row, jax.ShapeDtypeStruct((m, d), out_dtype)
    return pl.pallas_call(
        body, grid=(m // tm,),
        in_specs=[row, pl.BlockSpec((1, d), lambda i: (0, 0))],
        out_specs=outs, out_shape=shapes,
        compiler_params=_params(("parallel",), 32), name="rmsnorm",
    )(x, g.reshape(1, d))


def add_rows(x, pt, g=None, tm=256):
    m, d = x.shape
    tm = _tile(m, tm)
    row = pl.BlockSpec((tm, d), lambda i: (i, 0))
    col = pl.BlockSpec((d, tm), lambda i: (0, i))
    if g is None:
        body, ins, args = _add_body, [row, col], (x, pt)
    else:
        body, ins, args = _addnorm_body, [row, col, pl.BlockSpec((1, d), lambda i: (0, 0))], (x, pt, g.reshape(1, d))
    return pl.pallas_call(
        body, grid=(m // tm,), in_specs=ins, out_specs=row,
        out_shape=jax.ShapeDtypeStruct((m, d), F32),
        compiler_params=_params(("parallel",), 40), name="add_norm",
    )(*args)


def _mm_body(a_ref, w_ref, *rest, has_res):
    o_ref = rest[-1]
    acc = jnp.dot(a_ref[...].astype(BF16), w_ref[...], preferred_element_type=F32)
    if has_res:
        acc = rest[0][...] + acc
    o_ref[...] = acc.astype(o_ref.dtype)


def matmul(a, w, res=None, out_dtype=F32, tm=512, tn=1024):
    m, k = a.shape
    n = w.shape[1]
    tm, tn = _tile(m, tm), _tile(n, tn)
    ins = [pl.BlockSpec((tm, k), lambda j, i: (i, 0)), pl.BlockSpec((k, tn), lambda j, i: (0, j))]
    args = [a, w]
    if res is not None:
        ins.append(pl.BlockSpec((tm, tn), lambda j, i: (i, j)))
        args.append(res)
    return pl.pallas_call(
        functools.partial(_mm_body, has_res=res is not None), grid=(n // tn, m // tm),
        in_specs=ins, out_specs=pl.BlockSpec((tm, tn), lambda j, i: (i, j)),
        out_shape=jax.ShapeDtypeStruct((m, n), out_dtype),
        compiler_params=_params(("parallel", "parallel"), 48), name="matmul",
    )(*args)


def _mix_body(a1_ref, w1_ref, a2_ref, w2_ref, g1_ref, g2_ref, o_ref):
    y1 = jnp.dot(a1_ref[...], w1_ref[...], preferred_element_type=F32)
    y2 = jnp.dot(a2_ref[...], w2_ref[...], preferred_element_type=F32)
    o_ref[...] = (_sigmoid(g1_ref[...]) * y1 + _sigmoid(g2_ref[...]) * y2).astype(o_ref.dtype)


def gated_mix(a1, w1, a2, w2, gates, tm=512, tn=1024):
    m, k1 = a1.shape
    k2 = a2.shape[1]
    n = w1.shape[1]
    tm, tn = _tile(m, tm), _tile(n, tn)
    nj = n // tn
    return pl.pallas_call(
        _mix_body, grid=(nj, m // tm),
        in_specs=[pl.BlockSpec((tm, k1), lambda j, i: (i, 0)), pl.BlockSpec((k1, tn), lambda j, i: (0, j)),
                  pl.BlockSpec((tm, k2), lambda j, i: (i, 0)), pl.BlockSpec((k2, tn), lambda j, i: (0, j)),
                  pl.BlockSpec((tm, tn), lambda j, i: (i, j)), pl.BlockSpec((tm, tn), lambda j, i: (i, j + nj))],
        out_specs=pl.BlockSpec((tm, tn), lambda j, i: (i, j)),
        out_shape=jax.ShapeDtypeStruct((m, n), BF16),
        compiler_params=_params(("parallel", "parallel"), 48), name="gated_mix",
    )(a1, w1, a2, w2, gates, gates)


def _conv_body(x_ref, prev_ref, st_ref, w_ref, b_ref, o_ref, buf, *, rb):
    i = pl.program_id(1)
    buf[SUBLANES:SUBLANES + rb, :] = x_ref[...]

    @pl.when(i == 0)
    def _():
        buf[SUBLANES - (CONV_W - 1):SUBLANES, :] = st_ref[0]

    @pl.when(i > 0)
    def _():
        buf[0:SUBLANES, :] = prev_ref[...]

    base = SUBLANES - (CONV_W - 1)
    acc = buf[base:base + rb, :] * w_ref[0:1, :]
    for j in range(1, CONV_W):
        acc = acc + buf[base + j:base + j + rb, :] * w_ref[j:j + 1, :]
    acc = acc + b_ref[...]
    o_ref[...] = _silu(acc)


def conv_silu(x2d, row0, nb, seqlen, col0, ncol, state, w, b, rb):
    rb = _tile(seqlen, rb)
    cb = 512 if (ncol % 512 == 0 and col0 % 512 == 0) else LANES
    nrb, ncb = seqlen // rb, ncol // cb
    r0, p0, c0 = row0 // rb, row0 // SUBLANES, col0 // cb
    rs = rb // SUBLANES
    return pl.pallas_call(
        functools.partial(_conv_body, rb=rb), grid=(nb, nrb, ncb),
        in_specs=[
            pl.BlockSpec((rb, cb), lambda s, i, j: (r0 + s * nrb + i, c0 + j)),
            pl.BlockSpec((SUBLANES, cb), lambda s, i, j: (jnp.maximum(p0 + (s * nrb + i) * rs - 1, 0), c0 + j)),
            pl.BlockSpec((1, CONV_W - 1, cb), lambda s, i, j: (s, 0, j)),
            pl.BlockSpec((CONV_W, cb), lambda s, i, j: (0, j)),
            pl.BlockSpec((1, cb), lambda s, i, j: (0, j)),
        ],
        out_specs=pl.BlockSpec((rb, cb), lambda s, i, j: (s * nrb + i, j)),
        out_shape=jax.ShapeDtypeStruct((nb * seqlen, ncol), F32),
        scratch_shapes=[pltpu.VMEM((rb + SUBLANES, cb), F32)],
        compiler_params=_params(("parallel", "parallel", "parallel"), 32), name="conv_silu",
    )(x2d, x2d, state, w, b.reshape(1, ncol))


def _chunk_cumsum(g, c):
    tt = lax.broadcasted_iota(jnp.int32, (c, c), 0)
    ss = lax.broadcasted_iota(jnp.int32, (c, c), 1)
    tril = jnp.where(tt >= ss, 1.0, 0.0).astype(BF16)
    hi = g.astype(BF16)
    r1 = g - hi.astype(F32)
    mid = r1.astype(BF16)
    lo = (r1 - mid.astype(F32)).astype(BF16)
    dot = functools.partial(jnp.dot, preferred_element_type=F32)
    return dot(tril, hi) + dot(tril, mid) + dot(tril, lo)


def _rows_as_lanes(x, c):
    if c < LANES:
        x = jnp.concatenate([x, jnp.zeros((LANES - c, LANES), F32)], axis=0)
    return x.T


def _dn_body(q_ref, k_ref, v_ref, z_ref, sm_ref, h0_ref, pa_ref, ng_ref, o_ref, ht_ref, h_scr,
             *, c, valid, nh, dk, nsq):
    ci = pl.program_id(1)

    @pl.when(ci == 0)
    def _():
        h_scr[...] = h0_ref[0]

    sm = sm_ref[...]
    pa = pa_ref[...]
    beta_all = _sigmoid(sm)
    g_all = -jnp.exp(pa[0:1]) * _softplus(sm + pa[1:2])
    if valid < c:
        live = lax.broadcasted_iota(jnp.int32, (c, LANES), 0) < valid
        beta_all = jnp.where(live, beta_all, 0.0)
        g_all = jnp.where(live, g_all, 0.0)
    gc_all = _chunk_cumsum(g_all, c)
    gct = _rows_as_lanes(gc_all, c)
    eg_all = jnp.exp(gc_all)
    tt = lax.broadcasted_iota(jnp.int32, (c, c), 0)
    ss = lax.broadcasted_iota(jnp.int32, (c, c), 1)
    strict = tt > ss
    incl = tt >= ss
    scale = dk ** -0.5

    for h in range(nh):
        sl = slice(h * dk, (h + 1) * dk)
        q = q_ref[:, sl]
        k = k_ref[:, sl]
        v = v_ref[:, sl]
        q = q * lax.rsqrt(jnp.sum(q * q, axis=-1, keepdims=True) + EPS) * scale
        k = k * lax.rsqrt(jnp.sum(k * k, axis=-1, keepdims=True) + EPS)
        gcol = gc_all[:, nh + h:nh + h + 1]
        grow = gct[nh + h:nh + h + 1, 0:c]
        diff = gcol - grow
        dec_s = jnp.exp(jnp.where(strict, diff, NEG))
        dec_i = jnp.exp(jnp.where(incl, diff, NEG))
        bcol = beta_all[:, h:h + 1]
        egc = eg_all[:, nh + h:nh + h + 1]
        pw = -(bcol * _bdot_nt(k, k) * dec_s)
        qm = pw
        for _ in range(nsq):
            pw = _bdot(pw, pw)
            qm = qm + pw + _bdot(qm, pw)
        rhs = jnp.concatenate([v * bcol, k * (bcol * egc)], axis=1)
        sol = rhs + _bdot(qm, rhs)
        hh = h_scr[h]
        w = sol[:, :dk] - _bdot(sol[:, dk:], hh)
        qk = _bdot_nt(q, k) * dec_i
        o = _bdot(q, hh) * egc + _bdot(qk, w)
        glast = gc_all[c - 1:c, nh + h:nh + h + 1]
        h_scr[h] = jnp.exp(glast) * hh + _bdot_tn(k * jnp.exp(glast - gcol), w)
        o = o * lax.rsqrt(jnp.mean(o * o, axis=-1, keepdims=True) + EPS) * ng_ref[...]
        o_ref[:, sl] = (o * _silu(z_ref[:, sl])).astype(o_ref.dtype)

    @pl.when(ci == pl.num_programs(1) - 1)
    def _():
        ht_ref[0] = h_scr[...]


def deltanet(conv, seg_a, seg_d, row0, nb, seqlen, c, valid, h0, pa, ng, nh, dk):
    w = nh * dk
    nc = seqlen // c
    r0 = row0 // c
    nsq = 0
    while (2 << nsq) < valid:
        nsq += 1
    loc = lambda s, i: (s * nc + i, 0)
    glob = lambda s, i: (r0 + s * nc + i, 0)
    st = pl.BlockSpec((1, nh, dk, dk), lambda s, i: (s, 0, 0, 0))
    return pl.pallas_call(
        functools.partial(_dn_body, c=c, valid=valid, nh=nh, dk=dk, nsq=nsq), grid=(nb, nc),
        in_specs=[pl.BlockSpec((c, w), loc),
                  pl.BlockSpec((c, w), lambda s, i: (s * nc + i, 1)),
                  pl.BlockSpec((c, w), lambda s, i: (s * nc + i, 2)),
                  pl.BlockSpec((c, w), lambda s, i: (r0 + s * nc + i, 3)),
                  pl.BlockSpec((c, LANES), glob),
                  st,
                  pl.BlockSpec((SUBLANES, LANES), lambda s, i: (0, 0)),
                  pl.BlockSpec((1, dk), lambda s, i: (0, 0))],
        out_specs=[pl.BlockSpec((c, w), loc), st],
        out_shape=[jax.ShapeDtypeStruct((nb * seqlen, w), BF16),
                   jax.ShapeDtypeStruct((nb, nh, dk, dk), F32)],
        scratch_shapes=[pltpu.VMEM((nh, dk, dk), F32)],
        compiler_params=_params(("parallel", "arbitrary"), 32), name="deltanet",
    )(conv, conv, conv, seg_a, seg_d, h0, pa, ng.reshape(1, dk))


def _ssd_body(x_ref, b_ref, c_ref, z_ref, sm_ref, h0_ref, pa_ref, ng_ref, y_ref, ht_ref, h_scr, y_scr,
              *, c, valid, ng, nr, hp, ns, off):
    ci = pl.program_id(1)

    @pl.when(ci == 0)
    def _():
        h_scr[...] = h0_ref[0]

    gw = nr * hp
    shift = hp.bit_length() - 1
    sm = sm_ref[...]
    pa = pa_ref[...]
    dt_all = _softplus(sm + pa[1:2])
    if valid < c:
        dt_all = jnp.where(lax.broadcasted_iota(jnp.int32, (c, LANES), 0) < valid, dt_all, 0.0)
    gc_all = _chunk_cumsum(dt_all * (-jnp.exp(pa[0:1])), c)
    gct = _rows_as_lanes(gc_all, c)
    eg_all = jnp.exp(gc_all)
    tt = lax.broadcasted_iota(jnp.int32, (c, c), 0)
    ss = lax.broadcasted_iota(jnp.int32, (c, c), 1)
    incl = tt >= ss
    lane_head = lax.broadcasted_iota(jnp.int32, (c, gw), 1) >> shift
    row_head = lax.broadcasted_iota(jnp.int32, (gw, ns), 0) >> shift

    for g in range(ng):
        bg = b_ref[:, g * ns:(g + 1) * ns]
        cg = c_ref[:, g * ns:(g + 1) * ns]
        xg = x_ref[:, g * gw:(g + 1) * gw]
        hg = h_scr[g]
        cb = _bdot_nt(cg, bg)
        dtb = jnp.zeros((c, gw), F32)
        egb = jnp.zeros((c, gw), F32)
        kdb = jnp.zeros((c, gw), F32)
        skip = jnp.zeros((1, gw), F32)
        hdec = jnp.zeros((gw, ns), F32)
        for r in range(nr):
            col = off + g * nr + r
            gcol = gc_all[:, col:col + 1]
            glast = gc_all[c - 1:c, col:col + 1]
            seg = lane_head == r
            dtb = jnp.where(seg, dt_all[:, col:col + 1], dtb)
            egb = jnp.where(seg, eg_all[:, col:col + 1], egb)
            kdb = jnp.where(seg, jnp.exp(glast - gcol), kdb)
            skip = jnp.where(seg[0:1], pa[2:3, col:col + 1], skip)
            hdec = jnp.where(row_head == r, jnp.exp(glast), hdec)
        xdt = xg * dtb
        yg = _bdot_nt(cg, hg) * egb + skip * xg
        for r in range(nr):
            col = off + g * nr + r
            lm = jnp.exp(jnp.where(incl, gc_all[:, col:col + 1] - gct[col:col + 1, 0:c], NEG))
            yg = yg + _bdot(cb * lm, jnp.where(lane_head == r, xdt, 0.0))
        h_scr[g] = hdec * hg + _bdot_tn(xdt * kdb, bg)
        y_scr[:, g * gw:(g + 1) * gw] = yg * _silu(z_ref[:, g * gw:(g + 1) * gw])

    y = y_scr[...]
    y = y * lax.rsqrt(jnp.mean(y * y, axis=-1, keepdims=True) + EPS) * ng_ref[...]
    y_ref[...] = y.astype(y_ref.dtype)

    @pl.when(ci == pl.num_programs(1) - 1)
    def _():
        ht_ref[0] = h_scr[...]


def ssd(conv, seg_b, seg_d, row0, nb, seqlen, c, valid, h0, pa, ngain, ng, nr, hp, ns, off):
    inner = ng * nr * hp
    gn = ng * ns
    nc = seqlen // c
    r0 = row0 // c
    loc = lambda s, i: (s * nc + i, 0)
    st = pl.BlockSpec((1, ng, nr * hp, ns), lambda s, i: (s, 0, 0, 0))
    return pl.pallas_call(
        functools.partial(_ssd_body, c=c, valid=valid, ng=ng, nr=nr, hp=hp, ns=ns, off=off), grid=(nb, nc),
        in_specs=[pl.BlockSpec((c, inner), loc),
                  pl.BlockSpec((c, gn), lambda s, i: (s * nc + i, inner // gn)),
                  pl.BlockSpec((c, gn), lambda s, i: (s * nc + i, inner // gn + 1)),
                  pl.BlockSpec((c, inner), lambda s, i: (r0 + s * nc + i, 0)),
                  pl.BlockSpec((c, LANES), lambda s, i: (r0 + s * nc + i, 0)),
                  st,
                  pl.BlockSpec((SUBLANES, LANES), lambda s, i: (0, 0)),
                  pl.BlockSpec((1, inner), lambda s, i: (0, 0))],
        out_specs=[pl.BlockSpec((c, inner), loc), st],
        out_shape=[jax.ShapeDtypeStruct((nb * seqlen, inner), BF16),
                   jax.ShapeDtypeStruct((nb, ng, nr * hp, ns), F32)],
        scratch_shapes=[pltpu.VMEM((ng, nr * hp, ns), F32), pltpu.VMEM((c, inner), F32)],
        compiler_params=_params(("parallel", "arbitrary"), 32), name="ssd",
    )(conv, conv, conv, seg_b, seg_d, h0, pa, ngain.reshape(1, inner))


def _ca_body(q_ref, k_ref, v_ref, o_ref, *, nh, dh):
    scale = dh ** -0.5
    for h in range(nh):
        sl = slice(h * dh, (h + 1) * dh)
        s = _bdot_nt(q_ref[:, sl], k_ref[0, :, sl]) * scale
        e = jnp.exp(s - jnp.max(s, axis=-1, keepdims=True))
        p = e / jnp.sum(e, axis=-1, keepdims=True)
        o_ref[:, sl] = _bdot(p, v_ref[0, :, sl]).astype(o_ref.dtype)


def cross_attention(q2d, row0, nb, seqlen, mem_k, mem_v, nh, dh, tl=512):
    tl = _tile(seqlen, tl)
    nl = seqlen // tl
    r0 = row0 // tl
    mem, wd = mem_k.shape[1], nh * dh
    kv = pl.BlockSpec((1, mem, wd), lambda s, i: (s, 0, 0))
    return pl.pallas_call(
        functools.partial(_ca_body, nh=nh, dh=dh), grid=(nb, nl),
        in_specs=[pl.BlockSpec((tl, wd), lambda s, i: (r0 + s * nl + i, 0)), kv, kv],
        out_specs=pl.BlockSpec((tl, wd), lambda s, i: (s * nl + i, 0)),
        out_shape=jax.ShapeDtypeStruct((nb * seqlen, wd), BF16),
        compiler_params=_params(("parallel", "parallel"), 32), name="cross_attention",
    )(q2d, mem_k, mem_v)


def _extract_top(x, n):
    out = []
    for it in range(n):
        m = jnp.max(x, axis=0, keepdims=True)
        out.append(m)
        if it + 1 < n:
            x = jnp.where(x == m, -jnp.inf, x)
    return out


def _peer_route_body(q_ref, keys_ref, s1_ref, e1_ref, s2_ref, e2_ref, thr_ref, *, nh, dq, topk):
    for h in range(nh):
        s1 = _bdot_nt(keys_ref[2 * h], q_ref[:, (2 * h) * dq:(2 * h + 1) * dq])
        s2 = _bdot_nt(keys_ref[2 * h + 1], q_ref[:, (2 * h + 1) * dq:(2 * h + 2) * dq])
        v1 = _extract_top(s1, topk)
        v2 = _extract_top(s2, topk)
        v2s = jnp.concatenate(v2, axis=0)
        cand = jnp.concatenate([v1[a] + v2s for a in range(topk)], axis=0)
        tops = _extract_top(cand, topk)
        zsum = jnp.exp(tops[0] - tops[0])
        for a in range(1, topk):
            zsum = zsum + jnp.exp(tops[a] - tops[0])
        s1_ref[h] = s1
        s2_ref[h] = s2
        e1_ref[h] = jnp.exp(s1 - v1[0]) / zsum
        e2_ref[h] = jnp.exp(s2 - v2[0])
        thr_ref[h:h + 1, :] = tops[topk - 1]


def peer_route(q, keys, nh, nk, dq, tm=256):
    m = q.shape[0]
    tm = _tile(m, tm)
    big = pl.BlockSpec((nh, nk, tm), lambda i: (0, 0, i))
    shape = jax.ShapeDtypeStruct((nh, nk, m), F32)
    return pl.pallas_call(
        functools.partial(_peer_route_body, nh=nh, dq=dq, topk=PEER_TOPK), grid=(m // tm,),
        in_specs=[pl.BlockSpec((tm, 2 * nh * dq), lambda i: (i, 0)),
                  pl.BlockSpec((2 * nh, nk, dq), lambda i: (0, 0, 0))],
        out_specs=[big, big, big, big, pl.BlockSpec((nh, tm), lambda i: (0, i))],
        out_shape=[shape, shape, shape, shape, jax.ShapeDtypeStruct((nh, m), F32)],
        compiler_params=_params(("parallel",), 32), name="peer_route",
    )(q, keys)


def _peer_body(hft_ref, u_ref, vt_ref, s1_ref, e1_ref, s2_ref, e2_ref, thr_ref, ot_ref, st_scr, act_scr,
               *, nh, nk, nrow, dchunk):
    ei = pl.program_id(1)
    d, tm = hft_ref.shape

    @pl.when(ei == 0)
    def _():
        ot_ref[...] = jnp.zeros_like(ot_ref)
        st_scr[1] = jnp.zeros(st_scr.shape[1:], F32)
        act_scr[0] = jnp.zeros(act_scr.shape[1:], BF16)

    def step(a):
        b = 1 - a

        def scores(r):
            rows = slice(r * nk, (r + 1) * nk)
            st_scr[a, rows, :] = jnp.dot(u_ref[rows, :], hft_ref[...], preferred_element_type=F32)

        def apply_values(j):
            ds = slice(j * dchunk, (j + 1) * dchunk)
            ot_ref[ds, :] += jnp.dot(vt_ref[ds, :], act_scr[a], preferred_element_type=F32)

        def gate_block(r, tb):
            ls = slice(tb * LANES, (tb + 1) * LANES)
            rows = slice(r * nk, (r + 1) * nk)
            gate = jnp.zeros((nk, LANES), F32)
            for h in range(nh):
                hit = (s2_ref[h, :, ls] + s1_ref[r, h:h + 1, ls]) >= thr_ref[h:h + 1, ls]
                gate = gate + jnp.where(hit, e2_ref[h, :, ls] * e1_ref[r, h:h + 1, ls], 0.0)
            act_scr[b, rows, ls] = (_gelu_tanh(st_scr[b, rows, ls]) * gate).astype(BF16)

        blocks = [(r, tb) for r in range(nrow) for tb in range(tm // LANES)]
        nd = d // dchunk
        per = -(-len(blocks) // nd)
        every = max(nd // nrow, 1)
        for j in range(max(nd, nrow * every)):
            if j % every == 0 and j // every < nrow:
                scores(j // every)
            if j < nd:
                apply_values(j)
            for r, tb in blocks[j * per:(j + 1) * per]:
                gate_block(r, tb)

    @pl.when(ei % 2 == 0)
    def _():
        step(0)

    @pl.when(ei % 2 == 1)
    def _():
        step(1)


def peer_experts(hft, u, vt, s1g, e1g, s2, e2, thr, nh, nk, tm=512, nrow=4):
    d, m = hft.shape
    tm = _tile(m, tm)
    et = nrow * nk
    once = pl.Buffered(1)
    ne = nk // nrow
    tile = lambda e, lag: jnp.clip(e - lag, 0, ne - 1)
    return pl.pallas_call(
        functools.partial(_peer_body, nh=nh, nk=nk, nrow=nrow, dchunk=_tile(d, 512)), grid=(m // tm, ne + 2),
        in_specs=[pl.BlockSpec((d, tm), lambda t, e: (0, t), pipeline_mode=once),
                  pl.BlockSpec((et, d), lambda t, e: (tile(e, 0), 0)),
                  pl.BlockSpec((d, et), lambda t, e: (0, tile(e, 2))),
                  pl.BlockSpec((nrow, nh, tm), lambda t, e: (tile(e, 1), 0, t)),
                  pl.BlockSpec((nrow, nh, tm), lambda t, e: (tile(e, 1), 0, t)),
                  pl.BlockSpec((nh, nk, tm), lambda t, e: (0, 0, t), pipeline_mode=once),
                  pl.BlockSpec((nh, nk, tm), lambda t, e: (0, 0, t), pipeline_mode=once),
                  pl.BlockSpec((nh, tm), lambda t, e: (0, t))],
        out_specs=pl.BlockSpec((d, tm), lambda t, e: (0, t)),
        out_shape=jax.ShapeDtypeStruct((d, m), F32),
        scratch_shapes=[pltpu.VMEM((2, et, tm), F32), pltpu.VMEM((2, et, tm), BF16)],
        compiler_params=_params(("parallel", "arbitrary"), 56), name="peer_experts",
    )(hft, u, vt, s1g, e1g, s2, e2, thr)


def _lane_row(nrows, pieces):
    out = jnp.zeros((SUBLANES, LANES), F32)
    for row, off, vec in pieces:
        out = out.at[row, off:off + vec.shape[0]].set(vec.astype(F32))
    return out


def kernel(x_prompt, x_sample, cache_mem_k, cache_mem_v, state_dn_conv, state_dn_rec, state_ssm_conv,
           state_ssm_rec, mem_prompt, norm_mix_g, w_in, dn_conv_w, dn_a_log, dn_dt_bias, dn_norm_g, dn_w_out,
           ssm_conv_w, ssm_conv_b, ssm_a_log, ssm_dt_bias, ssm_d, ssm_norm_g, ssm_w_out, w_o, norm_ca_g,
           ca_w_q, ca_w_k, ca_w_v, ca_w_o, norm_ffn_g, peer_w_q, peer_keys, peer_u, peer_v, final_norm_g):
    depth = w_in.shape[0]
    bp, lp, d = x_prompt.shape
    bs, ls, _ = x_sample.shape
    _, _, dnh, dk, _ = state_dn_rec.shape
    _, _, sh, hp, ns = state_ssm_rec.shape
    dnw = dnh * dk
    inner = sh * hp
    sconv = state_ssm_conv.shape[-1]
    sg = (sconv - inner) // (2 * ns)
    sr = sh // sg
    _, _, mem, cah, cad = cache_mem_k.shape
    caw = cah * cad
    _, pnh, _, nk, dq = peer_keys.shape
    lpad = -(-ls // SUBLANES) * SUBLANES
    assert lp % CHUNK == 0 and lpad <= CHUNK and dk == LANES and ns == LANES and nk == LANES and dq == LANES
    assert 2 * dnh + sh <= LANES and hp & (hp - 1) == 0
    mp, ms = bp * lp, bs * lpad
    ssm_off = 2 * dnh

    xs = jnp.pad(x_sample, ((0, 0), (0, lpad - ls), (0, 0)))
    x = jnp.concatenate([x_prompt.reshape(mp, d), xs.reshape(ms, d)], axis=0)

    outs = [[] for _ in range(10)]
    for l in range(depth):
        wl = w_in[l]
        o1 = 3 * dnw + dnw
        o2 = o1 + 2 * dnh
        o3 = o2 + inner
        o4 = o3 + sconv
        o5 = o4 + sh
        w_a = wl[:, :o1].astype(BF16)
        w_b = wl[:, o2:o4].astype(BF16)
        w_c = wl[:, o5:].astype(BF16)
        w_d = jnp.concatenate([wl[:, o1:o2], wl[:, o4:o5],
                               jnp.zeros((d, LANES - 2 * dnh - sh), F32)], axis=1).astype(BF16)

        hn = rmsnorm(x, norm_mix_g[l], BF16)
        seg_a = matmul(hn, w_a)
        seg_b = matmul(hn, w_b)
        seg_c = matmul(hn, w_c)
        seg_d = matmul(hn, w_d)

        dn_pa = _lane_row(SUBLANES, [(0, dnh, dn_a_log[l]), (1, dnh, dn_dt_bias[l])])
        ssm_pa = _lane_row(SUBLANES, [(0, ssm_off, ssm_a_log[l]), (1, ssm_off, ssm_dt_bias[l]), (2, ssm_off, ssm_d[l])])
        groups = (
            (0, bp, lp, CHUNK, CHUNK, jnp.zeros((bp, CONV_W - 1, 3 * dnw), F32), jnp.zeros((bp, dnh, dk, dk), F32),
             jnp.zeros((bp, CONV_W - 1, sconv), F32), jnp.zeros((bp, sg, sr * hp, ns), F32), 512),
            (mp, bs, lpad, lpad, ls, state_dn_conv[l], state_dn_rec[l],
             state_ssm_conv[l], state_ssm_rec[l].reshape(bs, sg, sr * hp, ns), lpad),
        )
        o_dn, o_ssm, dn_h, ssm_h = [], [], [], []
        for row0, nb, sl, c, valid, dc0, dh0, sc0, sh0, rb in groups:
            cdn = conv_silu(seg_a, row0, nb, sl, 0, 3 * dnw, dc0, dn_conv_w[l], jnp.zeros((3 * dnw,), F32), rb)
            od, hd = deltanet(cdn, seg_a, seg_d, row0, nb, sl, c, valid, dh0, dn_pa, dn_norm_g[l], dnh, dk)
            cs = conv_silu(seg_b, row0, nb, sl, inner, sconv, sc0, ssm_conv_w[l], ssm_conv_b[l], rb)
            osd, hsd = ssd(cs, seg_b, seg_d, row0, nb, sl, c, valid, sh0, ssm_pa, ssm_norm_g[l],
                           sg, sr, hp, ns, ssm_off)
            o_dn.append(od)
            o_ssm.append(osd)
            dn_h.append(hd)
            ssm_h.append(hsd.reshape(nb, sh, hp, ns))

        mixed = gated_mix(jnp.concatenate(o_dn, axis=0), dn_w_out[l].astype(BF16),
                          jnp.concatenate(o_ssm, axis=0), ssm_w_out[l].astype(BF16), seg_c)
        x1 = matmul(mixed, w_o[l].astype(BF16), res=x)

        hc = rmsnorm(x1, norm_ca_g[l], BF16)
        qc = matmul(hc, ca_w_q[l].astype(BF16))
        memp = mem_prompt.reshape(bp * mem, d)
        mk = matmul(memp, ca_w_k[l].astype(BF16))
        mv = matmul(memp, ca_w_v[l].astype(BF16))
        oc = jnp.concatenate([
            cross_attention(qc, 0, bp, lp, mk.reshape(bp, mem, caw), mv.reshape(bp, mem, caw), cah, cad),
            cross_attention(qc, mp, bs, lpad, cache_mem_k[l].reshape(bs, mem, caw),
                            cache_mem_v[l].reshape(bs, mem, caw), cah, cad)], axis=0)
        x2 = matmul(oc, ca_w_o[l].astype(BF16), res=x1)

        hf, hft = rmsnorm(x2, norm_ffn_g[l], BF16, with_transpose=True)
        pq = matmul(hf, peer_w_q[l].astype(BF16))
        s1, e1, s2, e2, thr = peer_route(pq, peer_keys[l].reshape(2 * pnh, nk, dq).astype(BF16), pnh, nk, dq)
        pot = peer_experts(hft, peer_u[l].astype(BF16), peer_v[l].T.astype(BF16),
                           s1.transpose(1, 0, 2), e1.transpose(1, 0, 2), s2, e2, thr, pnh, nk)
        x = add_rows(x2, pot, final_norm_g if l == depth - 1 else None)

        def tail(seg, row0, nb, sl, nvalid, lo, hi, state0):
            u = seg[row0:row0 + nb * sl].reshape(nb, sl, -1)[:, max(nvalid - (CONV_W - 1), 0):nvalid, lo:hi]
            if nvalid < CONV_W - 1:
                u = jnp.concatenate([state0, u], axis=1)[:, -(CONV_W - 1):]
            return u

        outs[0].append(mk.reshape(bp, mem, cah, cad))
        outs[1].append(mv.reshape(bp, mem, cah, cad))
        outs[2].append(tail(seg_a, 0, bp, lp, lp, 0, 3 * dnw, groups[0][5]))
        outs[3].append(dn_h[0])
        outs[4].append(tail(seg_b, 0, bp, lp, lp, inner, inner + sconv, groups[0][7]))
        outs[5].append(ssm_h[0])
        outs[6].append(tail(seg_a, mp, bs, lpad, ls, 0, 3 * dnw, state_dn_conv[l]))
        outs[7].append(dn_h[1])
        outs[8].append(tail(seg_b, mp, bs, lpad, ls, inner, inner + sconv, state_ssm_conv[l]))
        outs[9].append(ssm_h[1])

    y_prompt = x[:mp].reshape(bp, lp, d)
    y_sample = x[mp:].reshape(bs, lpad, d)[:, :ls]
    return (y_prompt, y_sample) + tuple(jnp.stack(o) for o in outs)
```

```python
import functools

import jax
import jax.numpy as jnp
from jax import lax
from jax.experimental import pallas as pl
from jax.experimental.pallas import tpu as pltpu

F32 = jnp.float32
BF16 = jnp.bfloat16

EPS = 1e-6
CONV_W = 4
CHUNK = 64
PEER_TOPK = 16
LANES = 128
SUBLANES = 8
NEG = -1e30
MIB = 1 << 20


def _params(sem, vmem_mib):
    return pltpu.CompilerParams(dimension_semantics=sem, vmem_limit_bytes=vmem_mib * MIB)


def _tile(n, pref):
    t = min(pref, n)
    while n % t:
        t //= 2
    return t


def _sigmoid(x):
    return 1.0 / (1.0 + jnp.exp(-x))


def _silu(x):
    return x * _sigmoid(x)


def _softplus(x):
    return jnp.maximum(x, 0.0) + jnp.log1p(jnp.exp(-jnp.abs(x)))


def _gelu_tanh(x):
    return x * (0.5 * (1.0 + jnp.tanh(0.7978845608028654 * (x + 0.044715 * (x * x * x)))))


def _bdot(a, b):
    return jnp.dot(a.astype(BF16), b.astype(BF16), preferred_element_type=F32)


def _bdot_nt(a, b):
    return lax.dot_general(a.astype(BF16), b.astype(BF16), (((1,), (1,)), ((), ())),
                           preferred_element_type=F32)


def _bdot_tn(a, b):
    return lax.dot_general(a.astype(BF16), b.astype(BF16), (((0,), (0,)), ((), ())),
                           preferred_element_type=F32)


def _rmsnorm_body(x_ref, g_ref, o_ref):
    x = x_ref[...]
    y = x * lax.rsqrt(jnp.mean(x * x, axis=-1, keepdims=True) + EPS)
    o_ref[...] = (y * g_ref[...]).astype(o_ref.dtype)


def _addnorm_body(x_ref, pt_ref, g_ref, o_ref):
    x = x_ref[...] + pt_ref[...].T
    y = x * lax.rsqrt(jnp.mean(x * x, axis=-1, keepdims=True) + EPS)
    o_ref[...] = (y * g_ref[...]).astype(o_ref.dtype)


def _add_body(x_ref, pt_ref, o_ref):
    o_ref[...] = x_ref[...] + pt_ref[...].T


def _rmsnorm_both_body(x_ref, g_ref, o_ref, ot_ref):
    x = x_ref[...]
    y = x * lax.rsqrt(jnp.mean(x * x, axis=-1, keepdims=True) + EPS) * g_ref[...]
    o_ref[...] = y.astype(o_ref.dtype)
    ot_ref[...] = y.T.astype(ot_ref.dtype)


def rmsnorm(x, g, out_dtype, tm=256, with_transpose=False):
    m, d = x.shape
    tm = _tile(m, tm)
    row = pl.BlockSpec((tm, d), lambda i: (i, 0))
    if with_transpose:
        body, outs = _rmsnorm_both_body, [row, pl.BlockSpec((d, tm), lambda i: (0, i))]
        shapes = [jax.ShapeDtypeStruct((m, d), out_dtype), jax.ShapeDtypeStruct((d, m), out_dtype)]
    else:
        body, outs, shapes = _rmsnorm_body, row, jax.ShapeDtypeStruct((m, d), out_dtype)
    return pl.pallas_call(
        body, grid=(m // tm,),
        in_specs=[row, pl.BlockSpec((1, d), lambda i: (0, 0))],
        out_specs=outs, out_shape=shapes,
        compiler_params=_params(("parallel",), 32), name="rmsnorm",
    )(x, g.reshape(1, d))


def add_rows(x, pt, g=None, tm=256):
    m, d = x.shape
    tm = _tile(m, tm)
    row = pl.BlockSpec((tm, d), lambda i: (i, 0))
    col = pl.BlockSpec((d, tm), lambda i: (0, i))
    if g is None:
        body, ins, args = _add_body, [row, col], (x, pt)
    else:
        body, ins, args = _addnorm_body, [row, col, pl.BlockSpec((1, d), lambda i: (0, 0))], (x, pt, g.reshape(1, d))
    return pl.pallas_call(
        body, grid=(m // tm,), in_specs=ins, out_specs=row,
        out_shape=jax.ShapeDtypeStruct((m, d), F32),
        compiler_params=_params(("parallel",), 40), name="add_norm",
    )(*args)


def _mm_body(a_ref, w_ref, *rest, has_res):
    o_ref = rest[-1]
    acc = jnp.dot(a_ref[...].astype(BF16), w_ref[...], preferred_element_type=F32)
    if has_res:
        acc = rest[0][...] + acc
    o_ref[...] = acc.astype(o_ref.dtype)


def matmul(a, w, res=None, out_dtype=F32, tm=512, tn=1024):
    m, k = a.shape
    n = w.shape[1]
    tm, tn = _tile(m, tm), _tile(n, tn)
    ins = [pl.BlockSpec((tm, k), lambda j, i: (i, 0)), pl.BlockSpec((k, tn), lambda j, i: (0, j))]
    args = [a, w]
    if res is not None:
        ins.append(pl.BlockSpec((tm, tn), lambda j, i: (i, j)))
        args.append(res)
    return pl.pallas_call(
        functools.partial(_mm_body, has_res=res is not None), grid=(n // tn, m // tm),
        in_specs=ins, out_specs=pl.BlockSpec((tm, tn), lambda j, i: (i, j)),
        out_shape=jax.ShapeDtypeStruct((m, n), out_dtype),
        compiler_params=_params(("parallel", "parallel"), 48), name="matmul",
    )(*args)


def _mix_body(a1_ref, w1_ref, a2_ref, w2_ref, g1_ref, g2_ref, o_ref):
    y1 = jnp.dot(a1_ref[...], w1_ref[...], preferred_element_type=F32)
    y2 = jnp.dot(a2_ref[...], w2_ref[...], preferred_element_type=F32)
    o_ref[...] = (_sigmoid(g1_ref[...]) * y1 + _sigmoid(g2_ref[...]) * y2).astype(o_ref.dtype)


def gated_mix(a1, w1, a2, w2, gates, tm=512, tn=1024):
    m, k1 = a1.shape
    k2 = a2.shape[1]
    n = w1.shape[1]
    tm, tn = _tile(m, tm), _tile(n, tn)
    nj = n // tn
    return pl.pallas_call(
        _mix_body, grid=(nj, m // tm),
        in_specs=[pl.BlockSpec((tm, k1), lambda j, i: (i, 0)), pl.BlockSpec((k1, tn), lambda j, i: (0, j)),
                  pl.BlockSpec((tm, k2), lambda j, i: (i, 0)), pl.BlockSpec((k2, tn), lambda j, i: (0, j)),
                  pl.BlockSpec((tm, tn), lambda j, i: (i, j)), pl.BlockSpec((tm, tn), lambda j, i: (i, j + nj))],
        out_specs=pl.BlockSpec((tm, tn), lambda j, i: (i, j)),
        out_shape=jax.ShapeDtypeStruct((m, n), BF16),
        compiler_params=_params(("parallel", "parallel"), 48), name="gated_mix",
    )(a1, w1, a2, w2, gates, gates)


def _conv_silu_rows(x_ref, st_ref, w_ref, b_ref, buf, cv, c, ci):
    k = CONV_W - 1

    @pl.when(ci == 0)
    def _():
        buf[SUBLANES - k:SUBLANES, :] = st_ref[0]

    buf[SUBLANES:SUBLANES + c, :] = x_ref[...]
    base = SUBLANES - k
    acc = buf[base:base + c, :] * w_ref[0:1, :]
    for j in range(1, CONV_W):
        acc = acc + buf[base + j:base + j + c, :] * w_ref[j:j + 1, :]
    if b_ref is not None:
        acc = acc + b_ref[...]
    cv[...] = _silu(acc)
    buf[0:SUBLANES, :] = buf[c:c + SUBLANES, :]


def _chunk_cumsum(g, c):
    tt = lax.broadcasted_iota(jnp.int32, (c, c), 0)
    ss = lax.broadcasted_iota(jnp.int32, (c, c), 1)
    tril = jnp.where(tt >= ss, 1.0, 0.0).astype(BF16)
    hi = g.astype(BF16)
    r1 = g - hi.astype(F32)
    mid = r1.astype(BF16)
    lo = (r1 - mid.astype(F32)).astype(BF16)
    dot = functools.partial(jnp.dot, preferred_element_type=F32)
    return dot(tril, hi) + dot(tril, mid) + dot(tril, lo)


def _rows_as_lanes(x, c):
    if c < LANES:
        x = jnp.concatenate([x, jnp.zeros((LANES - c, LANES), F32)], axis=0)
    return x.T


def _dn_body(x_ref, z_ref, sm_ref, cst_ref, cw_ref, h0_ref, pa_ref, ng_ref, o_ref, ht_ref, h_scr, buf, cv,
             *, c, valid, nh, dk, nsq):
    ci = pl.program_id(1)

    @pl.when(ci == 0)
    def _():
        h_scr[...] = h0_ref[0]

    _conv_silu_rows(x_ref, cst_ref, cw_ref, None, buf, cv, c, ci)
    w = nh * dk
    sm = sm_ref[...]
    pa = pa_ref[...]
    beta_all = _sigmoid(sm)
    g_all = -jnp.exp(pa[0:1]) * _softplus(sm + pa[1:2])
    if valid < c:
        live = lax.broadcasted_iota(jnp.int32, (c, LANES), 0) < valid
        beta_all = jnp.where(live, beta_all, 0.0)
        g_all = jnp.where(live, g_all, 0.0)
    gc_all = _chunk_cumsum(g_all, c)
    gct = _rows_as_lanes(gc_all, c)
    eg_all = jnp.exp(gc_all)
    tt = lax.broadcasted_iota(jnp.int32, (c, c), 0)
    ss = lax.broadcasted_iota(jnp.int32, (c, c), 1)
    strict = tt > ss
    incl = tt >= ss
    scale = dk ** -0.5
    hs = range(nh)

    q, k = [], []
    for h in hs:
        qh = cv[:, h * dk:(h + 1) * dk]
        kh = cv[:, w + h * dk:w + (h + 1) * dk]
        q.append(qh * lax.rsqrt(jnp.sum(qh * qh, axis=-1, keepdims=True) + EPS) * scale)
        k.append(kh * lax.rsqrt(jnp.sum(kh * kh, axis=-1, keepdims=True) + EPS))
    hh = [h_scr[h] for h in hs]
    kk = [_bdot_nt(k[h], k[h]) for h in hs]
    qk = [_bdot_nt(q[h], k[h]) for h in hs]
    qh0 = [_bdot(q[h], hh[h]) for h in hs]
    gcol = [gc_all[:, nh + h:nh + h + 1] for h in hs]
    diff = [gcol[h] - gct[nh + h:nh + h + 1, 0:c] for h in hs]
    bcol = [beta_all[:, h:h + 1] for h in hs]
    egc = [eg_all[:, nh + h:nh + h + 1] for h in hs]
    glast = [gc_all[c - 1:c, nh + h:nh + h + 1] for h in hs]
    qm = [-(bcol[h] * kk[h] * jnp.exp(jnp.where(strict, diff[h], NEG))) for h in hs]
    pw = [_bdot(qm[h], qm[h]) for h in hs] if nsq else None
    for it in range(nsq):
        t = [_bdot(qm[h], pw[h]) for h in hs]
        nxt = [_bdot(pw[h], pw[h]) for h in hs] if it + 1 < nsq else None
        qm = [qm[h] + pw[h] + t[h] for h in hs]
        pw = nxt
    rhs = [jnp.concatenate([cv[:, 2 * w + h * dk:2 * w + (h + 1) * dk] * bcol[h], k[h] * (bcol[h] * egc[h])], axis=1)
           for h in hs]
    sol = [rhs[h] + _bdot(qm[h], rhs[h]) for h in hs]
    wv = [sol[h][:, :dk] - _bdot(sol[h][:, dk:], hh[h]) for h in hs]
    o = [qh0[h] * egc[h] + _bdot(qk[h] * jnp.exp(jnp.where(incl, diff[h], NEG)), wv[h]) for h in hs]
    hn = [jnp.exp(glast[h]) * hh[h] + _bdot_tn(k[h] * jnp.exp(glast[h] - gcol[h]), wv[h]) for h in hs]
    for h in hs:
        h_scr[h] = hn[h]
        oh = o[h] * lax.rsqrt(jnp.mean(o[h] * o[h], axis=-1, keepdims=True) + EPS) * ng_ref[...]
        o_ref[:, h * dk:(h + 1) * dk] = (oh * _silu(z_ref[:, h * dk:(h + 1) * dk])).astype(o_ref.dtype)

    @pl.when(ci == pl.num_programs(1) - 1)
    def _():
        ht_ref[0] = h_scr[...]


def deltanet(seg_a, seg_d, row0, nb, seqlen, c, valid, conv0, conv_w, h0, pa, ng, nh, dk):
    w = nh * dk
    nc = seqlen // c
    r0 = row0 // c
    nsq = 0
    while (2 << nsq) < valid:
        nsq += 1
    st = pl.BlockSpec((1, nh, dk, dk), lambda s, i: (s, 0, 0, 0))
    return pl.pallas_call(
        functools.partial(_dn_body, c=c, valid=valid, nh=nh, dk=dk, nsq=nsq), grid=(nb, nc),
        in_specs=[pl.BlockSpec((c, 3 * w), lambda s, i: (r0 + s * nc + i, 0)),
                  pl.BlockSpec((c, w), lambda s, i: (r0 + s * nc + i, 3)),
                  pl.BlockSpec((c, LANES), lambda s, i: (r0 + s * nc + i, 0)),
                  pl.BlockSpec((1, CONV_W - 1, 3 * w), lambda s, i: (s, 0, 0)),
                  pl.BlockSpec((CONV_W, 3 * w), lambda s, i: (0, 0)),
                  st,
                  pl.BlockSpec((SUBLANES, LANES), lambda s, i: (0, 0)),
                  pl.BlockSpec((1, dk), lambda s, i: (0, 0))],
        out_specs=[pl.BlockSpec((c, w), lambda s, i: (s * nc + i, 0)), st],
        out_shape=[jax.ShapeDtypeStruct((nb * seqlen, w), BF16),
                   jax.ShapeDtypeStruct((nb, nh, dk, dk), F32)],
        scratch_shapes=[pltpu.VMEM((nh, dk, dk), F32), pltpu.VMEM((c + SUBLANES, 3 * w), F32),
                        pltpu.VMEM((c, 3 * w), F32)],
        compiler_params=_params(("parallel", "arbitrary"), 40), name="deltanet",
    )(seg_a, seg_a, seg_d, conv0, conv_w, h0, pa, ng.reshape(1, dk))


def _ssd_body(x_ref, z_ref, sm_ref, cst_ref, cw_ref, cb_ref, h0_ref, pa_ref, ng_ref, y_ref, ht_ref,
              h_scr, buf, cv, y_scr, *, c, valid, ng, nr, hp, ns, off):
    ci = pl.program_id(1)

    @pl.when(ci == 0)
    def _():
        h_scr[...] = h0_ref[0]

    _conv_silu_rows(x_ref, cst_ref, cw_ref, cb_ref, buf, cv, c, ci)
    gw = nr * hp
    inner = ng * gw
    shift = hp.bit_length() - 1
    sm = sm_ref[...]
    pa = pa_ref[...]
    dt_all = _softplus(sm + pa[1:2])
    if valid < c:
        dt_all = jnp.where(lax.broadcasted_iota(jnp.int32, (c, LANES), 0) < valid, dt_all, 0.0)
    gc_all = _chunk_cumsum(dt_all * (-jnp.exp(pa[0:1])), c)
    gct = _rows_as_lanes(gc_all, c)
    eg_all = jnp.exp(gc_all)
    tt = lax.broadcasted_iota(jnp.int32, (c, c), 0)
    ss = lax.broadcasted_iota(jnp.int32, (c, c), 1)
    incl = tt >= ss
    lane_head = lax.broadcasted_iota(jnp.int32, (c, gw), 1) >> shift
    lane_head1 = lax.broadcasted_iota(jnp.int32, (1, gw), 1) >> shift
    row_head = lax.broadcasted_iota(jnp.int32, (gw, ns), 0) >> shift
    gs = range(ng)

    bg = [cv[:, inner + g * ns:inner + (g + 1) * ns] for g in gs]
    cg = [cv[:, inner + (ng + g) * ns:inner + (ng + g + 1) * ns] for g in gs]
    xg = [cv[:, g * gw:(g + 1) * gw] for g in gs]
    hg = [h_scr[g] for g in gs]
    cb = [_bdot_nt(cg[g], bg[g]) for g in gs]
    ch = [_bdot_nt(cg[g], hg[g]) for g in gs]
    xdt, xkd, yg, hdec = [], [], [], []
    for g in gs:
        dtb = jnp.zeros((c, gw), F32)
        egb = jnp.zeros((c, gw), F32)
        kdb = jnp.zeros((c, gw), F32)
        skip = jnp.zeros((1, gw), F32)
        hd = jnp.zeros((gw, ns), F32)
        for r in range(nr):
            col = off + g * nr + r
            gcol = gc_all[:, col:col + 1]
            glast = gc_all[c - 1:c, col:col + 1]
            seg = lane_head == r
            dtb = jnp.where(seg, dt_all[:, col:col + 1], dtb)
            egb = jnp.where(seg, eg_all[:, col:col + 1], egb)
            kdb = jnp.where(seg, jnp.exp(glast - gcol), kdb)
            skip = jnp.where(lane_head1 == r, pa[2:3, col:col + 1], skip)
            hd = jnp.where(row_head == r, jnp.exp(glast), hd)
        xdt.append(xg[g] * dtb)
        xkd.append(xdt[g] * kdb)
        yg.append(ch[g] * egb + skip * xg[g])
        hdec.append(hd)
    for r in range(nr):
        part = []
        for g in gs:
            col = off + g * nr + r
            lm = jnp.exp(jnp.where(incl, gc_all[:, col:col + 1] - gct[col:col + 1, 0:c], NEG))
            part.append(_bdot(cb[g] * lm, jnp.where(lane_head == r, xdt[g], 0.0)))
        yg = [yg[g] + part[g] for g in gs]
    hn = [hdec[g] * hg[g] + _bdot_tn(xkd[g], bg[g]) for g in gs]
    for g in gs:
        h_scr[g] = hn[g]
        y_scr[:, g * gw:(g + 1) * gw] = yg[g] * _silu(z_ref[:, g * gw:(g + 1) * gw])

    y = y_scr[...]
    y = y * lax.rsqrt(jnp.mean(y * y, axis=-1, keepdims=True) + EPS) * ng_ref[...]
    y_ref[...] = y.astype(y_ref.dtype)

    @pl.when(ci == pl.num_programs(1) - 1)
    def _():
        ht_ref[0] = h_scr[...]


def ssd(seg_b, seg_d, row0, nb, seqlen, c, valid, conv0, conv_w, conv_b, h0, pa, ngain, ng, nr, hp, ns, off):
    inner = ng * nr * hp
    sconv = inner + 2 * ng * ns
    nc = seqlen // c
    r0 = row0 // c
    glob = lambda s, i: (r0 + s * nc + i, 0)
    st = pl.BlockSpec((1, ng, nr * hp, ns), lambda s, i: (s, 0, 0, 0))
    return pl.pallas_call(
        functools.partial(_ssd_body, c=c, valid=valid, ng=ng, nr=nr, hp=hp, ns=ns, off=off), grid=(nb, nc),
        in_specs=[pl.BlockSpec((c, sconv), glob),
                  pl.BlockSpec((c, inner), lambda s, i: (r0 + s * nc + i, sconv // inner)),
                  pl.BlockSpec((c, LANES), glob),
                  pl.BlockSpec((1, CONV_W - 1, sconv), lambda s, i: (s, 0, 0)),
                  pl.BlockSpec((CONV_W, sconv), lambda s, i: (0, 0)),
                  pl.BlockSpec((1, sconv), lambda s, i: (0, 0)),
                  st,
                  pl.BlockSpec((SUBLANES, LANES), lambda s, i: (0, 0)),
                  pl.BlockSpec((1, inner), lambda s, i: (0, 0))],
        out_specs=[pl.BlockSpec((c, inner), lambda s, i: (s * nc + i, 0)), st],
        out_shape=[jax.ShapeDtypeStruct((nb * seqlen, inner), BF16),
                   jax.ShapeDtypeStruct((nb, ng, nr * hp, ns), F32)],
        scratch_shapes=[pltpu.VMEM((ng, nr * hp, ns), F32), pltpu.VMEM((c + SUBLANES, sconv), F32),
                        pltpu.VMEM((c, sconv), F32), pltpu.VMEM((c, inner), F32)],
        compiler_params=_params(("parallel", "arbitrary"), 40), name="ssd",
    )(seg_b, seg_b, seg_d, conv0, conv_w, conv_b.reshape(1, sconv), h0, pa, ngain.reshape(1, inner))


def _ca_body(q_ref, k_ref, v_ref, o_ref, *, nh, dh):
    scale = dh ** -0.5
    for h in range(nh):
        sl = slice(h * dh, (h + 1) * dh)
        s = _bdot_nt(q_ref[:, sl], k_ref[0, :, sl]) * scale
        e = jnp.exp(s - jnp.max(s, axis=-1, keepdims=True))
        p = e / jnp.sum(e, axis=-1, keepdims=True)
        o_ref[:, sl] = _bdot(p, v_ref[0, :, sl]).astype(o_ref.dtype)


def cross_attention(q2d, row0, nb, seqlen, mem_k, mem_v, nh, dh, tl=512):
    tl = _tile(seqlen, tl)
    nl = seqlen // tl
    r0 = row0 // tl
    mem, wd = mem_k.shape[1], nh * dh
    kv = pl.BlockSpec((1, mem, wd), lambda s, i: (s, 0, 0))
    return pl.pallas_call(
        functools.partial(_ca_body, nh=nh, dh=dh), grid=(nb, nl),
        in_specs=[pl.BlockSpec((tl, wd), lambda s, i: (r0 + s * nl + i, 0)), kv, kv],
        out_specs=pl.BlockSpec((tl, wd), lambda s, i: (s * nl + i, 0)),
        out_shape=jax.ShapeDtypeStruct((nb * seqlen, wd), BF16),
        compiler_params=_params(("parallel", "parallel"), 32), name="cross_attention",
    )(q2d, mem_k, mem_v)


def _extract_top(x, n):
    out = []
    for it in range(n):
        m = jnp.max(x, axis=0, keepdims=True)
        out.append(m)
        if it + 1 < n:
            x = jnp.where(x == m, -jnp.inf, x)
    return out


def _peer_route_body(q_ref, keys_ref, s1_ref, e1_ref, s2_ref, e2_ref, thr_ref, *, nh, dq, topk):
    for h in range(nh):
        s1 = _bdot_nt(keys_ref[2 * h], q_ref[:, (2 * h) * dq:(2 * h + 1) * dq])
        s2 = _bdot_nt(keys_ref[2 * h + 1], q_ref[:, (2 * h + 1) * dq:(2 * h + 2) * dq])
        v1 = _extract_top(s1, topk)
        v2 = _extract_top(s2, topk)
        v2s = jnp.concatenate(v2, axis=0)
        cand = jnp.concatenate([v1[a] + v2s for a in range(topk)], axis=0)
        tops = _extract_top(cand, topk)
        zsum = jnp.exp(tops[0] - tops[0])
        for a in range(1, topk):
            zsum = zsum + jnp.exp(tops[a] - tops[0])
        s1_ref[h] = s1
        s2_ref[h] = s2
        e1_ref[h] = jnp.exp(s1 - v1[0]) / zsum
        e2_ref[h] = jnp.exp(s2 - v2[0])
        thr_ref[h:h + 1, :] = tops[topk - 1]


def peer_route(q, keys, nh, nk, dq, tm=256):
    m = q.shape[0]
    tm = _tile(m, tm)
    big = pl.BlockSpec((nh, nk, tm), lambda i: (0, 0, i))
    shape = jax.ShapeDtypeStruct((nh, nk, m), F32)
    return pl.pallas_call(
        functools.partial(_peer_route_body, nh=nh, dq=dq, topk=PEER_TOPK), grid=(m // tm,),
        in_specs=[pl.BlockSpec((tm, 2 * nh * dq), lambda i: (i, 0)),
                  pl.BlockSpec((2 * nh, nk, dq), lambda i: (0, 0, 0))],
        out_specs=[big, big, big, big, pl.BlockSpec((nh, tm), lambda i: (0, i))],
        out_shape=[shape, shape, shape, shape, jax.ShapeDtypeStruct((nh, m), F32)],
        compiler_params=_params(("parallel",), 32), name="peer_route",
    )(q, keys)


def _peer_body(hft_ref, u_ref, vt_ref, s1_ref, e1_ref, s2_ref, e2_ref, thr_ref, ot_ref, st_scr, act_scr,
               *, nh, nk, nrow, dchunk):
    ei = pl.program_id(1)
    d, tm = hft_ref.shape

    @pl.when(ei == 0)
    def _():
        ot_ref[...] = jnp.zeros_like(ot_ref)
        st_scr[1] = jnp.zeros(st_scr.shape[1:], F32)
        act_scr[0] = jnp.zeros(act_scr.shape[1:], BF16)

    def step(a):
        b = 1 - a

        def scores(r):
            rows = slice(r * nk, (r + 1) * nk)
            st_scr[a, rows, :] = jnp.dot(u_ref[rows, :], hft_ref[...], preferred_element_type=F32)

        def apply_values(j):
            ds = slice(j * dchunk, (j + 1) * dchunk)
            ot_ref[ds, :] += jnp.dot(vt_ref[ds, :], act_scr[a], preferred_element_type=F32)

        def gate_block(r, tb):
            ls = slice(tb * LANES, (tb + 1) * LANES)
            rows = slice(r * nk, (r + 1) * nk)
            gate = jnp.zeros((nk, LANES), F32)
            for h in range(nh):
                hit = (s2_ref[h, :, ls] + s1_ref[r, h:h + 1, ls]) >= thr_ref[h:h + 1, ls]
                gate = gate + jnp.where(hit, e2_ref[h, :, ls] * e1_ref[r, h:h + 1, ls], 0.0)
            act_scr[b, rows, ls] = (_gelu_tanh(st_scr[b, rows, ls]) * gate).astype(BF16)

        blocks = [(r, tb) for r in range(nrow) for tb in range(tm // LANES)]
        nd = d // dchunk
        per = -(-len(blocks) // nd)
        every = max(nd // nrow, 1)
        for j in range(max(nd, nrow * every)):
            if j % every == 0 and j // every < nrow:
                scores(j // every)
            if j < nd:
                apply_values(j)
            for r, tb in blocks[j * per:(j + 1) * per]:
                gate_block(r, tb)

    @pl.when(ei % 2 == 0)
    def _():
        step(0)

    @pl.when(ei % 2 == 1)
    def _():
        step(1)


def peer_experts(hft, u, vt, s1g, e1g, s2, e2, thr, nh, nk, tm=512, nrow=4):
    d, m = hft.shape
    tm = _tile(m, tm)
    et = nrow * nk
    once = pl.Buffered(1)
    ne = nk // nrow
    tile = lambda e, lag: jnp.clip(e - lag, 0, ne - 1)
    return pl.pallas_call(
        functools.partial(_peer_body, nh=nh, nk=nk, nrow=nrow, dchunk=_tile(d, 512)), grid=(m // tm, ne + 2),
        in_specs=[pl.BlockSpec((d, tm), lambda t, e: (0, t), pipeline_mode=once),
                  pl.BlockSpec((et, d), lambda t, e: (tile(e, 0), 0)),
                  pl.BlockSpec((d, et), lambda t, e: (0, tile(e, 2))),
                  pl.BlockSpec((nrow, nh, tm), lambda t, e: (tile(e, 1), 0, t)),
                  pl.BlockSpec((nrow, nh, tm), lambda t, e: (tile(e, 1), 0, t)),
                  pl.BlockSpec((nh, nk, tm), lambda t, e: (0, 0, t), pipeline_mode=once),
                  pl.BlockSpec((nh, nk, tm), lambda t, e: (0, 0, t), pipeline_mode=once),
                  pl.BlockSpec((nh, tm), lambda t, e: (0, t))],
        out_specs=pl.BlockSpec((d, tm), lambda t, e: (0, t)),
        out_shape=jax.ShapeDtypeStruct((d, m), F32),
        scratch_shapes=[pltpu.VMEM((2, et, tm), F32), pltpu.VMEM((2, et, tm), BF16)],
        compiler_params=_params(("parallel", "arbitrary"), 56), name="peer_experts",
    )(hft, u, vt, s1g, e1g, s2, e2, thr)


def _lane_row(nrows, pieces):
    out = jnp.zeros((SUBLANES, LANES), F32)
    for row, off, vec in pieces:
        out = out.at[row, off:off + vec.shape[0]].set(vec.astype(F32))
    return out


def kernel(x_prompt, x_sample, cache_mem_k, cache_mem_v, state_dn_conv, state_dn_rec, state_ssm_conv,
           state_ssm_rec, mem_prompt, norm_mix_g, w_in, dn_conv_w, dn_a_log, dn_dt_bias, dn_norm_g, dn_w_out,
           ssm_conv_w, ssm_conv_b, ssm_a_log, ssm_dt_bias, ssm_d, ssm_norm_g, ssm_w_out, w_o, norm_ca_g,
           ca_w_q, ca_w_k, ca_w_v, ca_w_o, norm_ffn_g, peer_w_q, peer_keys, peer_u, peer_v, final_norm_g):
    depth = w_in.shape[0]
    bp, lp, d = x_prompt.shape
    bs, ls, _ = x_sample.shape
    _, _, dnh, dk, _ = state_dn_rec.shape
    _, _, sh, hp, ns = state_ssm_rec.shape
    dnw = dnh * dk
    inner = sh * hp
    sconv = state_ssm_conv.shape[-1]
    sg = (sconv - inner) // (2 * ns)
    sr = sh // sg
    _, _, mem, cah, cad = cache_mem_k.shape
    caw = cah * cad
    _, pnh, _, nk, dq = peer_keys.shape
    lpad = -(-ls // SUBLANES) * SUBLANES
    assert lp % CHUNK == 0 and lpad <= CHUNK and dk == LANES and ns == LANES and nk == LANES and dq == LANES
    assert 2 * dnh + sh <= LANES and hp & (hp - 1) == 0
    mp, ms = bp * lp, bs * lpad
    ssm_off = 2 * dnh

    xs = jnp.pad(x_sample, ((0, 0), (0, lpad - ls), (0, 0)))
    x = jnp.concatenate([x_prompt.reshape(mp, d), xs.reshape(ms, d)], axis=0)

    outs = [[] for _ in range(10)]
    for l in range(depth):
        wl = w_in[l]
        o1 = 3 * dnw + dnw
        o2 = o1 + 2 * dnh
        o3 = o2 + inner
        o4 = o3 + sconv
        o5 = o4 + sh
        w_a = wl[:, :o1].astype(BF16)
        w_b = jnp.concatenate([wl[:, o3:o4], wl[:, o2:o3]], axis=1).astype(BF16)
        w_c = wl[:, o5:].astype(BF16)
        w_d = jnp.concatenate([wl[:, o1:o2], wl[:, o4:o5],
                               jnp.zeros((d, LANES - 2 * dnh - sh), F32)], axis=1).astype(BF16)

        hn = rmsnorm(x, norm_mix_g[l], BF16)
        seg_a = matmul(hn, w_a)
        seg_b = matmul(hn, w_b)
        seg_c = matmul(hn, w_c)
        seg_d = matmul(hn, w_d)

        dn_pa = _lane_row(SUBLANES, [(0, dnh, dn_a_log[l]), (1, dnh, dn_dt_bias[l])])
        ssm_pa = _lane_row(SUBLANES, [(0, ssm_off, ssm_a_log[l]), (1, ssm_off, ssm_dt_bias[l]), (2, ssm_off, ssm_d[l])])
        groups = (
            (0, bp, lp, CHUNK, CHUNK, jnp.zeros((bp, CONV_W - 1, 3 * dnw), F32), jnp.zeros((bp, dnh, dk, dk), F32),
             jnp.zeros((bp, CONV_W - 1, sconv), F32), jnp.zeros((bp, sg, sr * hp, ns), F32)),
            (mp, bs, lpad, lpad, ls, state_dn_conv[l], state_dn_rec[l],
             state_ssm_conv[l], state_ssm_rec[l].reshape(bs, sg, sr * hp, ns)),
        )
        o_dn, o_ssm, dn_h, ssm_h = [], [], [], []
        for row0, nb, sl, c, valid, dc0, dh0, sc0, sh0 in groups:
            od, hd = deltanet(seg_a, seg_d, row0, nb, sl, c, valid, dc0, dn_conv_w[l], dh0, dn_pa, dn_norm_g[l],
                              dnh, dk)
            osd, hsd = ssd(seg_b, seg_d, row0, nb, sl, c, valid, sc0, ssm_conv_w[l], ssm_conv_b[l], sh0, ssm_pa,
                           ssm_norm_g[l], sg, sr, hp, ns, ssm_off)
            o_dn.append(od)
            o_ssm.append(osd)
            dn_h.append(hd)
            ssm_h.append(hsd.reshape(nb, sh, hp, ns))

        mixed = gated_mix(jnp.concatenate(o_dn, axis=0), dn_w_out[l].astype(BF16),
                          jnp.concatenate(o_ssm, axis=0), ssm_w_out[l].astype(BF16), seg_c)
        x1 = matmul(mixed, w_o[l].astype(BF16), res=x)

        hc = rmsnorm(x1, norm_ca_g[l], BF16)
        qc = matmul(hc, ca_w_q[l].astype(BF16))
        memp = mem_prompt.reshape(bp * mem, d)
        mk = matmul(memp, ca_w_k[l].astype(BF16))
        mv = matmul(memp, ca_w_v[l].astype(BF16))
        oc = jnp.concatenate([
            cross_attention(qc, 0, bp, lp, mk.reshape(bp, mem, caw), mv.reshape(bp, mem, caw), cah, cad),
            cross_attention(qc, mp, bs, lpad, cache_mem_k[l].reshape(bs, mem, caw),
                            cache_mem_v[l].reshape(bs, mem, caw), cah, cad)], axis=0)
        x2 = matmul(oc, ca_w_o[l].astype(BF16), res=x1)

        hf, hft = rmsnorm(x2, norm_ffn_g[l], BF16, with_transpose=True)
        pq = matmul(hf, peer_w_q[l].astype(BF16))
        s1, e1, s2, e2, thr = peer_route(pq, peer_keys[l].reshape(2 * pnh, nk, dq).astype(BF16), pnh, nk, dq)
        pot = peer_experts(hft, peer_u[l].astype(BF16), peer_v[l].T.astype(BF16),
                           s1.transpose(1, 0, 2), e1.transpose(1, 0, 2), s2, e2, thr, pnh, nk)
        x = add_rows(x2, pot, final_norm_g if l == depth - 1 else None)

        def tail(seg, row0, nb, sl, nvalid, lo, hi, state0):
            u = seg[row0:row0 + nb * sl].reshape(nb, sl, -1)[:, max(nvalid - (CONV_W - 1), 0):nvalid, lo:hi]
            if nvalid < CONV_W - 1:
                u = jnp.concatenate([state0, u], axis=1)[:, -(CONV_W - 1):]
            return u

        outs[0].append(mk.reshape(bp, mem, cah, cad))
        outs[1].append(mv.reshape(bp, mem, cah, cad))
        outs[2].append(tail(seg_a, 0, bp, lp, lp, 0, 3 * dnw, groups[0][5]))
        outs[3].append(dn_h[0])
        outs[4].append(tail(seg_b, 0, bp, lp, lp, 0, sconv, groups[0][7]))
        outs[5].append(ssm_h[0])
        outs[6].append(tail(seg_a, mp, bs, lpad, ls, 0, 3 * dnw, state_dn_conv[l]))
        outs[7].append(dn_h[1])
        outs[8].append(tail(seg_b, mp, bs, lpad, ls, 0, sconv, state_ssm_conv[l]))
        outs[9].append(ssm_h[1])

    y_prompt = x[:mp].reshape(bp, lp, d)
    y_sample = x[mp:].reshape(bs, lpad, d)[:, :ls]
    return (y_prompt, y_sample) + tuple(jnp.stack(o) for o in outs)
```

```python
import functools
import math

import jax
import jax.numpy as jnp
import numpy as np
from jax import lax
from jax.experimental import pallas as pl
from jax.experimental.pallas import tpu as pltpu

F32 = jnp.float32
BF16 = jnp.bfloat16

EPS = 1e-6
CONV_W = 4
CHUNK = 64
PEER_TOPK = 16
LANES = 128
SUBLANES = 8
NEG = -1e30
MIB = 1 << 20


def _params(sem, vmem_mib):
    return pltpu.CompilerParams(dimension_semantics=sem, vmem_limit_bytes=vmem_mib * MIB)


def _tile(n, pref):
    t = min(pref, n)
    while n % t:
        t //= 2
    return t


def _sigmoid(x):
    return 1.0 / (1.0 + jnp.exp(-x))


def _silu(x):
    return x * _sigmoid(x)


def _softplus(x):
    return jnp.maximum(x, 0.0) + jnp.log1p(jnp.exp(-jnp.abs(x)))


def _gelu_tanh(x):
    return x * (0.5 * (1.0 + jnp.tanh(0.7978845608028654 * (x + 0.044715 * (x * x * x)))))


def _bdot(a, b):
    return jnp.dot(a.astype(BF16), b.astype(BF16), preferred_element_type=F32)


def _bdot_nt(a, b):
    return lax.dot_general(a.astype(BF16), b.astype(BF16), (((1,), (1,)), ((), ())),
                           preferred_element_type=F32)


def _bdot_tn(a, b):
    return lax.dot_general(a.astype(BF16), b.astype(BF16), (((0,), (0,)), ((), ())),
                           preferred_element_type=F32)


def _as_pieces(a):
    return list(a) if isinstance(a, (list, tuple)) else [a]


def _piece_tile(pieces, pref):
    return _tile(math.gcd(*[p.shape[0] for p in pieces]), pref)


def _piece_ends(pieces, tm):
    ends, tot = [], 0
    for p in pieces:
        tot += p.shape[0] // tm
        ends.append(tot)
    return tuple(ends)


def _piece_specs(pieces, tm, cols, row_of, col_of):
    specs, start = [], 0
    for p in pieces:
        n = p.shape[0] // tm
        specs.append(pl.BlockSpec(
            (tm, cols), lambda *g, s=start, n=n: (jnp.clip(row_of(*g) - s, 0, n - 1), col_of(*g))))
        start += n
    return specs


def _piece_value(refs, tile, ends):
    val = refs[-1][...]
    for k in range(len(refs) - 2, -1, -1):
        val = jnp.where(tile < ends[k], refs[k][...], val)
    return val


def _rmsnorm_body(*refs, nx, ends, both):
    x = _piece_value(refs[:nx], pl.program_id(0), ends)
    y = x * lax.rsqrt(jnp.mean(x * x, axis=-1, keepdims=True) + EPS) * refs[nx][...]
    refs[nx + 1][...] = y.astype(refs[nx + 1].dtype)
    if both:
        refs[nx + 2][...] = y.T.astype(refs[nx + 2].dtype)


def _addnorm_body(x_ref, pt_ref, g_ref, o_ref):
    x = x_ref[...] + pt_ref[...].T
    y = x * lax.rsqrt(jnp.mean(x * x, axis=-1, keepdims=True) + EPS)
    o_ref[...] = (y * g_ref[...]).astype(o_ref.dtype)


def _add_body(x_ref, pt_ref, o_ref):
    o_ref[...] = x_ref[...] + pt_ref[...].T


def rmsnorm(x, g, out_dtype, tm=256, with_transpose=False):
    xs = _as_pieces(x)
    d = xs[0].shape[1]
    m = sum(p.shape[0] for p in xs)
    tm = _piece_tile(xs, tm)
    row = pl.BlockSpec((tm, d), lambda i: (i, 0))
    outs, shapes = [row], [jax.ShapeDtypeStruct((m, d), out_dtype)]
    if with_transpose:
        outs.append(pl.BlockSpec((d, tm), lambda i: (0, i)))
        shapes.append(jax.ShapeDtypeStruct((d, m), out_dtype))
    res = pl.pallas_call(
        functools.partial(_rmsnorm_body, nx=len(xs), ends=_piece_ends(xs, tm), both=with_transpose),
        grid=(m // tm,),
        in_specs=_piece_specs(xs, tm, d, lambda i: i, lambda i: 0) + [pl.BlockSpec((1, d), lambda i: (0, 0))],
        out_specs=outs, out_shape=shapes,
        compiler_params=_params(("parallel",), 40), name="rmsnorm",
    )(*xs, g.reshape(1, d))
    return res if with_transpose else res[0]


def add_rows(x, pt, g, row0, nrows, tm=256):
    d = x.shape[1]
    tm = _tile(math.gcd(row0, nrows) if row0 else nrows, tm)
    r0 = row0 // tm
    ins = [pl.BlockSpec((tm, d), lambda i: (r0 + i, 0)), pl.BlockSpec((d, tm), lambda i: (0, r0 + i))]
    if g is None:
        body, args = _add_body, (x, pt)
    else:
        body, args = _addnorm_body, (x, pt, g.reshape(1, d))
        ins.append(pl.BlockSpec((1, d), lambda i: (0, 0)))
    return pl.pallas_call(
        body, grid=(nrows // tm,), in_specs=ins, out_specs=pl.BlockSpec((tm, d), lambda i: (i, 0)),
        out_shape=jax.ShapeDtypeStruct((nrows, d), F32),
        compiler_params=_params(("parallel",), 40), name="add_norm",
    )(*args)


def _mm_body(*refs, na, nr, a_ends, r_ends):
    w_ref, o_ref = refs[na], refs[-1]
    i = pl.program_id(1)
    a = _piece_value(refs[:na], i, a_ends)
    acc = jnp.dot(a.astype(BF16), w_ref[...], preferred_element_type=F32)
    if nr:
        acc = _piece_value(refs[na + 1:na + 1 + nr], i, r_ends) + acc
    o_ref[...] = acc.astype(o_ref.dtype)


def matmul(a, w, res=None, out_dtype=F32, tm=512, tn=1024):
    a_p = _as_pieces(a)
    r_p = _as_pieces(res) if res is not None else []
    k = a_p[0].shape[1]
    m = sum(p.shape[0] for p in a_p)
    n = w.shape[1]
    tm, tn = _piece_tile(a_p + r_p, tm), _tile(n, tn)
    row_of, zero = (lambda j, i: i), (lambda j, i: 0)
    ins = (_piece_specs(a_p, tm, k, row_of, zero) + [pl.BlockSpec((k, tn), lambda j, i: (0, j))]
           + _piece_specs(r_p, tm, tn, row_of, lambda j, i: j))
    return pl.pallas_call(
        functools.partial(_mm_body, na=len(a_p), nr=len(r_p), a_ends=_piece_ends(a_p, tm),
                          r_ends=_piece_ends(r_p, tm)),
        grid=(n // tn, m // tm),
        in_specs=ins, out_specs=pl.BlockSpec((tm, tn), lambda j, i: (i, j)),
        out_shape=jax.ShapeDtypeStruct((m, n), out_dtype),
        compiler_params=_params(("parallel", "parallel"), 48), name="matmul",
    )(*a_p, w, *r_p)


def _mix_body(*refs, n1, n2, ends1, ends2):
    w1_ref, w2_ref = refs[n1], refs[n1 + 1 + n2]
    g1_ref, g2_ref, o_ref = refs[-3], refs[-2], refs[-1]
    i = pl.program_id(1)
    y1 = jnp.dot(_piece_value(refs[:n1], i, ends1), w1_ref[...], preferred_element_type=F32)
    y2 = jnp.dot(_piece_value(refs[n1 + 1:n1 + 1 + n2], i, ends2), w2_ref[...], preferred_element_type=F32)
    o_ref[...] = (_sigmoid(g1_ref[...]) * y1 + _sigmoid(g2_ref[...]) * y2).astype(o_ref.dtype)


def gated_mix(a1, w1, a2, w2, gates, tm=512, tn=1024):
    p1, p2 = _as_pieces(a1), _as_pieces(a2)
    k1, k2 = p1[0].shape[1], p2[0].shape[1]
    m, n = gates.shape[0], w1.shape[1]
    tm, tn = _piece_tile(p1 + p2, tm), _tile(n, tn)
    nj = n // tn
    row_of, zero = (lambda j, i: i), (lambda j, i: 0)
    return pl.pallas_call(
        functools.partial(_mix_body, n1=len(p1), n2=len(p2), ends1=_piece_ends(p1, tm), ends2=_piece_ends(p2, tm)),
        grid=(nj, m // tm),
        in_specs=(_piece_specs(p1, tm, k1, row_of, zero) + [pl.BlockSpec((k1, tn), lambda j, i: (0, j))]
                  + _piece_specs(p2, tm, k2, row_of, zero) + [pl.BlockSpec((k2, tn), lambda j, i: (0, j))]
                  + [pl.BlockSpec((tm, tn), lambda j, i: (i, j)), pl.BlockSpec((tm, tn), lambda j, i: (i, j + nj))]),
        out_specs=pl.BlockSpec((tm, tn), lambda j, i: (i, j)),
        out_shape=jax.ShapeDtypeStruct((m, n), BF16),
        compiler_params=_params(("parallel", "parallel"), 48), name="gated_mix",
    )(*p1, w1, *p2, w2, gates, gates)


def _conv_silu_rows(x_ref, st_ref, w_ref, b_ref, buf, cv, c, ci):
    k = CONV_W - 1

    @pl.when(ci == 0)
    def _():
        buf[SUBLANES - k:SUBLANES, :] = st_ref[0]

    buf[SUBLANES:SUBLANES + c, :] = x_ref[...]
    base = SUBLANES - k
    acc = buf[base:base + c, :] * w_ref[0:1, :]
    for j in range(1, CONV_W):
        acc = acc + buf[base + j:base + j + c, :] * w_ref[j:j + 1, :]
    if b_ref is not None:
        acc = acc + b_ref[...]
    cv[...] = _silu(acc)
    buf[0:SUBLANES, :] = buf[c:c + SUBLANES, :]


def _chunk_cumsum(g, c):
    tt = lax.broadcasted_iota(jnp.int32, (c, c), 0)
    ss = lax.broadcasted_iota(jnp.int32, (c, c), 1)
    tril = jnp.where(tt >= ss, 1.0, 0.0).astype(BF16)
    hi = g.astype(BF16)
    r1 = g - hi.astype(F32)
    mid = r1.astype(BF16)
    lo = (r1 - mid.astype(F32)).astype(BF16)
    dot = functools.partial(jnp.dot, preferred_element_type=F32)
    return dot(tril, hi) + dot(tril, mid) + dot(tril, lo)


def _rows_as_lanes(x, c):
    if c < LANES:
        x = jnp.concatenate([x, jnp.zeros((LANES - c, LANES), F32)], axis=0)
    return x.T


def _dn_body(x_ref, z_ref, sm_ref, cst_ref, cw_ref, h0_ref, pa_ref, ng_ref, o_ref, ht_ref, h_scr, buf, cv,
             *, c, valid, nh, dk, nsq):
    ci = pl.program_id(1)

    @pl.when(ci == 0)
    def _():
        h_scr[...] = h0_ref[0]

    _conv_silu_rows(x_ref, cst_ref, cw_ref, None, buf, cv, c, ci)
    w = nh * dk
    sm = sm_ref[...]
    pa = pa_ref[...]
    beta_all = _sigmoid(sm)
    g_all = -jnp.exp(pa[0:1]) * _softplus(sm + pa[1:2])
    if valid < c:
        live = lax.broadcasted_iota(jnp.int32, (c, LANES), 0) < valid
        beta_all = jnp.where(live, beta_all, 0.0)
        g_all = jnp.where(live, g_all, 0.0)
    gc_all = _chunk_cumsum(g_all, c)
    gct = _rows_as_lanes(gc_all, c)
    eg_all = jnp.exp(gc_all)
    tt = lax.broadcasted_iota(jnp.int32, (c, c), 0)
    ss = lax.broadcasted_iota(jnp.int32, (c, c), 1)
    strict = tt > ss
    incl = tt >= ss
    scale = dk ** -0.5
    hs = range(nh)

    q, k = [], []
    for h in hs:
        qh = cv[:, h * dk:(h + 1) * dk]
        kh = cv[:, w + h * dk:w + (h + 1) * dk]
        q.append(qh * lax.rsqrt(jnp.sum(qh * qh, axis=-1, keepdims=True) + EPS) * scale)
        k.append(kh * lax.rsqrt(jnp.sum(kh * kh, axis=-1, keepdims=True) + EPS))
    hh = [h_scr[h] for h in hs]
    kk = [_bdot_nt(k[h], k[h]) for h in hs]
    qk = [_bdot_nt(q[h], k[h]) for h in hs]
    qh0 = [_bdot(q[h], hh[h]) for h in hs]
    gcol = [gc_all[:, nh + h:nh + h + 1] for h in hs]
    diff = [gcol[h] - gct[nh + h:nh + h + 1, 0:c] for h in hs]
    bcol = [beta_all[:, h:h + 1] for h in hs]
    egc = [eg_all[:, nh + h:nh + h + 1] for h in hs]
    glast = [gc_all[c - 1:c, nh + h:nh + h + 1] for h in hs]
    qm = [-(bcol[h] * kk[h] * jnp.exp(jnp.where(strict, diff[h], NEG))) for h in hs]
    pw = [_bdot(qm[h], qm[h]) for h in hs] if nsq else None
    for it in range(nsq):
        t = [_bdot(qm[h], pw[h]) for h in hs]
        nxt = [_bdot(pw[h], pw[h]) for h in hs] if it + 1 < nsq else None
        qm = [qm[h] + pw[h] + t[h] for h in hs]
        pw = nxt
    rhs = [jnp.concatenate([cv[:, 2 * w + h * dk:2 * w + (h + 1) * dk] * bcol[h], k[h] * (bcol[h] * egc[h])], axis=1)
           for h in hs]
    sol = [rhs[h] + _bdot(qm[h], rhs[h]) for h in hs]
    wv = [sol[h][:, :dk] - _bdot(sol[h][:, dk:], hh[h]) for h in hs]
    o = [qh0[h] * egc[h] + _bdot(qk[h] * jnp.exp(jnp.where(incl, diff[h], NEG)), wv[h]) for h in hs]
    hn = [jnp.exp(glast[h]) * hh[h] + _bdot_tn(k[h] * jnp.exp(glast[h] - gcol[h]), wv[h]) for h in hs]
    for h in hs:
        h_scr[h] = hn[h]
        oh = o[h] * lax.rsqrt(jnp.mean(o[h] * o[h], axis=-1, keepdims=True) + EPS) * ng_ref[...]
        o_ref[:, h * dk:(h + 1) * dk] = (oh * _silu(z_ref[:, h * dk:(h + 1) * dk])).astype(o_ref.dtype)

    @pl.when(ci == pl.num_programs(1) - 1)
    def _():
        ht_ref[0] = h_scr[...]


def deltanet(seg_a, seg_d, row0, nb, seqlen, c, valid, conv0, conv_w, h0, pa, ng, nh, dk):
    w = nh * dk
    nc = seqlen // c
    r0 = row0 // c
    nsq = 0
    while (2 << nsq) < valid:
        nsq += 1
    st = pl.BlockSpec((1, nh, dk, dk), lambda s, i: (s, 0, 0, 0))
    return pl.pallas_call(
        functools.partial(_dn_body, c=c, valid=valid, nh=nh, dk=dk, nsq=nsq), grid=(nb, nc),
        in_specs=[pl.BlockSpec((c, 3 * w), lambda s, i: (r0 + s * nc + i, 0)),
                  pl.BlockSpec((c, w), lambda s, i: (r0 + s * nc + i, 3)),
                  pl.BlockSpec((c, LANES), lambda s, i: (r0 + s * nc + i, 0)),
                  pl.BlockSpec((1, CONV_W - 1, 3 * w), lambda s, i: (s, 0, 0)),
                  pl.BlockSpec((CONV_W, 3 * w), lambda s, i: (0, 0)),
                  st,
                  pl.BlockSpec((SUBLANES, LANES), lambda s, i: (0, 0)),
                  pl.BlockSpec((1, dk), lambda s, i: (0, 0))],
        out_specs=[pl.BlockSpec((c, w), lambda s, i: (s * nc + i, 0)), st],
        out_shape=[jax.ShapeDtypeStruct((nb * seqlen, w), BF16),
                   jax.ShapeDtypeStruct((nb, nh, dk, dk), F32)],
        scratch_shapes=[pltpu.VMEM((nh, dk, dk), F32), pltpu.VMEM((c + SUBLANES, 3 * w), F32),
                        pltpu.VMEM((c, 3 * w), F32)],
        compiler_params=_params(("parallel", "arbitrary"), 40), name="deltanet",
    )(seg_a, seg_a, seg_d, conv0, conv_w, h0, pa, ng.reshape(1, dk))


def _ssd_body(x_ref, z_ref, sm_ref, cst_ref, cw_ref, cb_ref, h0_ref, pa_ref, ng_ref, y_ref, ht_ref,
              h_scr, buf, cv, y_scr, *, c, valid, ng, nr, hp, ns, off):
    ci = pl.program_id(1)

    @pl.when(ci == 0)
    def _():
        h_scr[...] = h0_ref[0]

    _conv_silu_rows(x_ref, cst_ref, cw_ref, cb_ref, buf, cv, c, ci)
    gw = nr * hp
    inner = ng * gw
    shift = hp.bit_length() - 1
    sm = sm_ref[...]
    pa = pa_ref[...]
    dt_all = _softplus(sm + pa[1:2])
    if valid < c:
        dt_all = jnp.where(lax.broadcasted_iota(jnp.int32, (c, LANES), 0) < valid, dt_all, 0.0)
    gc_all = _chunk_cumsum(dt_all * (-jnp.exp(pa[0:1])), c)
    gct = _rows_as_lanes(gc_all, c)
    eg_all = jnp.exp(gc_all)
    tt = lax.broadcasted_iota(jnp.int32, (c, c), 0)
    ss = lax.broadcasted_iota(jnp.int32, (c, c), 1)
    incl = tt >= ss
    lane_head = lax.broadcasted_iota(jnp.int32, (c, gw), 1) >> shift
    lane_head1 = lax.broadcasted_iota(jnp.int32, (1, gw), 1) >> shift
    row_head = lax.broadcasted_iota(jnp.int32, (gw, ns), 0) >> shift
    gs = range(ng)

    bg = [cv[:, inner + g * ns:inner + (g + 1) * ns] for g in gs]
    cg = [cv[:, inner + (ng + g) * ns:inner + (ng + g + 1) * ns] for g in gs]
    xg = [cv[:, g * gw:(g + 1) * gw] for g in gs]
    hg = [h_scr[g] for g in gs]
    cb = [_bdot_nt(cg[g], bg[g]) for g in gs]
    ch = [_bdot_nt(cg[g], hg[g]) for g in gs]
    xdt, xkd, yg, hdec = [], [], [], []
    for g in gs:
        dtb = jnp.zeros((c, gw), F32)
        egb = jnp.zeros((c, gw), F32)
        kdb = jnp.zeros((c, gw), F32)
        skip = jnp.zeros((1, gw), F32)
        hd = jnp.zeros((gw, ns), F32)
        for r in range(nr):
            col = off + g * nr + r
            gcol = gc_all[:, col:col + 1]
            glast = gc_all[c - 1:c, col:col + 1]
            seg = lane_head == r
            dtb = jnp.where(seg, dt_all[:, col:col + 1], dtb)
            egb = jnp.where(seg, eg_all[:, col:col + 1], egb)
            kdb = jnp.where(seg, jnp.exp(glast - gcol), kdb)
            skip = jnp.where(lane_head1 == r, pa[2:3, col:col + 1], skip)
            hd = jnp.where(row_head == r, jnp.exp(glast), hd)
        xdt.append(xg[g] * dtb)
        xkd.append(xdt[g] * kdb)
        yg.append(ch[g] * egb + skip * xg[g])
        hdec.append(hd)
    for r in range(nr):
        part = []
        for g in gs:
            col = off + g * nr + r
            lm = jnp.exp(jnp.where(incl, gc_all[:, col:col + 1] - gct[col:col + 1, 0:c], NEG))
            part.append(_bdot(cb[g] * lm, jnp.where(lane_head == r, xdt[g], 0.0)))
        yg = [yg[g] + part[g] for g in gs]
    hn = [hdec[g] * hg[g] + _bdot_tn(xkd[g], bg[g]) for g in gs]
    for g in gs:
        h_scr[g] = hn[g]
        y_scr[:, g * gw:(g + 1) * gw] = yg[g] * _silu(z_ref[:, g * gw:(g + 1) * gw])

    y = y_scr[...]
    y = y * lax.rsqrt(jnp.mean(y * y, axis=-1, keepdims=True) + EPS) * ng_ref[...]
    y_ref[...] = y.astype(y_ref.dtype)

    @pl.when(ci == pl.num_programs(1) - 1)
    def _():
        ht_ref[0] = h_scr[...]


def ssd(seg_b, seg_d, row0, nb, seqlen, c, valid, conv0, conv_w, conv_b, h0, pa, ngain, ng, nr, hp, ns, off):
    inner = ng * nr * hp
    sconv = inner + 2 * ng * ns
    nc = seqlen // c
    r0 = row0 // c
    glob = lambda s, i: (r0 + s * nc + i, 0)
    st = pl.BlockSpec((1, ng, nr * hp, ns), lambda s, i: (s, 0, 0, 0))
    return pl.pallas_call(
        functools.partial(_ssd_body, c=c, valid=valid, ng=ng, nr=nr, hp=hp, ns=ns, off=off), grid=(nb, nc),
        in_specs=[pl.BlockSpec((c, sconv), glob),
                  pl.BlockSpec((c, inner), lambda s, i: (r0 + s * nc + i, sconv // inner)),
                  pl.BlockSpec((c, LANES), glob),
                  pl.BlockSpec((1, CONV_W - 1, sconv), lambda s, i: (s, 0, 0)),
                  pl.BlockSpec((CONV_W, sconv), lambda s, i: (0, 0)),
                  pl.BlockSpec((1, sconv), lambda s, i: (0, 0)),
                  st,
                  pl.BlockSpec((SUBLANES, LANES), lambda s, i: (0, 0)),
                  pl.BlockSpec((1, inner), lambda s, i: (0, 0))],
        out_specs=[pl.BlockSpec((c, inner), lambda s, i: (s * nc + i, 0)), st],
        out_shape=[jax.ShapeDtypeStruct((nb * seqlen, inner), BF16),
                   jax.ShapeDtypeStruct((nb, ng, nr * hp, ns), F32)],
        scratch_shapes=[pltpu.VMEM((ng, nr * hp, ns), F32), pltpu.VMEM((c + SUBLANES, sconv), F32),
                        pltpu.VMEM((c, sconv), F32), pltpu.VMEM((c, inner), F32)],
        compiler_params=_params(("parallel", "arbitrary"), 40), name="ssd",
    )(seg_b, seg_b, seg_d, conv0, conv_w, conv_b.reshape(1, sconv), h0, pa, ngain.reshape(1, inner))


def _ca_body(q_ref, k_ref, v_ref, o_ref, *, nh, dh):
    scale = dh ** -0.5
    hs = range(nh)
    cols = [slice(h * dh, (h + 1) * dh) for h in hs]
    s = [_bdot_nt(q_ref[:, cols[h]], k_ref[0, :, cols[h]]) * scale for h in hs]
    e = [jnp.exp(s[h] - jnp.max(s[h], axis=-1, keepdims=True)) for h in hs]
    p = [e[h] / jnp.sum(e[h], axis=-1, keepdims=True) for h in hs]
    o = [_bdot(p[h], v_ref[0, :, cols[h]]) for h in hs]
    for h in hs:
        o_ref[:, cols[h]] = o[h].astype(o_ref.dtype)


def cross_attention(q2d, row0, nb, seqlen, mem_k, mem_v, nh, dh, tl=512):
    tl = _tile(seqlen, tl)
    nl = seqlen // tl
    r0 = row0 // tl
    mem, wd = mem_k.shape[1], nh * dh
    kv = pl.BlockSpec((1, mem, wd), lambda s, i: (s, 0, 0))
    return pl.pallas_call(
        functools.partial(_ca_body, nh=nh, dh=dh), grid=(nb, nl),
        in_specs=[pl.BlockSpec((tl, wd), lambda s, i: (r0 + s * nl + i, 0)), kv, kv],
        out_specs=pl.BlockSpec((tl, wd), lambda s, i: (s * nl + i, 0)),
        out_shape=jax.ShapeDtypeStruct((nb * seqlen, wd), BF16),
        compiler_params=_params(("parallel", "parallel"), 32), name="cross_attention",
    )(q2d, mem_k, mem_v)


def _extract_top(x, n):
    out = []
    for it in range(n):
        m = jnp.max(x, axis=0, keepdims=True)
        out.append(m)
        if it + 1 < n:
            x = jnp.where(x == m, -jnp.inf, x)
    return out


def _peer_route_body(q_ref, keys_ref, s1_ref, e1_ref, s2_ref, e2_ref, thr_ref, *, nh, dq, topk):
    for h in range(nh):
        s1 = _bdot_nt(keys_ref[2 * h], q_ref[:, (2 * h) * dq:(2 * h + 1) * dq])
        s2 = _bdot_nt(keys_ref[2 * h + 1], q_ref[:, (2 * h + 1) * dq:(2 * h + 2) * dq])
        v1 = _extract_top(s1, topk)
        v2 = _extract_top(s2, topk)
        v2s = jnp.concatenate(v2, axis=0)
        cand = jnp.concatenate([v1[a] + v2s for a in range(topk)], axis=0)
        tops = _extract_top(cand, topk)
        zsum = jnp.exp(tops[0] - tops[0])
        for a in range(1, topk):
            zsum = zsum + jnp.exp(tops[a] - tops[0])
        s1_ref[h] = s1
        s2_ref[h] = s2
        e1_ref[h] = jnp.exp(s1 - v1[0]) / zsum
        e2_ref[h] = jnp.exp(s2 - v2[0])
        thr_ref[h:h + 1, :] = tops[topk - 1]


def peer_route(q, keys, nh, nk, dq, tm=256):
    m = q.shape[0]
    tm = _tile(m, tm)
    big = pl.BlockSpec((nh, nk, tm), lambda i: (0, 0, i))
    shape = jax.ShapeDtypeStruct((nh, nk, m), F32)
    return pl.pallas_call(
        functools.partial(_peer_route_body, nh=nh, dq=dq, topk=PEER_TOPK), grid=(m // tm,),
        in_specs=[pl.BlockSpec((tm, 2 * nh * dq), lambda i: (i, 0)),
                  pl.BlockSpec((2 * nh, nk, dq), lambda i: (0, 0, 0))],
        out_specs=[big, big, big, big, pl.BlockSpec((nh, tm), lambda i: (0, i))],
        out_shape=[shape, shape, shape, shape, jax.ShapeDtypeStruct((nh, m), F32)],
        compiler_params=_params(("parallel",), 32), name="peer_route",
    )(q, keys)


def _peer_body(hft_ref, u_ref, vt_ref, s1_ref, e1_ref, s2_ref, e2_ref, thr_ref, ot_ref, st0, st1, act0, act1,
               *, nh, nk, nrow, dchunk):
    ei = pl.program_id(1)
    d, tm = hft_ref.shape

    @pl.when(ei == 0)
    def _():
        ot_ref[...] = jnp.zeros_like(ot_ref)
        st1[...] = jnp.zeros_like(st1)
        act0[...] = jnp.zeros_like(act0)

    pieces = nrow
    prow = nrow * nk // pieces

    def step(a):
        st_new, st_old = (st0, st1) if a == 0 else (st1, st0)
        act_old, act_new = (act0, act1) if a == 0 else (act1, act0)

        def scores(p):
            rows = slice(p * prow, (p + 1) * prow)
            st_new[rows, :] = jnp.dot(u_ref[rows, :], hft_ref[...], preferred_element_type=F32)

        def apply_values(j):
            ds = slice(j * dchunk, (j + 1) * dchunk)
            ot_ref[ds, :] += jnp.dot(vt_ref[ds, :], act_old[...], preferred_element_type=F32)

        def gate_block(r, tb):
            ls = slice(tb * LANES, (tb + 1) * LANES)
            rows = slice(r * nk, (r + 1) * nk)
            gate = jnp.zeros((nk, LANES), F32)
            for h in range(nh):
                hit = (s2_ref[h, :, ls] + s1_ref[r, h:h + 1, ls]) >= thr_ref[h:h + 1, ls]
                gate = gate + jnp.where(hit, e2_ref[h, :, ls] * e1_ref[r, h:h + 1, ls], 0.0)
            act_new[rows, ls] = (_gelu_tanh(st_old[rows, ls]) * gate).astype(BF16)

        blocks = [(r, tb) for r in range(nrow) for tb in range(tm // LANES)]
        nd = d // dchunk
        per = -(-len(blocks) // nd)
        every = max(nd // pieces, 1)
        for j in range(max(nd, pieces * every)):
            if j % every == 0 and j // every < pieces:
                scores(j // every)
            if j < nd:
                apply_values(j)
            for r, tb in blocks[j * per:(j + 1) * per]:
                gate_block(r, tb)

    @pl.when(ei % 2 == 0)
    def _():
        step(0)

    @pl.when(ei % 2 == 1)
    def _():
        step(1)


def peer_experts(hft, u, vt, s1g, e1g, s2, e2, thr, nh, nk, tm=512, nrow=4):
    d, m = hft.shape
    tm = _tile(m, tm)
    et = nrow * nk
    once = pl.Buffered(1)
    ne = nk // nrow
    tile = lambda e, lag: jnp.clip(e - lag, 0, ne - 1)
    return pl.pallas_call(
        functools.partial(_peer_body, nh=nh, nk=nk, nrow=nrow, dchunk=_tile(d, 256)), grid=(m // tm, ne + 2),
        in_specs=[pl.BlockSpec((d, tm), lambda t, e: (0, t), pipeline_mode=once),
                  pl.BlockSpec((et, d), lambda t, e: (tile(e, 0), 0)),
                  pl.BlockSpec((d, et), lambda t, e: (0, tile(e, 2))),
                  pl.BlockSpec((nrow, nh, tm), lambda t, e: (tile(e, 1), 0, t)),
                  pl.BlockSpec((nrow, nh, tm), lambda t, e: (tile(e, 1), 0, t)),
                  pl.BlockSpec((nh, nk, tm), lambda t, e: (0, 0, t), pipeline_mode=once),
                  pl.BlockSpec((nh, nk, tm), lambda t, e: (0, 0, t), pipeline_mode=once),
                  pl.BlockSpec((nh, tm), lambda t, e: (0, t))],
        out_specs=pl.BlockSpec((d, tm), lambda t, e: (0, t)),
        out_shape=jax.ShapeDtypeStruct((d, m), F32),
        scratch_shapes=[pltpu.VMEM((et, tm), F32), pltpu.VMEM((et, tm), F32),
                        pltpu.VMEM((et, tm), BF16), pltpu.VMEM((et, tm), BF16)],
        compiler_params=_params(("parallel", "arbitrary"), 56), name="peer_experts",
    )(hft, u, vt, s1g, e1g, s2, e2, thr)


def _lane_row(nrows, pieces):
    out = jnp.zeros((SUBLANES, LANES), F32)
    for row, off, vec in pieces:
        out = out.at[row, off:off + vec.shape[0]].set(vec.astype(F32))
    return out


def kernel(x_prompt, x_sample, cache_mem_k, cache_mem_v, state_dn_conv, state_dn_rec, state_ssm_conv,
           state_ssm_rec, mem_prompt, norm_mix_g, w_in, dn_conv_w, dn_a_log, dn_dt_bias, dn_norm_g, dn_w_out,
           ssm_conv_w, ssm_conv_b, ssm_a_log, ssm_dt_bias, ssm_d, ssm_norm_g, ssm_w_out, w_o, norm_ca_g,
           ca_w_q, ca_w_k, ca_w_v, ca_w_o, norm_ffn_g, peer_w_q, peer_keys, peer_u, peer_v, final_norm_g):
    depth = w_in.shape[0]
    bp, lp, d = x_prompt.shape
    bs, ls, _ = x_sample.shape
    _, _, dnh, dk, _ = state_dn_rec.shape
    _, _, sh, hp, ns = state_ssm_rec.shape
    dnw = dnh * dk
    inner = sh * hp
    sconv = state_ssm_conv.shape[-1]
    sg = (sconv - inner) // (2 * ns)
    sr = sh // sg
    _, _, mem, cah, cad = cache_mem_k.shape
    caw = cah * cad
    _, pnh, _, nk, dq = peer_keys.shape
    lpad = -(-ls // SUBLANES) * SUBLANES
    assert lp % CHUNK == 0 and lpad <= CHUNK and dk == LANES and ns == LANES and nk == LANES and dq == LANES
    assert 2 * dnh + sh <= LANES and hp & (hp - 1) == 0
    mp, ms = bp * lp, bs * lpad
    ssm_off = 2 * dnh

    x = [x_prompt.reshape(mp, d), jnp.pad(x_sample, ((0, 0), (0, lpad - ls), (0, 0))).reshape(ms, d)]

    outs = [[] for _ in range(10)]
    for l in range(depth):
        wl = w_in[l]
        o1 = 3 * dnw + dnw
        o2 = o1 + 2 * dnh
        o3 = o2 + inner
        o4 = o3 + sconv
        o5 = o4 + sh
        w_a = wl[:, :o1].astype(BF16)
        w_b = jnp.concatenate([wl[:, o3:o4], wl[:, o2:o3]], axis=1).astype(BF16)
        w_c = wl[:, o5:].astype(BF16)
        w_d = jnp.concatenate([wl[:, o1:o2], wl[:, o4:o5],
                               jnp.zeros((d, LANES - 2 * dnh - sh), F32)], axis=1).astype(BF16)

        hn = rmsnorm(x, norm_mix_g[l], BF16)
        seg_a = matmul(hn, w_a)
        seg_b = matmul(hn, w_b)
        seg_c = matmul(hn, w_c)
        seg_d = matmul(hn, w_d)

        dn_pa = _lane_row(SUBLANES, [(0, dnh, dn_a_log[l]), (1, dnh, dn_dt_bias[l])])
        ssm_pa = _lane_row(SUBLANES, [(0, ssm_off, ssm_a_log[l]), (1, ssm_off, ssm_dt_bias[l]), (2, ssm_off, ssm_d[l])])
        groups = (
            (0, bp, lp, CHUNK, CHUNK, jnp.zeros((bp, CONV_W - 1, 3 * dnw), F32), jnp.zeros((bp, dnh, dk, dk), F32),
             jnp.zeros((bp, CONV_W - 1, sconv), F32), jnp.zeros((bp, sg, sr * hp, ns), F32)),
            (mp, bs, lpad, lpad, ls, state_dn_conv[l], state_dn_rec[l],
             state_ssm_conv[l], state_ssm_rec[l].reshape(bs, sg, sr * hp, ns)),
        )
        o_dn, o_ssm, dn_h, ssm_h = [], [], [], []
        for row0, nb, sl, c, valid, dc0, dh0, sc0, sh0 in groups:
            od, hd = deltanet(seg_a, seg_d, row0, nb, sl, c, valid, dc0, dn_conv_w[l], dh0, dn_pa, dn_norm_g[l],
                              dnh, dk)
            osd, hsd = ssd(seg_b, seg_d, row0, nb, sl, c, valid, sc0, ssm_conv_w[l], ssm_conv_b[l], sh0, ssm_pa,
                           ssm_norm_g[l], sg, sr, hp, ns, ssm_off)
            o_dn.append(od)
            o_ssm.append(osd)
            dn_h.append(hd)
            ssm_h.append(hsd.reshape(nb, sh, hp, ns))

        mixed = gated_mix(o_dn, dn_w_out[l].astype(BF16), o_ssm, ssm_w_out[l].astype(BF16), seg_c)
        x1 = matmul(mixed, w_o[l].astype(BF16), res=x)

        hc = rmsnorm(x1, norm_ca_g[l], BF16)
        qc = matmul(hc, ca_w_q[l].astype(BF16))
        memp = mem_prompt.reshape(bp * mem, d)
        mk = matmul(memp, ca_w_k[l].astype(BF16))
        mv = matmul(memp, ca_w_v[l].astype(BF16))
        oc = [cross_attention(qc, 0, bp, lp, mk.reshape(bp, mem, caw), mv.reshape(bp, mem, caw), cah, cad),
              cross_attention(qc, mp, bs, lpad, cache_mem_k[l].reshape(bs, mem, caw),
                              cache_mem_v[l].reshape(bs, mem, caw), cah, cad)]
        x2 = matmul(oc, ca_w_o[l].astype(BF16), res=x1)

        hf, hft = rmsnorm(x2, norm_ffn_g[l], BF16, with_transpose=True)
        pq = matmul(hf, peer_w_q[l].astype(BF16))
        s1, e1, s2, e2, thr = peer_route(pq, peer_keys[l].reshape(2 * pnh, nk, dq).astype(BF16), pnh, nk, dq)
        pot = peer_experts(hft, peer_u[l].astype(BF16), peer_v[l].T.astype(BF16),
                           s1.transpose(1, 0, 2), e1.transpose(1, 0, 2), s2, e2, thr, pnh, nk)
        if l == depth - 1:
            y_prompt = add_rows(x2, pot, final_norm_g, 0, mp).reshape(bp, lp, d)
            y_sample = add_rows(x2, pot, final_norm_g, mp, ms).reshape(bs, lpad, d)[:, :ls]
        else:
            x = add_rows(x2, pot, None, 0, mp + ms)

        def tail(seg, row0, nb, sl, nvalid, lo, hi, state0):
            take = min(CONV_W - 1, nvalid)
            rows = row0 + np.arange(nb)[:, None] * sl + np.arange(nvalid - take, nvalid)[None, :]
            u = jnp.take(seg, rows.reshape(-1), axis=0)[:, lo:hi].reshape(nb, take, hi - lo)
            if take < CONV_W - 1:
                u = jnp.concatenate([state0[:, take:], u], axis=1)
            return u

        outs[0].append(mk.reshape(bp, mem, cah, cad))
        outs[1].append(mv.reshape(bp, mem, cah, cad))
        outs[2].append(tail(seg_a, 0, bp, lp, lp, 0, 3 * dnw, groups[0][5]))
        outs[3].append(dn_h[0])
        outs[4].append(tail(seg_b, 0, bp, lp, lp, 0, sconv, groups[0][7]))
        outs[5].append(ssm_h[0])
        outs[6].append(tail(seg_a, mp, bs, lpad, ls, 0, 3 * dnw, state_dn_conv[l]))
        outs[7].append(dn_h[1])
        outs[8].append(tail(seg_b, mp, bs, lpad, ls, 0, sconv, state_ssm_conv[l]))
        outs[9].append(ssm_h[1])

    return (y_prompt, y_sample) + tuple(jnp.stack(o) for o in outs)
```

```python
import functools
import math

import jax
import jax.numpy as jnp
import numpy as np
from jax import lax
from jax.experimental import pallas as pl
from jax.experimental.pallas import tpu as pltpu

F32 = jnp.float32
BF16 = jnp.bfloat16

EPS = 1e-6
CONV_W = 4
CHUNK = 64
PEER_TOPK = 16
LANES = 128
SUBLANES = 8
NEG = -1e30
MIB = 1 << 20


def _params(sem, vmem_mib):
    return pltpu.CompilerParams(dimension_semantics=sem, vmem_limit_bytes=vmem_mib * MIB)


def _tile(n, pref):
    t = min(pref, n)
    while n % t:
        t //= 2
    return t


def _sigmoid(x):
    return 1.0 / (1.0 + jnp.exp(-x))


def _silu(x):
    return x * _sigmoid(x)


def _softplus(x):
    return jnp.maximum(x, 0.0) + jnp.log1p(jnp.exp(-jnp.abs(x)))


def _gelu_tanh(x):
    return x * (0.5 * (1.0 + jnp.tanh(0.7978845608028654 * (x + 0.044715 * (x * x * x)))))


def _bdot(a, b):
    return jnp.dot(a.astype(BF16), b.astype(BF16), preferred_element_type=F32)


def _bdot_nt(a, b):
    return lax.dot_general(a.astype(BF16), b.astype(BF16), (((1,), (1,)), ((), ())),
                           preferred_element_type=F32)


def _bdot_tn(a, b):
    return lax.dot_general(a.astype(BF16), b.astype(BF16), (((0,), (0,)), ((), ())),
                           preferred_element_type=F32)


def _as_pieces(a):
    return list(a) if isinstance(a, (list, tuple)) else [a]


def _piece_tile(pieces, pref):
    return _tile(math.gcd(*[p.shape[0] for p in pieces]), pref)


def _piece_ends(pieces, tm):
    ends, tot = [], 0
    for p in pieces:
        tot += p.shape[0] // tm
        ends.append(tot)
    return tuple(ends)


def _piece_specs(pieces, tm, cols, row_of, col_of):
    specs, start = [], 0
    for p in pieces:
        n = p.shape[0] // tm
        specs.append(pl.BlockSpec(
            (tm, cols), lambda *g, s=start, n=n: (jnp.clip(row_of(*g) - s, 0, n - 1), col_of(*g))))
        start += n
    return specs


def _piece_value(refs, tile, ends):
    val = refs[-1][...]
    for k in range(len(refs) - 2, -1, -1):
        val = jnp.where(tile < ends[k], refs[k][...], val)
    return val


def _rmsnorm_body(*refs, nx, ends, both):
    x = _piece_value(refs[:nx], pl.program_id(0), ends)
    y = x * lax.rsqrt(jnp.mean(x * x, axis=-1, keepdims=True) + EPS) * refs[nx][...]
    refs[nx + 1][...] = y.astype(refs[nx + 1].dtype)
    if both:
        refs[nx + 2][...] = y.T.astype(refs[nx + 2].dtype)


def _addnorm_body(x_ref, pt_ref, g_ref, o_ref):
    x = x_ref[...] + pt_ref[...].T
    y = x * lax.rsqrt(jnp.mean(x * x, axis=-1, keepdims=True) + EPS)
    o_ref[...] = (y * g_ref[...]).astype(o_ref.dtype)


def _add_body(x_ref, pt_ref, o_ref):
    o_ref[...] = x_ref[...] + pt_ref[...].T


def rmsnorm(x, g, out_dtype, tm=256, with_transpose=False):
    xs = _as_pieces(x)
    d = xs[0].shape[1]
    m = sum(p.shape[0] for p in xs)
    tm = _piece_tile(xs, tm)
    row = pl.BlockSpec((tm, d), lambda i: (i, 0))
    outs, shapes = [row], [jax.ShapeDtypeStruct((m, d), out_dtype)]
    if with_transpose:
        outs.append(pl.BlockSpec((d, tm), lambda i: (0, i)))
        shapes.append(jax.ShapeDtypeStruct((d, m), out_dtype))
    res = pl.pallas_call(
        functools.partial(_rmsnorm_body, nx=len(xs), ends=_piece_ends(xs, tm), both=with_transpose),
        grid=(m // tm,),
        in_specs=_piece_specs(xs, tm, d, lambda i: i, lambda i: 0) + [pl.BlockSpec((1, d), lambda i: (0, 0))],
        out_specs=outs, out_shape=shapes,
        compiler_params=_params(("parallel",), 40), name="rmsnorm",
    )(*xs, g.reshape(1, d))
    return res if with_transpose else res[0]


def add_rows(x, pt, g, row0, nrows, tm=256):
    d = x.shape[1]
    tm = _tile(math.gcd(row0, nrows) if row0 else nrows, tm)
    r0 = row0 // tm
    ins = [pl.BlockSpec((tm, d), lambda i: (r0 + i, 0)), pl.BlockSpec((d, tm), lambda i: (0, r0 + i))]
    if g is None:
        body, args = _add_body, (x, pt)
    else:
        body, args = _addnorm_body, (x, pt, g.reshape(1, d))
        ins.append(pl.BlockSpec((1, d), lambda i: (0, 0)))
    return pl.pallas_call(
        body, grid=(nrows // tm,), in_specs=ins, out_specs=pl.BlockSpec((tm, d), lambda i: (i, 0)),
        out_shape=jax.ShapeDtypeStruct((nrows, d), F32),
        compiler_params=_params(("parallel",), 40), name="add_norm",
    )(*args)


def _transpose_cast_body(x_ref, o_ref):
    o_ref[...] = x_ref[...].T.astype(o_ref.dtype)


def transpose_cast(x, dtype, tr=512, tc=1024):
    r, c = x.shape
    tr, tc = _tile(r, tr), _tile(c, tc)
    return pl.pallas_call(
        _transpose_cast_body, grid=(r // tr, c // tc),
        in_specs=[pl.BlockSpec((tr, tc), lambda i, j: (i, j))],
        out_specs=pl.BlockSpec((tc, tr), lambda i, j: (j, i)),
        out_shape=jax.ShapeDtypeStruct((c, r), dtype),
        compiler_params=_params(("parallel", "parallel"), 32), name="transpose_cast",
    )(x)


def _mm_body(*refs, na, nr, a_ends, r_ends):
    w_ref, o_ref = refs[na], refs[-1]
    i = pl.program_id(1)
    a = _piece_value(refs[:na], i, a_ends)
    acc = jnp.dot(a.astype(BF16), w_ref[...], preferred_element_type=F32)
    if nr:
        acc = _piece_value(refs[na + 1:na + 1 + nr], i, r_ends) + acc
    o_ref[...] = acc.astype(o_ref.dtype)


def matmul(a, w, res=None, out_dtype=F32, tm=512, tn=1024):
    a_p = _as_pieces(a)
    r_p = _as_pieces(res) if res is not None else []
    k = a_p[0].shape[1]
    m = sum(p.shape[0] for p in a_p)
    n = w.shape[1]
    tm, tn = _piece_tile(a_p + r_p, tm), _tile(n, tn)
    row_of, zero = (lambda j, i: i), (lambda j, i: 0)
    ins = (_piece_specs(a_p, tm, k, row_of, zero) + [pl.BlockSpec((k, tn), lambda j, i: (0, j))]
           + _piece_specs(r_p, tm, tn, row_of, lambda j, i: j))
    return pl.pallas_call(
        functools.partial(_mm_body, na=len(a_p), nr=len(r_p), a_ends=_piece_ends(a_p, tm),
                          r_ends=_piece_ends(r_p, tm)),
        grid=(n // tn, m // tm),
        in_specs=ins, out_specs=pl.BlockSpec((tm, tn), lambda j, i: (i, j)),
        out_shape=jax.ShapeDtypeStruct((m, n), out_dtype),
        compiler_params=_params(("parallel", "parallel"), 48), name="matmul",
    )(*a_p, w, *r_p)


def _mix_body(*refs, n1, n2, ends1, ends2):
    w1_ref, w2_ref = refs[n1], refs[n1 + 1 + n2]
    g1_ref, g2_ref, o_ref = refs[-3], refs[-2], refs[-1]
    i = pl.program_id(1)
    y1 = jnp.dot(_piece_value(refs[:n1], i, ends1), w1_ref[...], preferred_element_type=F32)
    y2 = jnp.dot(_piece_value(refs[n1 + 1:n1 + 1 + n2], i, ends2), w2_ref[...], preferred_element_type=F32)
    o_ref[...] = (_sigmoid(g1_ref[...]) * y1 + _sigmoid(g2_ref[...]) * y2).astype(o_ref.dtype)


def gated_mix(a1, w1, a2, w2, gates, tm=512, tn=1024):
    p1, p2 = _as_pieces(a1), _as_pieces(a2)
    k1, k2 = p1[0].shape[1], p2[0].shape[1]
    m, n = gates.shape[0], w1.shape[1]
    tm, tn = _piece_tile(p1 + p2, tm), _tile(n, tn)
    nj = n // tn
    row_of, zero = (lambda j, i: i), (lambda j, i: 0)
    return pl.pallas_call(
        functools.partial(_mix_body, n1=len(p1), n2=len(p2), ends1=_piece_ends(p1, tm), ends2=_piece_ends(p2, tm)),
        grid=(nj, m // tm),
        in_specs=(_piece_specs(p1, tm, k1, row_of, zero) + [pl.BlockSpec((k1, tn), lambda j, i: (0, j))]
                  + _piece_specs(p2, tm, k2, row_of, zero) + [pl.BlockSpec((k2, tn), lambda j, i: (0, j))]
                  + [pl.BlockSpec((tm, tn), lambda j, i: (i, j)), pl.BlockSpec((tm, tn), lambda j, i: (i, j + nj))]),
        out_specs=pl.BlockSpec((tm, tn), lambda j, i: (i, j)),
        out_shape=jax.ShapeDtypeStruct((m, n), BF16),
        compiler_params=_params(("parallel", "parallel"), 48), name="gated_mix",
    )(*p1, w1, *p2, w2, gates, gates)


def _conv_silu_rows(x_ref, st_ref, w_ref, b_ref, buf, cv, c, ci):
    k = CONV_W - 1

    @pl.when(ci == 0)
    def _():
        buf[SUBLANES - k:SUBLANES, :] = st_ref[0]

    buf[SUBLANES:SUBLANES + c, :] = x_ref[...]
    base = SUBLANES - k
    acc = buf[base:base + c, :] * w_ref[0:1, :]
    for j in range(1, CONV_W):
        acc = acc + buf[base + j:base + j + c, :] * w_ref[j:j + 1, :]
    if b_ref is not None:
        acc = acc + b_ref[...]
    cv[...] = _silu(acc)
    buf[0:SUBLANES, :] = buf[c:c + SUBLANES, :]


def _chunk_cumsum(g, c):
    tt = lax.broadcasted_iota(jnp.int32, (c, c), 0)
    ss = lax.broadcasted_iota(jnp.int32, (c, c), 1)
    tril = jnp.where(tt >= ss, 1.0, 0.0).astype(BF16)
    hi = g.astype(BF16)
    r1 = g - hi.astype(F32)
    mid = r1.astype(BF16)
    lo = (r1 - mid.astype(F32)).astype(BF16)
    dot = functools.partial(jnp.dot, preferred_element_type=F32)
    return dot(tril, hi) + dot(tril, mid) + dot(tril, lo)


def _rows_as_lanes(x, c):
    if c < LANES:
        x = jnp.concatenate([x, jnp.zeros((LANES - c, LANES), F32)], axis=0)
    return x.T


def _dn_body(x_ref, z_ref, sm_ref, cst_ref, cw_ref, h0_ref, pa_ref, ng_ref, o_ref, ht_ref, h_scr, buf, cv,
             *, c, valid, nh, dk, nsq):
    ci = pl.program_id(1)

    @pl.when(ci == 0)
    def _():
        h_scr[...] = h0_ref[0]

    _conv_silu_rows(x_ref, cst_ref, cw_ref, None, buf, cv, c, ci)
    w = nh * dk
    sm = sm_ref[...]
    pa = pa_ref[...]
    beta_all = _sigmoid(sm)
    g_all = -jnp.exp(pa[0:1]) * _softplus(sm + pa[1:2])
    if valid < c:
        live = lax.broadcasted_iota(jnp.int32, (c, LANES), 0) < valid
        beta_all = jnp.where(live, beta_all, 0.0)
        g_all = jnp.where(live, g_all, 0.0)
    gc_all = _chunk_cumsum(g_all, c)
    gct = _rows_as_lanes(gc_all, c)
    eg_all = jnp.exp(gc_all)
    tt = lax.broadcasted_iota(jnp.int32, (c, c), 0)
    ss = lax.broadcasted_iota(jnp.int32, (c, c), 1)
    strict = tt > ss
    incl = tt >= ss
    scale = dk ** -0.5
    hs = range(nh)

    q, k = [], []
    for h in hs:
        qh = cv[:, h * dk:(h + 1) * dk]
        kh = cv[:, w + h * dk:w + (h + 1) * dk]
        q.append(qh * lax.rsqrt(jnp.sum(qh * qh, axis=-1, keepdims=True) + EPS) * scale)
        k.append(kh * lax.rsqrt(jnp.sum(kh * kh, axis=-1, keepdims=True) + EPS))
    hh = [h_scr[h] for h in hs]
    kk = [_bdot_nt(k[h], k[h]) for h in hs]
    qk = [_bdot_nt(q[h], k[h]) for h in hs]
    qh0 = [_bdot(q[h], hh[h]) for h in hs]
    gcol = [gc_all[:, nh + h:nh + h + 1] for h in hs]
    diff = [gcol[h] - gct[nh + h:nh + h + 1, 0:c] for h in hs]
    bcol = [beta_all[:, h:h + 1] for h in hs]
    egc = [eg_all[:, nh + h:nh + h + 1] for h in hs]
    glast = [gc_all[c - 1:c, nh + h:nh + h + 1] for h in hs]
    qm = [-(bcol[h] * kk[h] * jnp.exp(jnp.where(strict, diff[h], NEG))) for h in hs]
    pw = [_bdot(qm[h], qm[h]) for h in hs] if nsq else None
    for it in range(nsq):
        t = [_bdot(qm[h], pw[h]) for h in hs]
        nxt = [_bdot(pw[h], pw[h]) for h in hs] if it + 1 < nsq else None
        qm = [qm[h] + pw[h] + t[h] for h in hs]
        pw = nxt
    rhs = [jnp.concatenate([cv[:, 2 * w + h * dk:2 * w + (h + 1) * dk] * bcol[h], k[h] * (bcol[h] * egc[h])], axis=1)
           for h in hs]
    sol = [rhs[h] + _bdot(qm[h], rhs[h]) for h in hs]
    wv = [sol[h][:, :dk] - _bdot(sol[h][:, dk:], hh[h]) for h in hs]
    o = [qh0[h] * egc[h] + _bdot(qk[h] * jnp.exp(jnp.where(incl, diff[h], NEG)), wv[h]) for h in hs]
    hn = [jnp.exp(glast[h]) * hh[h] + _bdot_tn(k[h] * jnp.exp(glast[h] - gcol[h]), wv[h]) for h in hs]
    for h in hs:
        h_scr[h] = hn[h]
        oh = o[h] * lax.rsqrt(jnp.mean(o[h] * o[h], axis=-1, keepdims=True) + EPS) * ng_ref[...]
        o_ref[:, h * dk:(h + 1) * dk] = (oh * _silu(z_ref[:, h * dk:(h + 1) * dk])).astype(o_ref.dtype)

    @pl.when(ci == pl.num_programs(1) - 1)
    def _():
        ht_ref[0] = h_scr[...]


def deltanet(seg_a, seg_d, row0, nb, seqlen, c, valid, conv0, conv_w, h0, pa, ng, nh, dk):
    w = nh * dk
    nc = seqlen // c
    r0 = row0 // c
    nsq = 0
    while (2 << nsq) < valid:
        nsq += 1
    st = pl.BlockSpec((1, nh, dk, dk), lambda s, i: (s, 0, 0, 0))
    return pl.pallas_call(
        functools.partial(_dn_body, c=c, valid=valid, nh=nh, dk=dk, nsq=nsq), grid=(nb, nc),
        in_specs=[pl.BlockSpec((c, 3 * w), lambda s, i: (r0 + s * nc + i, 0)),
                  pl.BlockSpec((c, w), lambda s, i: (r0 + s * nc + i, 3)),
                  pl.BlockSpec((c, LANES), lambda s, i: (r0 + s * nc + i, 0)),
                  pl.BlockSpec((1, CONV_W - 1, 3 * w), lambda s, i: (s, 0, 0)),
                  pl.BlockSpec((CONV_W, 3 * w), lambda s, i: (0, 0)),
                  st,
                  pl.BlockSpec((SUBLANES, LANES), lambda s, i: (0, 0)),
                  pl.BlockSpec((1, dk), lambda s, i: (0, 0))],
        out_specs=[pl.BlockSpec((c, w), lambda s, i: (s * nc + i, 0)), st],
        out_shape=[jax.ShapeDtypeStruct((nb * seqlen, w), BF16),
                   jax.ShapeDtypeStruct((nb, nh, dk, dk), F32)],
        scratch_shapes=[pltpu.VMEM((nh, dk, dk), F32), pltpu.VMEM((c + SUBLANES, 3 * w), F32),
                        pltpu.VMEM((c, 3 * w), F32)],
        compiler_params=_params(("parallel", "arbitrary"), 40), name="deltanet",
    )(seg_a, seg_a, seg_d, conv0, conv_w, h0, pa, ng.reshape(1, dk))


def _ssd_body(x_ref, z_ref, sm_ref, cst_ref, cw_ref, cb_ref, h0_ref, pa_ref, ng_ref, y_ref, ht_ref,
              h_scr, buf, cv, y_scr, *, c, valid, ng, nr, hp, ns, off):
    ci = pl.program_id(1)

    @pl.when(ci == 0)
    def _():
        h_scr[...] = h0_ref[0]

    _conv_silu_rows(x_ref, cst_ref, cw_ref, cb_ref, buf, cv, c, ci)
    gw = nr * hp
    inner = ng * gw
    shift = hp.bit_length() - 1
    sm = sm_ref[...]
    pa = pa_ref[...]
    dt_all = _softplus(sm + pa[1:2])
    if valid < c:
        dt_all = jnp.where(lax.broadcasted_iota(jnp.int32, (c, LANES), 0) < valid, dt_all, 0.0)
    gc_all = _chunk_cumsum(dt_all * (-jnp.exp(pa[0:1])), c)
    gct = _rows_as_lanes(gc_all, c)
    eg_all = jnp.exp(gc_all)
    tt = lax.broadcasted_iota(jnp.int32, (c, c), 0)
    ss = lax.broadcasted_iota(jnp.int32, (c, c), 1)
    incl = tt >= ss
    lane_head = lax.broadcasted_iota(jnp.int32, (c, gw), 1) >> shift
    lane_head1 = lax.broadcasted_iota(jnp.int32, (1, gw), 1) >> shift
    row_head = lax.broadcasted_iota(jnp.int32, (gw, ns), 0) >> shift
    gs = range(ng)

    bg = [cv[:, inner + g * ns:inner + (g + 1) * ns] for g in gs]
    cg = [cv[:, inner + (ng + g) * ns:inner + (ng + g + 1) * ns] for g in gs]
    xg = [cv[:, g * gw:(g + 1) * gw] for g in gs]
    hg = [h_scr[g] for g in gs]
    cb = [_bdot_nt(cg[g], bg[g]) for g in gs]
    ch = [_bdot_nt(cg[g], hg[g]) for g in gs]
    xdt, xkd, yg, hdec = [], [], [], []
    for g in gs:
        dtb = jnp.zeros((c, gw), F32)
        egb = jnp.zeros((c, gw), F32)
        kdb = jnp.zeros((c, gw), F32)
        skip = jnp.zeros((1, gw), F32)
        hd = jnp.zeros((gw, ns), F32)
        for r in range(nr):
            col = off + g * nr + r
            gcol = gc_all[:, col:col + 1]
            glast = gc_all[c - 1:c, col:col + 1]
            seg = lane_head == r
            dtb = jnp.where(seg, dt_all[:, col:col + 1], dtb)
            egb = jnp.where(seg, eg_all[:, col:col + 1], egb)
            kdb = jnp.where(seg, jnp.exp(glast - gcol), kdb)
            skip = jnp.where(lane_head1 == r, pa[2:3, col:col + 1], skip)
            hd = jnp.where(row_head == r, jnp.exp(glast), hd)
        xdt.append(xg[g] * dtb)
        xkd.append(xdt[g] * kdb)
        yg.append(ch[g] * egb + skip * xg[g])
        hdec.append(hd)
    for r in range(nr):
        part = []
        for g in gs:
            col = off + g * nr + r
            lm = jnp.exp(jnp.where(incl, gc_all[:, col:col + 1] - gct[col:col + 1, 0:c], NEG))
            part.append(_bdot(cb[g] * lm, jnp.where(lane_head == r, xdt[g], 0.0)))
        yg = [yg[g] + part[g] for g in gs]
    hn = [hdec[g] * hg[g] + _bdot_tn(xkd[g], bg[g]) for g in gs]
    for g in gs:
        h_scr[g] = hn[g]
        y_scr[:, g * gw:(g + 1) * gw] = yg[g] * _silu(z_ref[:, g * gw:(g + 1) * gw])

    y = y_scr[...]
    y = y * lax.rsqrt(jnp.mean(y * y, axis=-1, keepdims=True) + EPS) * ng_ref[...]
    y_ref[...] = y.astype(y_ref.dtype)

    @pl.when(ci == pl.num_programs(1) - 1)
    def _():
        ht_ref[0] = h_scr[...]


def ssd(seg_b, seg_d, row0, nb, seqlen, c, valid, conv0, conv_w, conv_b, h0, pa, ngain, ng, nr, hp, ns, off):
    inner = ng * nr * hp
    sconv = inner + 2 * ng * ns
    nc = seqlen // c
    r0 = row0 // c
    glob = lambda s, i: (r0 + s * nc + i, 0)
    st = pl.BlockSpec((1, ng, nr * hp, ns), lambda s, i: (s, 0, 0, 0))
    return pl.pallas_call(
        functools.partial(_ssd_body, c=c, valid=valid, ng=ng, nr=nr, hp=hp, ns=ns, off=off), grid=(nb, nc),
        in_specs=[pl.BlockSpec((c, sconv), glob),
                  pl.BlockSpec((c, inner), lambda s, i: (r0 + s * nc + i, sconv // inner)),
                  pl.BlockSpec((c, LANES), glob),
                  pl.BlockSpec((1, CONV_W - 1, sconv), lambda s, i: (s, 0, 0)),
                  pl.BlockSpec((CONV_W, sconv), lambda s, i: (0, 0)),
                  pl.BlockSpec((1, sconv), lambda s, i: (0, 0)),
                  st,
                  pl.BlockSpec((SUBLANES, LANES), lambda s, i: (0, 0)),
                  pl.BlockSpec((1, inner), lambda s, i: (0, 0))],
        out_specs=[pl.BlockSpec((c, inner), lambda s, i: (s * nc + i, 0)), st],
        out_shape=[jax.ShapeDtypeStruct((nb * seqlen, inner), BF16),
                   jax.ShapeDtypeStruct((nb, ng, nr * hp, ns), F32)],
        scratch_shapes=[pltpu.VMEM((ng, nr * hp, ns), F32), pltpu.VMEM((c + SUBLANES, sconv), F32),
                        pltpu.VMEM((c, sconv), F32), pltpu.VMEM((c, inner), F32)],
        compiler_params=_params(("parallel", "arbitrary"), 40), name="ssd",
    )(seg_b, seg_b, seg_d, conv0, conv_w, conv_b.reshape(1, sconv), h0, pa, ngain.reshape(1, inner))


def _ca_body(q_ref, k_ref, v_ref, o_ref, *, nh, dh):
    scale = dh ** -0.5
    hs = range(nh)
    cols = [slice(h * dh, (h + 1) * dh) for h in hs]
    s = [_bdot_nt(q_ref[:, cols[h]], k_ref[0, :, cols[h]]) * scale for h in hs]
    e = [jnp.exp(s[h] - jnp.max(s[h], axis=-1, keepdims=True)) for h in hs]
    p = [e[h] / jnp.sum(e[h], axis=-1, keepdims=True) for h in hs]
    o = [_bdot(p[h], v_ref[0, :, cols[h]]) for h in hs]
    for h in hs:
        o_ref[:, cols[h]] = o[h].astype(o_ref.dtype)


def cross_attention(q2d, row0, nb, seqlen, mem_k, mem_v, nh, dh, tl=512):
    tl = _tile(seqlen, tl)
    nl = seqlen // tl
    r0 = row0 // tl
    mem, wd = mem_k.shape[1], nh * dh
    kv = pl.BlockSpec((1, mem, wd), lambda s, i: (s, 0, 0))
    return pl.pallas_call(
        functools.partial(_ca_body, nh=nh, dh=dh), grid=(nb, nl),
        in_specs=[pl.BlockSpec((tl, wd), lambda s, i: (r0 + s * nl + i, 0)), kv, kv],
        out_specs=pl.BlockSpec((tl, wd), lambda s, i: (s * nl + i, 0)),
        out_shape=jax.ShapeDtypeStruct((nb * seqlen, wd), BF16),
        compiler_params=_params(("parallel", "parallel"), 32), name="cross_attention",
    )(q2d, mem_k, mem_v)


def _extract_top(x, n):
    out = []
    for it in range(n):
        m = jnp.max(x, axis=0, keepdims=True)
        out.append(m)
        if it + 1 < n:
            x = jnp.where(x == m, -jnp.inf, x)
    return out


def _pair_candidates(v1, v2, topk):
    v1s = jnp.concatenate(v1, axis=0)
    v2s = jnp.concatenate(v2, axis=0)
    pieces = []
    a = 0
    while a < topk:
        nb = topk // (a + 1)
        if nb == 1 and a % SUBLANES == 0:
            pieces.append(v1s[a:] + v2[0])
            break
        pieces.append(v1[a] + v2s[:min(-(-nb // SUBLANES) * SUBLANES, topk)])
        a += 1
    return jnp.concatenate(pieces, axis=0)


def _peer_route_body(q_ref, keys_ref, s1_ref, e1_ref, s2_ref, e2_ref, thr_ref, *, nh, dq, topk):
    for h in range(nh):
        s1 = _bdot_nt(keys_ref[2 * h], q_ref[:, (2 * h) * dq:(2 * h + 1) * dq])
        s2 = _bdot_nt(keys_ref[2 * h + 1], q_ref[:, (2 * h + 1) * dq:(2 * h + 2) * dq])
        v1 = _extract_top(s1, topk)
        v2 = _extract_top(s2, topk)
        tops = _extract_top(_pair_candidates(v1, v2, topk), topk)
        zsum = jnp.exp(tops[0] - tops[0])
        for a in range(1, topk):
            zsum = zsum + jnp.exp(tops[a] - tops[0])
        s1_ref[h] = s1
        s2_ref[h] = s2
        e1_ref[h] = jnp.exp(s1 - v1[0]) / zsum
        e2_ref[h] = jnp.exp(s2 - v2[0])
        thr_ref[h:h + 1, :] = tops[topk - 1]


def peer_route(q, keys, nh, nk, dq, tm=256):
    m = q.shape[0]
    tm = _tile(m, tm)
    big = pl.BlockSpec((nh, nk, tm), lambda i: (0, 0, i))
    shape = jax.ShapeDtypeStruct((nh, nk, m), F32)
    return pl.pallas_call(
        functools.partial(_peer_route_body, nh=nh, dq=dq, topk=PEER_TOPK), grid=(m // tm,),
        in_specs=[pl.BlockSpec((tm, 2 * nh * dq), lambda i: (i, 0)),
                  pl.BlockSpec((2 * nh, nk, dq), lambda i: (0, 0, 0))],
        out_specs=[big, big, big, big, pl.BlockSpec((nh, tm), lambda i: (0, i))],
        out_shape=[shape, shape, shape, shape, jax.ShapeDtypeStruct((nh, m), F32)],
        compiler_params=_params(("parallel",), 32), name="peer_route",
    )(q, keys)


def _peer_body(hft_ref, u_ref, vt_ref, s1_ref, e1_ref, s2_ref, e2_ref, thr_ref, ot_ref, st0, st1, act0, act1,
               *, nh, nk, nrow, dchunk):
    ei = pl.program_id(1)
    d, tm = hft_ref.shape

    @pl.when(ei == 0)
    def _():
        ot_ref[...] = jnp.zeros_like(ot_ref)
        st1[...] = jnp.zeros_like(st1)
        act0[...] = jnp.zeros_like(act0)

    pieces = nrow
    prow = nrow * nk // pieces

    def step(a):
        st_new, st_old = (st0, st1) if a == 0 else (st1, st0)
        act_old, act_new = (act0, act1) if a == 0 else (act1, act0)

        def scores(p):
            rows = slice(p * prow, (p + 1) * prow)
            st_new[rows, :] = jnp.dot(u_ref[rows, :], hft_ref[...], preferred_element_type=F32)

        def apply_values(j):
            ds = slice(j * dchunk, (j + 1) * dchunk)
            ot_ref[ds, :] += jnp.dot(vt_ref[ds, :], act_old[...], preferred_element_type=F32)

        def gate_block(r, tb):
            ls = slice(tb * LANES, (tb + 1) * LANES)
            rows = slice(r * nk, (r + 1) * nk)
            gate = jnp.zeros((nk, LANES), F32)
            for h in range(nh):
                hit = (s2_ref[h, :, ls] + s1_ref[r, h:h + 1, ls]) >= thr_ref[h:h + 1, ls]
                gate = gate + jnp.where(hit, e2_ref[h, :, ls] * e1_ref[r, h:h + 1, ls], 0.0)
            act_new[rows, ls] = (_gelu_tanh(st_old[rows, ls]) * gate).astype(BF16)

        blocks = [(r, tb) for r in range(nrow) for tb in range(tm // LANES)]
        nd = d // dchunk
        per = -(-len(blocks) // nd)
        every = max(nd // pieces, 1)
        for j in range(max(nd, pieces * every)):
            if j % every == 0 and j // every < pieces:
                scores(j // every)
            if j < nd:
                apply_values(j)
            for r, tb in blocks[j * per:(j + 1) * per]:
                gate_block(r, tb)

    @pl.when(ei % 2 == 0)
    def _():
        step(0)

    @pl.when(ei % 2 == 1)
    def _():
        step(1)


def peer_experts(hft, u, vt, s1g, e1g, s2, e2, thr, nh, nk, tm=512, nrow=4):
    d, m = hft.shape
    tm = _tile(m, tm)
    et = nrow * nk
    once = pl.Buffered(1)
    ne = nk // nrow
    tile = lambda e, lag: jnp.clip(e - lag, 0, ne - 1)
    return pl.pallas_call(
        functools.partial(_peer_body, nh=nh, nk=nk, nrow=nrow, dchunk=_tile(d, 512)), grid=(m // tm, ne + 2),
        in_specs=[pl.BlockSpec((d, tm), lambda t, e: (0, t), pipeline_mode=once),
                  pl.BlockSpec((et, d), lambda t, e: (tile(e, 0), 0)),
                  pl.BlockSpec((d, et), lambda t, e: (0, tile(e, 2))),
                  pl.BlockSpec((nrow, nh, tm), lambda t, e: (tile(e, 1), 0, t)),
                  pl.BlockSpec((nrow, nh, tm), lambda t, e: (tile(e, 1), 0, t)),
                  pl.BlockSpec((nh, nk, tm), lambda t, e: (0, 0, t), pipeline_mode=once),
                  pl.BlockSpec((nh, nk, tm), lambda t, e: (0, 0, t), pipeline_mode=once),
                  pl.BlockSpec((nh, tm), lambda t, e: (0, t))],
        out_specs=pl.BlockSpec((d, tm), lambda t, e: (0, t)),
        out_shape=jax.ShapeDtypeStruct((d, m), F32),
        scratch_shapes=[pltpu.VMEM((et, tm), F32), pltpu.VMEM((et, tm), F32),
                        pltpu.VMEM((et, tm), BF16), pltpu.VMEM((et, tm), BF16)],
        compiler_params=_params(("parallel", "arbitrary"), 56), name="peer_experts",
    )(hft, u, vt, s1g, e1g, s2, e2, thr)


def _lane_row(nrows, pieces):
    out = jnp.zeros((SUBLANES, LANES), F32)
    for row, off, vec in pieces:
        out = out.at[row, off:off + vec.shape[0]].set(vec.astype(F32))
    return out


def kernel(x_prompt, x_sample, cache_mem_k, cache_mem_v, state_dn_conv, state_dn_rec, state_ssm_conv,
           state_ssm_rec, mem_prompt, norm_mix_g, w_in, dn_conv_w, dn_a_log, dn_dt_bias, dn_norm_g, dn_w_out,
           ssm_conv_w, ssm_conv_b, ssm_a_log, ssm_dt_bias, ssm_d, ssm_norm_g, ssm_w_out, w_o, norm_ca_g,
           ca_w_q, ca_w_k, ca_w_v, ca_w_o, norm_ffn_g, peer_w_q, peer_keys, peer_u, peer_v, final_norm_g):
    depth = w_in.shape[0]
    bp, lp, d = x_prompt.shape
    bs, ls, _ = x_sample.shape
    _, _, dnh, dk, _ = state_dn_rec.shape
    _, _, sh, hp, ns = state_ssm_rec.shape
    dnw = dnh * dk
    inner = sh * hp
    sconv = state_ssm_conv.shape[-1]
    sg = (sconv - inner) // (2 * ns)
    sr = sh // sg
    _, _, mem, cah, cad = cache_mem_k.shape
    caw = cah * cad
    _, pnh, _, nk, dq = peer_keys.shape
    lpad = -(-ls // SUBLANES) * SUBLANES
    assert lp % CHUNK == 0 and lpad <= CHUNK and dk == LANES and ns == LANES and nk == LANES and dq == LANES
    assert 2 * dnh + sh <= LANES and hp & (hp - 1) == 0
    mp, ms = bp * lp, bs * lpad
    ssm_off = 2 * dnh

    x = [x_prompt.reshape(mp, d), jnp.pad(x_sample, ((0, 0), (0, lpad - ls), (0, 0))).reshape(ms, d)]

    outs = [[] for _ in range(10)]
    for l in range(depth):
        wl = w_in[l]
        o1 = 3 * dnw + dnw
        o2 = o1 + 2 * dnh
        o3 = o2 + inner
        o4 = o3 + sconv
        o5 = o4 + sh
        w_a = wl[:, :o1].astype(BF16)
        w_b = jnp.concatenate([wl[:, o3:o4], wl[:, o2:o3]], axis=1).astype(BF16)
        w_c = wl[:, o5:].astype(BF16)
        w_d = jnp.concatenate([wl[:, o1:o2], wl[:, o4:o5],
                               jnp.zeros((d, LANES - 2 * dnh - sh), F32)], axis=1).astype(BF16)

        hn = rmsnorm(x, norm_mix_g[l], BF16)
        seg_a = matmul(hn, w_a)
        seg_b = matmul(hn, w_b)
        seg_c = matmul(hn, w_c)
        seg_d = matmul(hn, w_d)

        dn_pa = _lane_row(SUBLANES, [(0, dnh, dn_a_log[l]), (1, dnh, dn_dt_bias[l])])
        ssm_pa = _lane_row(SUBLANES, [(0, ssm_off, ssm_a_log[l]), (1, ssm_off, ssm_dt_bias[l]), (2, ssm_off, ssm_d[l])])
        groups = (
            (0, bp, lp, CHUNK, CHUNK, jnp.zeros((bp, CONV_W - 1, 3 * dnw), F32), jnp.zeros((bp, dnh, dk, dk), F32),
             jnp.zeros((bp, CONV_W - 1, sconv), F32), jnp.zeros((bp, sg, sr * hp, ns), F32)),
            (mp, bs, lpad, lpad, ls, state_dn_conv[l], state_dn_rec[l],
             state_ssm_conv[l], state_ssm_rec[l].reshape(bs, sg, sr * hp, ns)),
        )
        o_dn, o_ssm, dn_h, ssm_h = [], [], [], []
        for row0, nb, sl, c, valid, dc0, dh0, sc0, sh0 in groups:
            od, hd = deltanet(seg_a, seg_d, row0, nb, sl, c, valid, dc0, dn_conv_w[l], dh0, dn_pa, dn_norm_g[l],
                              dnh, dk)
            osd, hsd = ssd(seg_b, seg_d, row0, nb, sl, c, valid, sc0, ssm_conv_w[l], ssm_conv_b[l], sh0, ssm_pa,
                           ssm_norm_g[l], sg, sr, hp, ns, ssm_off)
            o_dn.append(od)
            o_ssm.append(osd)
            dn_h.append(hd)
            ssm_h.append(hsd.reshape(nb, sh, hp, ns))

        mixed = gated_mix(o_dn, dn_w_out[l].astype(BF16), o_ssm, ssm_w_out[l].astype(BF16), seg_c)
        x1 = matmul(mixed, w_o[l].astype(BF16), res=x)

        hc = rmsnorm(x1, norm_ca_g[l], BF16)
        qc = matmul(hc, ca_w_q[l].astype(BF16))
        memp = mem_prompt.reshape(bp * mem, d)
        mk = matmul(memp, ca_w_k[l].astype(BF16))
        mv = matmul(memp, ca_w_v[l].astype(BF16))
        oc = [cross_attention(qc, 0, bp, lp, mk.reshape(bp, mem, caw), mv.reshape(bp, mem, caw), cah, cad),
              cross_attention(qc, mp, bs, lpad, cache_mem_k[l].reshape(bs, mem, caw),
                              cache_mem_v[l].reshape(bs, mem, caw), cah, cad)]
        x2 = matmul(oc, ca_w_o[l].astype(BF16), res=x1)

        hf, hft = rmsnorm(x2, norm_ffn_g[l], BF16, with_transpose=True)
        pq = matmul(hf, peer_w_q[l].astype(BF16))
        s1, e1, s2, e2, thr = peer_route(pq, peer_keys[l].reshape(2 * pnh, nk, dq).astype(BF16), pnh, nk, dq)
        pot = peer_experts(hft, peer_u[l].astype(BF16), transpose_cast(peer_v[l], BF16),
                           s1.transpose(1, 0, 2), e1.transpose(1, 0, 2), s2, e2, thr, pnh, nk)
        if l == depth - 1:
            y_prompt = add_rows(x2, pot, final_norm_g, 0, mp).reshape(bp, lp, d)
            y_sample = add_rows(x2, pot, final_norm_g, mp, ms).reshape(bs, lpad, d)[:, :ls]
        else:
            x = add_rows(x2, pot, None, 0, mp + ms)

        def tail(seg, row0, nb, sl, nvalid, lo, hi, state0):
            take = min(CONV_W - 1, nvalid)
            rows = row0 + np.arange(nb)[:, None] * sl + np.arange(nvalid - take, nvalid)[None, :]
            u = jnp.take(seg, rows.reshape(-1), axis=0)[:, lo:hi].reshape(nb, take, hi - lo)
            if take < CONV_W - 1:
                u = jnp.concatenate([state0[:, take:], u], axis=1)
            return u

        outs[0].append(mk.reshape(bp, mem, cah, cad))
        outs[1].append(mv.reshape(bp, mem, cah, cad))
        outs[2].append(tail(seg_a, 0, bp, lp, lp, 0, 3 * dnw, groups[0][5]))
        outs[3].append(dn_h[0])
        outs[4].append(tail(seg_b, 0, bp, lp, lp, 0, sconv, groups[0][7]))
        outs[5].append(ssm_h[0])
        outs[6].append(tail(seg_a, mp, bs, lpad, ls, 0, 3 * dnw, state_dn_conv[l]))
        outs[7].append(dn_h[1])
        outs[8].append(tail(seg_b, mp, bs, lpad, ls, 0, sconv, state_ssm_conv[l]))
        outs[9].append(ssm_h[1])

    return (y_prompt, y_sample) + tuple(jnp.stack(o) for o in outs)
```

```python
import functools
import math

import jax
import jax.numpy as jnp
import numpy as np
from jax import lax
from jax.experimental import pallas as pl
from jax.experimental.pallas import tpu as pltpu

F32 = jnp.float32
BF16 = jnp.bfloat16

EPS = 1e-6
CONV_W = 4
CHUNK = 64
PEER_TOPK = 16
LANES = 128
SUBLANES = 8
NEG = -1e30
MIB = 1 << 20


def _params(sem, vmem_mib):
    return pltpu.CompilerParams(dimension_semantics=sem, vmem_limit_bytes=vmem_mib * MIB)


def _tile(n, pref):
    t = min(pref, n)
    while n % t:
        t //= 2
    return t


def _sigmoid(x):
    return 1.0 / (1.0 + jnp.exp(-x))


def _silu(x):
    return x * _sigmoid(x)


def _softplus(x):
    return jnp.maximum(x, 0.0) + jnp.log1p(jnp.exp(-jnp.abs(x)))


def _gelu_tanh(x):
    return x * (0.5 * (1.0 + jnp.tanh(0.7978845608028654 * (x + 0.044715 * (x * x * x)))))


def _bdot(a, b):
    return jnp.dot(a.astype(BF16), b.astype(BF16), preferred_element_type=F32)


def _bdot_nt(a, b):
    return lax.dot_general(a.astype(BF16), b.astype(BF16), (((1,), (1,)), ((), ())),
                           preferred_element_type=F32)


def _bdot_tn(a, b):
    return lax.dot_general(a.astype(BF16), b.astype(BF16), (((0,), (0,)), ((), ())),
                           preferred_element_type=F32)


def _as_pieces(a):
    return list(a) if isinstance(a, (list, tuple)) else [a]


def _piece_tile(pieces, pref):
    return _tile(math.gcd(*[p.shape[0] for p in pieces]), pref)


def _piece_ends(pieces, tm):
    ends, tot = [], 0
    for p in pieces:
        tot += p.shape[0] // tm
        ends.append(tot)
    return tuple(ends)


def _piece_specs(pieces, tm, cols, row_of, col_of):
    specs, start = [], 0
    for p in pieces:
        n = p.shape[0] // tm
        specs.append(pl.BlockSpec(
            (tm, cols), lambda *g, s=start, n=n: (jnp.clip(row_of(*g) - s, 0, n - 1), col_of(*g))))
        start += n
    return specs


def _piece_value(refs, tile, ends):
    val = refs[-1][...]
    for k in range(len(refs) - 2, -1, -1):
        val = jnp.where(tile < ends[k], refs[k][...], val)
    return val


def _rmsnorm_body(*refs, nx, ends, both):
    x = _piece_value(refs[:nx], pl.program_id(0), ends)
    y = x * lax.rsqrt(jnp.mean(x * x, axis=-1, keepdims=True) + EPS) * refs[nx][...]
    refs[nx + 1][...] = y.astype(refs[nx + 1].dtype)
    if both:
        refs[nx + 2][...] = y.T.astype(refs[nx + 2].dtype)


def _addnorm_body(x_ref, pt_ref, g_ref, o_ref):
    x = x_ref[...] + pt_ref[...].T
    y = x * lax.rsqrt(jnp.mean(x * x, axis=-1, keepdims=True) + EPS)
    o_ref[...] = (y * g_ref[...]).astype(o_ref.dtype)


def _add_body(x_ref, pt_ref, o_ref):
    o_ref[...] = x_ref[...] + pt_ref[...].T


def rmsnorm(x, g, out_dtype, tm=256, with_transpose=False):
    xs = _as_pieces(x)
    d = xs[0].shape[1]
    m = sum(p.shape[0] for p in xs)
    tm = _piece_tile(xs, tm)
    row = pl.BlockSpec((tm, d), lambda i: (i, 0))
    outs, shapes = [row], [jax.ShapeDtypeStruct((m, d), out_dtype)]
    if with_transpose:
        outs.append(pl.BlockSpec((d, tm), lambda i: (0, i)))
        shapes.append(jax.ShapeDtypeStruct((d, m), out_dtype))
    res = pl.pallas_call(
        functools.partial(_rmsnorm_body, nx=len(xs), ends=_piece_ends(xs, tm), both=with_transpose),
        grid=(m // tm,),
        in_specs=_piece_specs(xs, tm, d, lambda i: i, lambda i: 0) + [pl.BlockSpec((1, d), lambda i: (0, 0))],
        out_specs=outs, out_shape=shapes,
        compiler_params=_params(("parallel",), 40), name="rmsnorm",
    )(*xs, g.reshape(1, d))
    return res if with_transpose else res[0]


def add_rows(x, pt, g, row0, nrows, tm=256):
    d = x.shape[1]
    tm = _tile(math.gcd(row0, nrows) if row0 else nrows, tm)
    r0 = row0 // tm
    ins = [pl.BlockSpec((tm, d), lambda i: (r0 + i, 0)), pl.BlockSpec((d, tm), lambda i: (0, r0 + i))]
    if g is None:
        body, args = _add_body, (x, pt)
    else:
        body, args = _addnorm_body, (x, pt, g.reshape(1, d))
        ins.append(pl.BlockSpec((1, d), lambda i: (0, 0)))
    return pl.pallas_call(
        body, grid=(nrows // tm,), in_specs=ins, out_specs=pl.BlockSpec((tm, d), lambda i: (i, 0)),
        out_shape=jax.ShapeDtypeStruct((nrows, d), F32),
        compiler_params=_params(("parallel",), 40), name="add_norm",
    )(*args)


def _transpose_cast_body(x_ref, o_ref):
    o_ref[...] = x_ref[...].T.astype(o_ref.dtype)


def transpose_cast(x, dtype, tr=512, tc=1024):
    r, c = x.shape
    tr, tc = _tile(r, tr), _tile(c, tc)
    return pl.pallas_call(
        _transpose_cast_body, grid=(r // tr, c // tc),
        in_specs=[pl.BlockSpec((tr, tc), lambda i, j: (i, j))],
        out_specs=pl.BlockSpec((tc, tr), lambda i, j: (j, i)),
        out_shape=jax.ShapeDtypeStruct((c, r), dtype),
        compiler_params=_params(("parallel", "parallel"), 32), name="transpose_cast",
    )(x)


def _mm_body(*refs, na, nr, a_ends, r_ends):
    w_ref, o_ref = refs[na], refs[-1]
    i = pl.program_id(1)
    a = _piece_value(refs[:na], i, a_ends)
    acc = jnp.dot(a.astype(BF16), w_ref[...], preferred_element_type=F32)
    if nr:
        acc = _piece_value(refs[na + 1:na + 1 + nr], i, r_ends) + acc
    o_ref[...] = acc.astype(o_ref.dtype)


def matmul(a, w, res=None, out_dtype=F32, tm=512, tn=1024):
    a_p = _as_pieces(a)
    r_p = _as_pieces(res) if res is not None else []
    k = a_p[0].shape[1]
    m = sum(p.shape[0] for p in a_p)
    n = w.shape[1]
    tm, tn = _piece_tile(a_p + r_p, tm), _tile(n, tn)
    row_of, zero = (lambda j, i: i), (lambda j, i: 0)
    ins = (_piece_specs(a_p, tm, k, row_of, zero) + [pl.BlockSpec((k, tn), lambda j, i: (0, j))]
           + _piece_specs(r_p, tm, tn, row_of, lambda j, i: j))
    return pl.pallas_call(
        functools.partial(_mm_body, na=len(a_p), nr=len(r_p), a_ends=_piece_ends(a_p, tm),
                          r_ends=_piece_ends(r_p, tm)),
        grid=(n // tn, m // tm),
        in_specs=ins, out_specs=pl.BlockSpec((tm, tn), lambda j, i: (i, j)),
        out_shape=jax.ShapeDtypeStruct((m, n), out_dtype),
        compiler_params=_params(("parallel", "parallel"), 48), name="matmul",
    )(*a_p, w, *r_p)


def _mix_body(*refs, n1, n2, ends1, ends2):
    w1_ref, w2_ref = refs[n1], refs[n1 + 1 + n2]
    g1_ref, g2_ref, o_ref = refs[-3], refs[-2], refs[-1]
    i = pl.program_id(1)
    y1 = jnp.dot(_piece_value(refs[:n1], i, ends1), w1_ref[...], preferred_element_type=F32)
    y2 = jnp.dot(_piece_value(refs[n1 + 1:n1 + 1 + n2], i, ends2), w2_ref[...], preferred_element_type=F32)
    o_ref[...] = (_sigmoid(g1_ref[...]) * y1 + _sigmoid(g2_ref[...]) * y2).astype(o_ref.dtype)


def gated_mix(a1, w1, a2, w2, gates, tm=512, tn=1024):
    p1, p2 = _as_pieces(a1), _as_pieces(a2)
    k1, k2 = p1[0].shape[1], p2[0].shape[1]
    m, n = gates.shape[0], w1.shape[1]
    tm, tn = _piece_tile(p1 + p2, tm), _tile(n, tn)
    nj = n // tn
    row_of, zero = (lambda j, i: i), (lambda j, i: 0)
    return pl.pallas_call(
        functools.partial(_mix_body, n1=len(p1), n2=len(p2), ends1=_piece_ends(p1, tm), ends2=_piece_ends(p2, tm)),
        grid=(nj, m // tm),
        in_specs=(_piece_specs(p1, tm, k1, row_of, zero) + [pl.BlockSpec((k1, tn), lambda j, i: (0, j))]
                  + _piece_specs(p2, tm, k2, row_of, zero) + [pl.BlockSpec((k2, tn), lambda j, i: (0, j))]
                  + [pl.BlockSpec((tm, tn), lambda j, i: (i, j)), pl.BlockSpec((tm, tn), lambda j, i: (i, j + nj))]),
        out_specs=pl.BlockSpec((tm, tn), lambda j, i: (i, j)),
        out_shape=jax.ShapeDtypeStruct((m, n), BF16),
        compiler_params=_params(("parallel", "parallel"), 48), name="gated_mix",
    )(*p1, w1, *p2, w2, gates, gates)


def _conv_silu_rows(x_ref, st_ref, w_ref, b_ref, buf, cv, c, ci):
    k = CONV_W - 1

    @pl.when(ci == 0)
    def _():
        buf[SUBLANES - k:SUBLANES, :] = st_ref[0]

    buf[SUBLANES:SUBLANES + c, :] = x_ref[...]
    base = SUBLANES - k
    acc = buf[base:base + c, :] * w_ref[0:1, :]
    for j in range(1, CONV_W):
        acc = acc + buf[base + j:base + j + c, :] * w_ref[j:j + 1, :]
    if b_ref is not None:
        acc = acc + b_ref[...]
    cv[...] = _silu(acc)
    buf[0:SUBLANES, :] = buf[c:c + SUBLANES, :]


def _conv_silu_packed(x_ref, st_ref, w_ref, b_ref, buf, ov, cv, c, ls):
    k = CONV_W - 1
    ov[...] = jnp.zeros_like(ov)
    for j in range(c // ls):
        ov[SUBLANES + j * ls - k:SUBLANES + j * ls, :] = st_ref[j]
    buf[0:SUBLANES, :] = jnp.zeros((SUBLANES, buf.shape[1]), F32)
    buf[SUBLANES:SUBLANES + c, :] = x_ref[...]
    base = SUBLANES - k
    in_seq = lax.broadcasted_iota(jnp.int32, (c, buf.shape[1]), 0) & (ls - 1)
    acc = None
    for j in range(CONV_W):
        win = buf[base + j:base + j + c, :]
        if j < k:
            win = jnp.where(in_seq + j < k, ov[base + j:base + j + c, :], win)
        term = win * w_ref[j:j + 1, :]
        acc = term if acc is None else acc + term
    if b_ref is not None:
        acc = acc + b_ref[...]
    cv[...] = _silu(acc)


def _chunk_cumsum(g, c):
    tt = lax.broadcasted_iota(jnp.int32, (c, c), 0)
    ss = lax.broadcasted_iota(jnp.int32, (c, c), 1)
    tril = jnp.where(tt >= ss, 1.0, 0.0).astype(BF16)
    hi = g.astype(BF16)
    r1 = g - hi.astype(F32)
    mid = r1.astype(BF16)
    lo = (r1 - mid.astype(F32)).astype(BF16)
    dot = functools.partial(jnp.dot, preferred_element_type=F32)
    return dot(tril, hi) + dot(tril, mid) + dot(tril, lo)


def _rows_as_lanes(x, c):
    if c < LANES:
        x = jnp.concatenate([x, jnp.zeros((LANES - c, LANES), F32)], axis=0)
    return x.T


def _live_rows(c, seg):
    lo, hi = seg
    if (lo, hi) == (0, c):
        return None
    row = lax.broadcasted_iota(jnp.int32, (c, LANES), 0)
    return jnp.logical_and(row >= lo, row < hi)


def _by_sequence(vals, segs, c, width):
    out = vals[-1]
    if len(vals) > 1:
        row = lax.broadcasted_iota(jnp.int32, (c, width), 0)
        for s in range(len(vals) - 2, -1, -1):
            out = jnp.where(row < segs[s][1], vals[s], out)
    return out


def _dn_body(x_ref, z_ref, sm_ref, cst_ref, cw_ref, h0_ref, pa_ref, ng_ref, o_ref, ht_ref, h_scr, buf, cv, *ov,
             c, segs, nh, dk, nsq):
    ci = pl.program_id(1)

    @pl.when(ci == 0)
    def _():
        h_scr[...] = h0_ref[...]

    if len(segs) > 1:
        _conv_silu_packed(x_ref, cst_ref, cw_ref, None, buf, ov[0], cv, c, segs[0][1] - segs[0][0])
    else:
        _conv_silu_rows(x_ref, cst_ref, cw_ref, None, buf, cv, c, ci)
    w = nh * dk
    sm = sm_ref[...]
    pa = pa_ref[...]
    beta_raw = _sigmoid(sm)
    g_raw = -jnp.exp(pa[0:1]) * _softplus(sm + pa[1:2])
    beta_all, gc_all, gct, eg_all = [], [], [], []
    for seg in segs:
        live = _live_rows(c, seg)
        beta_all.append(beta_raw if live is None else jnp.where(live, beta_raw, 0.0))
        gc_all.append(_chunk_cumsum(g_raw if live is None else jnp.where(live, g_raw, 0.0), c))
        gct.append(_rows_as_lanes(gc_all[-1], c))
        eg_all.append(jnp.exp(gc_all[-1]))
    tt = lax.broadcasted_iota(jnp.int32, (c, c), 0)
    ss = lax.broadcasted_iota(jnp.int32, (c, c), 1)
    strict = tt > ss
    incl = tt >= ss
    scale = dk ** -0.5
    hs = range(nh)
    units = [(s, h) for s in range(len(segs)) for h in hs]
    us = range(len(units))

    q, k = [], []
    for h in hs:
        qh = cv[:, h * dk:(h + 1) * dk]
        kh = cv[:, w + h * dk:w + (h + 1) * dk]
        q.append(qh * lax.rsqrt(jnp.sum(qh * qh, axis=-1, keepdims=True) + EPS) * scale)
        k.append(kh * lax.rsqrt(jnp.sum(kh * kh, axis=-1, keepdims=True) + EPS))
    kk = [_bdot_nt(k[h], k[h]) for h in hs]
    qk = [_bdot_nt(q[h], k[h]) for h in hs]
    hh = [h_scr[s, h] for s, h in units]
    qh0 = [_bdot(q[h], hh[u]) for u, (s, h) in enumerate(units)]
    gcol = [gc_all[s][:, nh + h:nh + h + 1] for s, h in units]
    diff = [gcol[u] - gct[s][nh + h:nh + h + 1, 0:c] for u, (s, h) in enumerate(units)]
    bcol = [beta_all[s][:, h:h + 1] for s, h in units]
    egc = [eg_all[s][:, nh + h:nh + h + 1] for s, h in units]
    glast = [gc_all[s][c - 1:c, nh + h:nh + h + 1] for s, h in units]
    qm = [-(bcol[u] * kk[h] * jnp.exp(jnp.where(strict, diff[u], NEG))) for u, (s, h) in enumerate(units)]
    pw = [_bdot(qm[u], qm[u]) for u in us] if nsq else None
    for it in range(nsq):
        t = [_bdot(qm[u], pw[u]) for u in us]
        nxt = [_bdot(pw[u], pw[u]) for u in us] if it + 1 < nsq else None
        qm = [qm[u] + pw[u] + t[u] for u in us]
        pw = nxt
    rhs = [jnp.concatenate([cv[:, 2 * w + h * dk:2 * w + (h + 1) * dk] * bcol[u], k[h] * (bcol[u] * egc[u])], axis=1)
           for u, (s, h) in enumerate(units)]
    sol = [rhs[u] + _bdot(qm[u], rhs[u]) for u in us]
    wv = [sol[u][:, :dk] - _bdot(sol[u][:, dk:], hh[u]) for u in us]
    o = [qh0[u] * egc[u] + _bdot(qk[h] * jnp.exp(jnp.where(incl, diff[u], NEG)), wv[u])
         for u, (s, h) in enumerate(units)]
    hn = [jnp.exp(glast[u]) * hh[u] + _bdot_tn(k[h] * jnp.exp(glast[u] - gcol[u]), wv[u])
          for u, (s, h) in enumerate(units)]
    for u, (s, h) in enumerate(units):
        h_scr[s, h] = hn[u]
    for h in hs:
        oh = _by_sequence([o[s * nh + h] for s in range(len(segs))], segs, c, dk)
        oh = oh * lax.rsqrt(jnp.mean(oh * oh, axis=-1, keepdims=True) + EPS) * ng_ref[...]
        o_ref[:, h * dk:(h + 1) * dk] = (oh * _silu(z_ref[:, h * dk:(h + 1) * dk])).astype(o_ref.dtype)

    @pl.when(ci == pl.num_programs(1) - 1)
    def _():
        ht_ref[...] = h_scr[...]


def _num_squarings(segs):
    n, longest = 0, max(hi - lo for lo, hi in segs)
    while (2 << n) < longest:
        n += 1
    return n


def deltanet(seg_a, seg_d, row0, nb, nc, c, segs, conv0, conv_w, h0, pa, ng, nh, dk):
    w = nh * dk
    r0 = row0 // c
    ns = len(segs)
    st = pl.BlockSpec((ns, nh, dk, dk), lambda s, i: (s, 0, 0, 0))
    scratch = [pltpu.VMEM((ns, nh, dk, dk), F32), pltpu.VMEM((c + SUBLANES, 3 * w), F32),
               pltpu.VMEM((c, 3 * w), F32)]
    if ns > 1:
        scratch.append(pltpu.VMEM((c + SUBLANES, 3 * w), F32))
    return pl.pallas_call(
        functools.partial(_dn_body, c=c, segs=segs, nh=nh, dk=dk, nsq=_num_squarings(segs)), grid=(nb, nc),
        in_specs=[pl.BlockSpec((c, 3 * w), lambda s, i: (r0 + s * nc + i, 0)),
                  pl.BlockSpec((c, w), lambda s, i: (r0 + s * nc + i, 3)),
                  pl.BlockSpec((c, LANES), lambda s, i: (r0 + s * nc + i, 0)),
                  pl.BlockSpec((ns, CONV_W - 1, 3 * w), lambda s, i: (s, 0, 0)),
                  pl.BlockSpec((CONV_W, 3 * w), lambda s, i: (0, 0)),
                  st,
                  pl.BlockSpec((SUBLANES, LANES), lambda s, i: (0, 0)),
                  pl.BlockSpec((1, dk), lambda s, i: (0, 0))],
        out_specs=[pl.BlockSpec((c, w), lambda s, i: (s * nc + i, 0)), st],
        out_shape=[jax.ShapeDtypeStruct((nb * nc * c, w), BF16),
                   jax.ShapeDtypeStruct((nb * ns, nh, dk, dk), F32)],
        scratch_shapes=scratch,
        compiler_params=_params(("parallel", "arbitrary"), 40), name="deltanet",
    )(seg_a, seg_a, seg_d, conv0, conv_w, h0, pa, ng.reshape(1, dk))


def _ssd_body(x_ref, z_ref, sm_ref, cst_ref, cw_ref, cb_ref, h0_ref, pa_ref, ng_ref, y_ref, ht_ref,
              h_scr, buf, cv, y_scr, *ov, c, segs, ng, nr, hp, ns, off):
    ci = pl.program_id(1)

    @pl.when(ci == 0)
    def _():
        h_scr[...] = h0_ref[...]

    if len(segs) > 1:
        _conv_silu_packed(x_ref, cst_ref, cw_ref, cb_ref, buf, ov[0], cv, c, segs[0][1] - segs[0][0])
    else:
        _conv_silu_rows(x_ref, cst_ref, cw_ref, cb_ref, buf, cv, c, ci)
    gw = nr * hp
    inner = ng * gw
    shift = hp.bit_length() - 1
    sm = sm_ref[...]
    pa = pa_ref[...]
    dt_raw = _softplus(sm + pa[1:2])
    dt_all, gc_all, gct, eg_all = [], [], [], []
    for seg in segs:
        live = _live_rows(c, seg)
        dt_all.append(dt_raw if live is None else jnp.where(live, dt_raw, 0.0))
        gc_all.append(_chunk_cumsum(dt_all[-1] * (-jnp.exp(pa[0:1])), c))
        gct.append(_rows_as_lanes(gc_all[-1], c))
        eg_all.append(jnp.exp(gc_all[-1]))
    tt = lax.broadcasted_iota(jnp.int32, (c, c), 0)
    ss = lax.broadcasted_iota(jnp.int32, (c, c), 1)
    incl = tt >= ss
    lane_head = lax.broadcasted_iota(jnp.int32, (c, gw), 1) >> shift
    lane_head1 = lax.broadcasted_iota(jnp.int32, (1, gw), 1) >> shift
    row_head = lax.broadcasted_iota(jnp.int32, (gw, ns), 0) >> shift
    gs = range(ng)
    units = [(s, g) for s in range(len(segs)) for g in gs]
    us = range(len(units))

    bg = [cv[:, inner + g * ns:inner + (g + 1) * ns] for g in gs]
    cg = [cv[:, inner + (ng + g) * ns:inner + (ng + g + 1) * ns] for g in gs]
    xg = [cv[:, g * gw:(g + 1) * gw] for g in gs]
    hg = [h_scr[s, g] for s, g in units]
    cb = [_bdot_nt(cg[g], bg[g]) for g in gs]
    ch = [_bdot_nt(cg[g], hg[u]) for u, (s, g) in enumerate(units)]
    xdt, xkd, yg, hdec = [], [], [], []
    for u, (s, g) in enumerate(units):
        dtb = jnp.zeros((c, gw), F32)
        egb = jnp.zeros((c, gw), F32)
        kdb = jnp.zeros((c, gw), F32)
        skip = jnp.zeros((1, gw), F32)
        hd = jnp.zeros((gw, ns), F32)
        for r in range(nr):
            col = off + g * nr + r
            gcol = gc_all[s][:, col:col + 1]
            glast = gc_all[s][c - 1:c, col:col + 1]
            seg = lane_head == r
            dtb = jnp.where(seg, dt_all[s][:, col:col + 1], dtb)
            egb = jnp.where(seg, eg_all[s][:, col:col + 1], egb)
            kdb = jnp.where(seg, jnp.exp(glast - gcol), kdb)
            skip = jnp.where(lane_head1 == r, pa[2:3, col:col + 1], skip)
            hd = jnp.where(row_head == r, jnp.exp(glast), hd)
        xdt.append(xg[g] * dtb)
        xkd.append(xdt[u] * kdb)
        yg.append(ch[u] * egb + skip * xg[g])
        hdec.append(hd)
    for r in range(nr):
        part = []
        for u, (s, g) in enumerate(units):
            col = off + g * nr + r
            lm = jnp.exp(jnp.where(incl, gc_all[s][:, col:col + 1] - gct[s][col:col + 1, 0:c], NEG))
            part.append(_bdot(cb[g] * lm, jnp.where(lane_head == r, xdt[u], 0.0)))
        yg = [yg[u] + part[u] for u in us]
    hn = [hdec[u] * hg[u] + _bdot_tn(xkd[u], bg[g]) for u, (s, g) in enumerate(units)]
    for u, (s, g) in enumerate(units):
        h_scr[s, g] = hn[u]
    for g in gs:
        yv = _by_sequence([yg[s * ng + g] for s in range(len(segs))], segs, c, gw)
        y_scr[:, g * gw:(g + 1) * gw] = yv * _silu(z_ref[:, g * gw:(g + 1) * gw])

    y = y_scr[...]
    y = y * lax.rsqrt(jnp.mean(y * y, axis=-1, keepdims=True) + EPS) * ng_ref[...]
    y_ref[...] = y.astype(y_ref.dtype)

    @pl.when(ci == pl.num_programs(1) - 1)
    def _():
        ht_ref[...] = h_scr[...]


def ssd(seg_b, seg_d, row0, nb, nc, c, segs, conv0, conv_w, conv_b, h0, pa, ngain, ng, nr, hp, ns, off):
    inner = ng * nr * hp
    sconv = inner + 2 * ng * ns
    r0 = row0 // c
    nseq = len(segs)
    glob = lambda s, i: (r0 + s * nc + i, 0)
    st = pl.BlockSpec((nseq, ng, nr * hp, ns), lambda s, i: (s, 0, 0, 0))
    scratch = [pltpu.VMEM((nseq, ng, nr * hp, ns), F32), pltpu.VMEM((c + SUBLANES, sconv), F32),
               pltpu.VMEM((c, sconv), F32), pltpu.VMEM((c, inner), F32)]
    if nseq > 1:
        scratch.append(pltpu.VMEM((c + SUBLANES, sconv), F32))
    return pl.pallas_call(
        functools.partial(_ssd_body, c=c, segs=segs, ng=ng, nr=nr, hp=hp, ns=ns, off=off), grid=(nb, nc),
        in_specs=[pl.BlockSpec((c, sconv), glob),
                  pl.BlockSpec((c, inner), lambda s, i: (r0 + s * nc + i, sconv // inner)),
                  pl.BlockSpec((c, LANES), glob),
                  pl.BlockSpec((nseq, CONV_W - 1, sconv), lambda s, i: (s, 0, 0)),
                  pl.BlockSpec((CONV_W, sconv), lambda s, i: (0, 0)),
                  pl.BlockSpec((1, sconv), lambda s, i: (0, 0)),
                  st,
                  pl.BlockSpec((SUBLANES, LANES), lambda s, i: (0, 0)),
                  pl.BlockSpec((1, inner), lambda s, i: (0, 0))],
        out_specs=[pl.BlockSpec((c, inner), lambda s, i: (s * nc + i, 0)), st],
        out_shape=[jax.ShapeDtypeStruct((nb * nc * c, inner), BF16),
                   jax.ShapeDtypeStruct((nb * nseq, ng, nr * hp, ns), F32)],
        scratch_shapes=scratch,
        compiler_params=_params(("parallel", "arbitrary"), 40), name="ssd",
    )(seg_b, seg_b, seg_d, conv0, conv_w, conv_b.reshape(1, sconv), h0, pa, ngain.reshape(1, inner))


def _ca_body(q_ref, k_ref, v_ref, o_ref, *, nh, dh, nseq, ls):
    scale = dh ** -0.5
    cols = [slice(h * dh, (h + 1) * dh) for h in range(nh)]

    def head(ref, b, h):
        return ref[b, :, h, :] if len(ref.shape) == 4 else ref[b, :, cols[h]]

    units = [(b, h) for b in range(nseq) for h in range(nh)]
    s = [_bdot_nt(q_ref[:, cols[h]], head(k_ref, b, h)) * scale for b, h in units]
    e = [jnp.exp(x - jnp.max(x, axis=-1, keepdims=True)) for x in s]
    p = [x / jnp.sum(x, axis=-1, keepdims=True) for x in e]
    o = [_bdot(p[i], head(v_ref, b, h)) for i, (b, h) in enumerate(units)]
    row = lax.broadcasted_iota(jnp.int32, (q_ref.shape[0], dh), 0)
    for h in range(nh):
        val = o[(nseq - 1) * nh + h]
        for b in range(nseq - 2, -1, -1):
            val = jnp.where(row < (b + 1) * ls, o[b * nh + h], val)
        o_ref[:, cols[h]] = val.astype(o_ref.dtype)


def cross_attention(q2d, row0, nrows, tl, nseq, mem_k, mem_v, nh, dh):
    r0 = row0 // tl
    wd = nh * dh
    per = nrows // tl * nseq // mem_k.shape[0]
    if mem_k.ndim == 4:
        kv = pl.BlockSpec((nseq,) + mem_k.shape[1:], lambda i: (i // per, 0, 0, 0))
    else:
        kv = pl.BlockSpec((nseq,) + mem_k.shape[1:], lambda i: (i // per, 0, 0))
    return pl.pallas_call(
        functools.partial(_ca_body, nh=nh, dh=dh, nseq=nseq, ls=tl // nseq), grid=(nrows // tl,),
        in_specs=[pl.BlockSpec((tl, wd), lambda i: (r0 + i, 0)), kv, kv],
        out_specs=pl.BlockSpec((tl, wd), lambda i: (i, 0)),
        out_shape=jax.ShapeDtypeStruct((nrows, wd), BF16),
        compiler_params=_params(("parallel",), 32), name="cross_attention",
    )(q2d, mem_k, mem_v)


def _extract_top(x, n):
    out = []
    for it in range(n):
        m = jnp.max(x, axis=0, keepdims=True)
        out.append(m)
        if it + 1 < n:
            x = jnp.where(x == m, -jnp.inf, x)
    return out


def _pair_candidates(v1, v2, topk):
    v1s = jnp.concatenate(v1, axis=0)
    v2s = jnp.concatenate(v2, axis=0)
    pieces = []
    a = 0
    while a < topk:
        nb = topk // (a + 1)
        if nb == 1 and a % SUBLANES == 0:
            pieces.append(v1s[a:] + v2[0])
            break
        pieces.append(v1[a] + v2s[:min(-(-nb // SUBLANES) * SUBLANES, topk)])
        a += 1
    return jnp.concatenate(pieces, axis=0)


def _peer_route_body(q_ref, keys_ref, s1_ref, e1_ref, s2_ref, e2_ref, thr_ref, *, nh, dq, topk):
    for h in range(nh):
        s1 = _bdot_nt(keys_ref[2 * h], q_ref[:, (2 * h) * dq:(2 * h + 1) * dq])
        s2 = _bdot_nt(keys_ref[2 * h + 1], q_ref[:, (2 * h + 1) * dq:(2 * h + 2) * dq])
        v1 = _extract_top(s1, topk)
        v2 = _extract_top(s2, topk)
        tops = _extract_top(_pair_candidates(v1, v2, topk), topk)
        zsum = jnp.exp(tops[0] - tops[0])
        for a in range(1, topk):
            zsum = zsum + jnp.exp(tops[a] - tops[0])
        s1_ref[h] = s1
        s2_ref[h] = s2
        e1_ref[h] = jnp.exp(s1 - v1[0]) / zsum
        e2_ref[h] = jnp.exp(s2 - v2[0])
        thr_ref[h:h + 1, :] = tops[topk - 1]


def peer_route(q, keys, nh, nk, dq, tm=256):
    m = q.shape[0]
    tm = _tile(m, tm)
    big = pl.BlockSpec((nh, nk, tm), lambda i: (0, 0, i))
    shape = jax.ShapeDtypeStruct((nh, nk, m), F32)
    return pl.pallas_call(
        functools.partial(_peer_route_body, nh=nh, dq=dq, topk=PEER_TOPK), grid=(m // tm,),
        in_specs=[pl.BlockSpec((tm, 2 * nh * dq), lambda i: (i, 0)),
                  pl.BlockSpec((2 * nh, nk, dq), lambda i: (0, 0, 0))],
        out_specs=[big, big, big, big, pl.BlockSpec((nh, tm), lambda i: (0, i))],
        out_shape=[shape, shape, shape, shape, jax.ShapeDtypeStruct((nh, m), F32)],
        compiler_params=_params(("parallel",), 32), name="peer_route",
    )(q, keys)


def _peer_body(hft_ref, u_ref, vt_ref, s1_ref, e1_ref, s2_ref, e2_ref, thr_ref, ot_ref, st0, st1, act0, act1,
               *, nh, nk, nrow, dchunk):
    ei = pl.program_id(1)
    d, tm = hft_ref.shape

    @pl.when(ei == 0)
    def _():
        ot_ref[...] = jnp.zeros_like(ot_ref)
        st1[...] = jnp.zeros_like(st1)
        act0[...] = jnp.zeros_like(act0)

    pieces = nrow
    prow = nrow * nk // pieces

    def step(a):
        st_new, st_old = (st0, st1) if a == 0 else (st1, st0)
        act_old, act_new = (act0, act1) if a == 0 else (act1, act0)

        def scores(p):
            rows = slice(p * prow, (p + 1) * prow)
            st_new[rows, :] = jnp.dot(u_ref[rows, :], hft_ref[...], preferred_element_type=F32)

        def apply_values(j):
            ds = slice(j * dchunk, (j + 1) * dchunk)
            ot_ref[ds, :] += jnp.dot(vt_ref[ds, :], act_old[...], preferred_element_type=F32)

        def gate_block(r, tb):
            ls = slice(tb * LANES, (tb + 1) * LANES)
            rows = slice(r * nk, (r + 1) * nk)
            gate = jnp.zeros((nk, LANES), F32)
            for h in range(nh):
                hit = (s2_ref[h, :, ls] + s1_ref[r, h:h + 1, ls]) >= thr_ref[h:h + 1, ls]
                gate = gate + jnp.where(hit, e2_ref[h, :, ls] * e1_ref[r, h:h + 1, ls], 0.0)
            act_new[rows, ls] = (_gelu_tanh(st_old[rows, ls]) * gate).astype(BF16)

        blocks = [(r, tb) for r in range(nrow) for tb in range(tm // LANES)]
        nd = d // dchunk
        per = -(-len(blocks) // nd)
        every = max(nd // pieces, 1)
        for j in range(max(nd, pieces * every)):
            if j % every == 0 and j // every < pieces:
                scores(j // every)
            if j < nd:
                apply_values(j)
            for r, tb in blocks[j * per:(j + 1) * per]:
                gate_block(r, tb)

    @pl.when(ei % 2 == 0)
    def _():
        step(0)

    @pl.when(ei % 2 == 1)
    def _():
        step(1)


def peer_experts(hft, u, vt, s1g, e1g, s2, e2, thr, nh, nk, tm=512, nrow=4):
    d, m = hft.shape
    tm = _tile(m, tm)
    et = nrow * nk
    once = pl.Buffered(1)
    ne = nk // nrow
    tile = lambda e, lag: jnp.clip(e - lag, 0, ne - 1)
    return pl.pallas_call(
        functools.partial(_peer_body, nh=nh, nk=nk, nrow=nrow, dchunk=_tile(d, 512)), grid=(m // tm, ne + 2),
        in_specs=[pl.BlockSpec((d, tm), lambda t, e: (0, t), pipeline_mode=once),
                  pl.BlockSpec((et, d), lambda t, e: (tile(e, 0), 0)),
                  pl.BlockSpec((d, et), lambda t, e: (0, tile(e, 2))),
                  pl.BlockSpec((nrow, nh, tm), lambda t, e: (tile(e, 1), 0, t)),
                  pl.BlockSpec((nrow, nh, tm), lambda t, e: (tile(e, 1), 0, t)),
                  pl.BlockSpec((nh, nk, tm), lambda t, e: (0, 0, t), pipeline_mode=once),
                  pl.BlockSpec((nh, nk, tm), lambda t, e: (0, 0, t), pipeline_mode=once),
                  pl.BlockSpec((nh, tm), lambda t, e: (0, t))],
        out_specs=pl.BlockSpec((d, tm), lambda t, e: (0, t)),
        out_shape=jax.ShapeDtypeStruct((d, m), F32),
        scratch_shapes=[pltpu.VMEM((et, tm), F32), pltpu.VMEM((et, tm), F32),
                        pltpu.VMEM((et, tm), BF16), pltpu.VMEM((et, tm), BF16)],
        compiler_params=_params(("parallel", "arbitrary"), 56), name="peer_experts",
    )(hft, u, vt, s1g, e1g, s2, e2, thr)


def _lane_row(nrows, pieces):
    out = jnp.zeros((SUBLANES, LANES), F32)
    for row, off, vec in pieces:
        out = out.at[row, off:off + vec.shape[0]].set(vec.astype(F32))
    return out


def kernel(x_prompt, x_sample, cache_mem_k, cache_mem_v, state_dn_conv, state_dn_rec, state_ssm_conv,
           state_ssm_rec, mem_prompt, norm_mix_g, w_in, dn_conv_w, dn_a_log, dn_dt_bias, dn_norm_g, dn_w_out,
           ssm_conv_w, ssm_conv_b, ssm_a_log, ssm_dt_bias, ssm_d, ssm_norm_g, ssm_w_out, w_o, norm_ca_g,
           ca_w_q, ca_w_k, ca_w_v, ca_w_o, norm_ffn_g, peer_w_q, peer_keys, peer_u, peer_v, final_norm_g):
    depth = w_in.shape[0]
    bp, lp, d = x_prompt.shape
    bs, ls, _ = x_sample.shape
    _, _, dnh, dk, _ = state_dn_rec.shape
    _, _, sh, hp, ns = state_ssm_rec.shape
    dnw = dnh * dk
    inner = sh * hp
    sconv = state_ssm_conv.shape[-1]
    sg = (sconv - inner) // (2 * ns)
    sr = sh // sg
    _, _, mem, cah, cad = cache_mem_k.shape
    caw = cah * cad
    _, pnh, _, nk, dq = peer_keys.shape
    packed = SUBLANES % ls == 0 and ls >= CONV_W - 1 and bs % (SUBLANES // ls) == 0
    lrow = ls if packed else -(-ls // SUBLANES) * SUBLANES
    cs = SUBLANES if packed else lrow
    nseq = cs // lrow
    s_segs = tuple((j * lrow, j * lrow + ls) for j in range(nseq))
    assert lp % CHUNK == 0 and cs <= CHUNK and dk == LANES and ns == LANES and nk == LANES and dq == LANES
    assert 2 * dnh + sh <= LANES and hp & (hp - 1) == 0
    mp, ms = bp * lp, bs * lrow
    ssm_off = 2 * dnh

    x = [x_prompt.reshape(mp, d), jnp.pad(x_sample, ((0, 0), (0, lrow - ls), (0, 0))).reshape(ms, d)]

    outs = [[] for _ in range(10)]
    for l in range(depth):
        wl = w_in[l]
        o1 = 3 * dnw + dnw
        o2 = o1 + 2 * dnh
        o3 = o2 + inner
        o4 = o3 + sconv
        o5 = o4 + sh
        w_a = wl[:, :o1].astype(BF16)
        w_b = jnp.concatenate([wl[:, o3:o4], wl[:, o2:o3]], axis=1).astype(BF16)
        w_c = wl[:, o5:].astype(BF16)
        w_d = jnp.concatenate([wl[:, o1:o2], wl[:, o4:o5],
                               jnp.zeros((d, LANES - 2 * dnh - sh), F32)], axis=1).astype(BF16)

        hn = rmsnorm(x, norm_mix_g[l], BF16)
        seg_a = matmul(hn, w_a)
        seg_b = matmul(hn, w_b)
        seg_c = matmul(hn, w_c)
        seg_d = matmul(hn, w_d)

        dn_pa = _lane_row(SUBLANES, [(0, dnh, dn_a_log[l]), (1, dnh, dn_dt_bias[l])])
        ssm_pa = _lane_row(SUBLANES, [(0, ssm_off, ssm_a_log[l]), (1, ssm_off, ssm_dt_bias[l]), (2, ssm_off, ssm_d[l])])
        groups = (
            (0, bp, lp // CHUNK, CHUNK, ((0, CHUNK),), jnp.zeros((bp, CONV_W - 1, 3 * dnw), F32),
             jnp.zeros((bp, dnh, dk, dk), F32), jnp.zeros((bp, CONV_W - 1, sconv), F32),
             jnp.zeros((bp, sg, sr * hp, ns), F32)),
            (mp, bs // nseq, 1, cs, s_segs, state_dn_conv[l], state_dn_rec[l],
             state_ssm_conv[l], state_ssm_rec[l].reshape(bs, sg, sr * hp, ns)),
        )
        o_dn, o_ssm, dn_h, ssm_h = [], [], [], []
        for row0, nb, nc, c, segs, dc0, dh0, sc0, sh0 in groups:
            od, hd = deltanet(seg_a, seg_d, row0, nb, nc, c, segs, dc0, dn_conv_w[l], dh0, dn_pa, dn_norm_g[l],
                              dnh, dk)
            osd, hsd = ssd(seg_b, seg_d, row0, nb, nc, c, segs, sc0, ssm_conv_w[l], ssm_conv_b[l], sh0, ssm_pa,
                           ssm_norm_g[l], sg, sr, hp, ns, ssm_off)
            o_dn.append(od)
            o_ssm.append(osd)
            dn_h.append(hd)
            ssm_h.append(hsd.reshape(-1, sh, hp, ns))

        mixed = gated_mix(o_dn, dn_w_out[l].astype(BF16), o_ssm, ssm_w_out[l].astype(BF16), seg_c)
        x1 = matmul(mixed, w_o[l].astype(BF16), res=x)

        hc = rmsnorm(x1, norm_ca_g[l], BF16)
        qc = matmul(hc, ca_w_q[l].astype(BF16))
        memp = mem_prompt.reshape(bp * mem, d)
        mk = matmul(memp, ca_w_k[l].astype(BF16))
        mv = matmul(memp, ca_w_v[l].astype(BF16))
        oc = [cross_attention(qc, 0, mp, _tile(lp, 512), 1, mk.reshape(bp, mem, caw), mv.reshape(bp, mem, caw),
                              cah, cad),
              cross_attention(qc, mp, ms, cs, nseq, cache_mem_k[l], cache_mem_v[l], cah, cad)]
        x2 = matmul(oc, ca_w_o[l].astype(BF16), res=x1)

        hf, hft = rmsnorm(x2, norm_ffn_g[l], BF16, with_transpose=True)
        pq = matmul(hf, peer_w_q[l].astype(BF16))
        s1, e1, s2, e2, thr = peer_route(pq, peer_keys[l].reshape(2 * pnh, nk, dq).astype(BF16), pnh, nk, dq)
        pot = peer_experts(hft, peer_u[l].astype(BF16), transpose_cast(peer_v[l], BF16),
                           s1.transpose(1, 0, 2), e1.transpose(1, 0, 2), s2, e2, thr, pnh, nk)
        if l == depth - 1:
            y_prompt = add_rows(x2, pot, final_norm_g, 0, mp).reshape(bp, lp, d)
            y_sample = add_rows(x2, pot, final_norm_g, mp, ms).reshape(bs, lrow, d)[:, :ls]
        else:
            x = add_rows(x2, pot, None, 0, mp + ms)

        def tail(seg, row0, nb, sl, nvalid, lo, hi, state0):
            take = min(CONV_W - 1, nvalid)
            rows = row0 + np.arange(nb)[:, None] * sl + np.arange(nvalid - take, nvalid)[None, :]
            u = jnp.take(seg, rows.reshape(-1), axis=0)[:, lo:hi].reshape(nb, take, hi - lo)
            if take < CONV_W - 1:
                u = jnp.concatenate([state0[:, take:], u], axis=1)
            return u

        outs[0].append(mk.reshape(bp, mem, cah, cad))
        outs[1].append(mv.reshape(bp, mem, cah, cad))
        outs[2].append(tail(seg_a, 0, bp, lp, lp, 0, 3 * dnw, groups[0][5]))
        outs[3].append(dn_h[0])
        outs[4].append(tail(seg_b, 0, bp, lp, lp, 0, sconv, groups[0][7]))
        outs[5].append(ssm_h[0])
        outs[6].append(tail(seg_a, mp, bs, lrow, ls, 0, 3 * dnw, state_dn_conv[l]))
        outs[7].append(dn_h[1])
        outs[8].append(tail(seg_b, mp, bs, lrow, ls, 0, sconv, state_ssm_conv[l]))
        outs[9].append(ssm_h[1])

    return (y_prompt, y_sample) + tuple(jnp.stack(o) for o in outs)
```

```python
import functools
import math

import jax
import jax.numpy as jnp
import numpy as np
from jax import lax
from jax.experimental import pallas as pl
from jax.experimental.pallas import tpu as pltpu

F32 = jnp.float32
BF16 = jnp.bfloat16

EPS = 1e-6
CONV_W = 4
CHUNK = 64
PEER_TOPK = 16
LANES = 128
SUBLANES = 8
NEG = -1e30
MIB = 1 << 20


def _params(sem, vmem_mib):
    return pltpu.CompilerParams(dimension_semantics=sem, vmem_limit_bytes=vmem_mib * MIB)


def _tile(n, pref):
    t = min(pref, n)
    while n % t:
        t //= 2
    return t


def _sigmoid(x):
    return 1.0 / (1.0 + jnp.exp(-x))


def _silu(x):
    return x * _sigmoid(x)


def _softplus(x):
    return jnp.maximum(x, 0.0) + jnp.log1p(jnp.exp(-jnp.abs(x)))


def _gelu_tanh(x):
    return x * (0.5 * (1.0 + jnp.tanh(0.7978845608028654 * (x + 0.044715 * (x * x * x)))))


def _bdot(a, b):
    return jnp.dot(a.astype(BF16), b.astype(BF16), preferred_element_type=F32)


def _bdot_nt(a, b):
    return lax.dot_general(a.astype(BF16), b.astype(BF16), (((1,), (1,)), ((), ())),
                           preferred_element_type=F32)


def _bdot_tn(a, b):
    return lax.dot_general(a.astype(BF16), b.astype(BF16), (((0,), (0,)), ((), ())),
                           preferred_element_type=F32)


def _as_pieces(a):
    return list(a) if isinstance(a, (list, tuple)) else [a]


def _piece_tile(pieces, pref):
    return _tile(math.gcd(*[p.shape[0] for p in pieces]), pref)


def _piece_ends(pieces, tm):
    ends, tot = [], 0
    for p in pieces:
        tot += p.shape[0] // tm
        ends.append(tot)
    return tuple(ends)


def _piece_specs(pieces, tm, cols, row_of, col_of):
    specs, start = [], 0
    for p in pieces:
        n = p.shape[0] // tm
        specs.append(pl.BlockSpec(
            (tm, cols), lambda *g, s=start, n=n: (jnp.clip(row_of(*g) - s, 0, n - 1), col_of(*g))))
        start += n
    return specs


def _piece_value(refs, tile, ends):
    val = refs[-1][...]
    for k in range(len(refs) - 2, -1, -1):
        val = jnp.where(tile < ends[k], refs[k][...], val)
    return val


def _rmsnorm_body(*refs, nx, ends, both):
    x = _piece_value(refs[:nx], pl.program_id(0), ends)
    y = x * lax.rsqrt(jnp.mean(x * x, axis=-1, keepdims=True) + EPS) * refs[nx][...]
    refs[nx + 1][...] = y.astype(refs[nx + 1].dtype)
    if both:
        refs[nx + 2][...] = y.T.astype(refs[nx + 2].dtype)


def _addnorm_body(x_ref, pt_ref, g_ref, o_ref):
    x = x_ref[...] + pt_ref[...].T
    y = x * lax.rsqrt(jnp.mean(x * x, axis=-1, keepdims=True) + EPS)
    o_ref[...] = (y * g_ref[...]).astype(o_ref.dtype)


def _add_body(x_ref, pt_ref, o_ref):
    o_ref[...] = x_ref[...] + pt_ref[...].T


def rmsnorm(x, g, out_dtype, tm=256, with_transpose=False):
    xs = _as_pieces(x)
    d = xs[0].shape[1]
    m = sum(p.shape[0] for p in xs)
    tm = _piece_tile(xs, tm)
    row = pl.BlockSpec((tm, d), lambda i: (i, 0))
    outs, shapes = [row], [jax.ShapeDtypeStruct((m, d), out_dtype)]
    if with_transpose:
        outs.append(pl.BlockSpec((d, tm), lambda i: (0, i)))
        shapes.append(jax.ShapeDtypeStruct((d, m), out_dtype))
    res = pl.pallas_call(
        functools.partial(_rmsnorm_body, nx=len(xs), ends=_piece_ends(xs, tm), both=with_transpose),
        grid=(m // tm,),
        in_specs=_piece_specs(xs, tm, d, lambda i: i, lambda i: 0) + [pl.BlockSpec((1, d), lambda i: (0, 0))],
        out_specs=outs, out_shape=shapes,
        compiler_params=_params(("parallel",), 40), name="rmsnorm",
    )(*xs, g.reshape(1, d))
    return res if with_transpose else res[0]


def add_rows(x, pt, g, row0, nrows, tm=256):
    d = x.shape[1]
    tm = _tile(math.gcd(row0, nrows) if row0 else nrows, tm)
    r0 = row0 // tm
    ins = [pl.BlockSpec((tm, d), lambda i: (r0 + i, 0)), pl.BlockSpec((d, tm), lambda i: (0, r0 + i))]
    if g is None:
        body, args = _add_body, (x, pt)
    else:
        body, args = _addnorm_body, (x, pt, g.reshape(1, d))
        ins.append(pl.BlockSpec((1, d), lambda i: (0, 0)))
    return pl.pallas_call(
        body, grid=(nrows // tm,), in_specs=ins, out_specs=pl.BlockSpec((tm, d), lambda i: (i, 0)),
        out_shape=jax.ShapeDtypeStruct((nrows, d), F32),
        compiler_params=_params(("parallel",), 40), name="add_norm",
    )(*args)


def _transpose_cast_body(x_ref, o_ref):
    o_ref[...] = x_ref[...].T.astype(o_ref.dtype)


def transpose_cast(x, dtype, tr=512, tc=1024):
    r, c = x.shape
    tr, tc = _tile(r, tr), _tile(c, tc)
    return pl.pallas_call(
        _transpose_cast_body, grid=(r // tr, c // tc),
        in_specs=[pl.BlockSpec((tr, tc), lambda i, j: (i, j))],
        out_specs=pl.BlockSpec((tc, tr), lambda i, j: (j, i)),
        out_shape=jax.ShapeDtypeStruct((c, r), dtype),
        compiler_params=_params(("parallel", "parallel"), 32), name="transpose_cast",
    )(x)


def _split_in_proj_body(w_ref, a_ref, b_ref, c_ref, d_ref, *, o1, o2, o3, o4, o5):
    a_ref[...] = w_ref[:, 0:o1].astype(BF16)
    b_ref[:, 0:o4 - o3] = w_ref[:, o3:o4].astype(BF16)
    b_ref[:, o4 - o3:] = w_ref[:, o2:o3].astype(BF16)
    c_ref[...] = w_ref[:, o5:].astype(BF16)
    pad = jnp.zeros((d_ref.shape[0], LANES - (o2 - o1) - (o5 - o4)), F32)
    d_ref[...] = jnp.concatenate([w_ref[:, o1:o2], w_ref[:, o4:o5], pad], axis=1).astype(BF16)


def split_in_proj(w, o1, o2, o3, o4, o5, rb=128):
    d, n = w.shape
    rb = _tile(d, rb)
    widths = (o1, o4 - o2, n - o5, LANES)
    return pl.pallas_call(
        functools.partial(_split_in_proj_body, o1=o1, o2=o2, o3=o3, o4=o4, o5=o5), grid=(d // rb,),
        in_specs=[pl.BlockSpec((rb, n), lambda i: (i, 0))],
        out_specs=[pl.BlockSpec((rb, wd), lambda i: (i, 0)) for wd in widths],
        out_shape=[jax.ShapeDtypeStruct((d, wd), BF16) for wd in widths],
        compiler_params=_params(("parallel",), 48), name="split_in_proj",
    )(w)


def _mm_body(*refs, na, nr, a_ends, r_ends):
    w_ref, o_ref = refs[na], refs[-1]
    i = pl.program_id(1)
    a = _piece_value(refs[:na], i, a_ends)
    acc = jnp.dot(a.astype(BF16), w_ref[...], preferred_element_type=F32)
    if nr:
        acc = _piece_value(refs[na + 1:na + 1 + nr], i, r_ends) + acc
    o_ref[...] = acc.astype(o_ref.dtype)


def matmul(a, w, res=None, out_dtype=F32, tm=512, tn=1024):
    a_p = _as_pieces(a)
    r_p = _as_pieces(res) if res is not None else []
    k = a_p[0].shape[1]
    m = sum(p.shape[0] for p in a_p)
    n = w.shape[1]
    tm, tn = _piece_tile(a_p + r_p, tm), _tile(n, tn)
    row_of, zero = (lambda j, i: i), (lambda j, i: 0)
    ins = (_piece_specs(a_p, tm, k, row_of, zero) + [pl.BlockSpec((k, tn), lambda j, i: (0, j))]
           + _piece_specs(r_p, tm, tn, row_of, lambda j, i: j))
    return pl.pallas_call(
        functools.partial(_mm_body, na=len(a_p), nr=len(r_p), a_ends=_piece_ends(a_p, tm),
                          r_ends=_piece_ends(r_p, tm)),
        grid=(n // tn, m // tm),
        in_specs=ins, out_specs=pl.BlockSpec((tm, tn), lambda j, i: (i, j)),
        out_shape=jax.ShapeDtypeStruct((m, n), out_dtype),
        compiler_params=_params(("parallel", "parallel"), 48), name="matmul",
    )(*a_p, w, *r_p)


def _mix_body(*refs, n1, n2, ends1, ends2):
    w1_ref, w2_ref = refs[n1], refs[n1 + 1 + n2]
    g1_ref, g2_ref, o_ref = refs[-3], refs[-2], refs[-1]
    i = pl.program_id(1)
    y1 = jnp.dot(_piece_value(refs[:n1], i, ends1), w1_ref[...], preferred_element_type=F32)
    y2 = jnp.dot(_piece_value(refs[n1 + 1:n1 + 1 + n2], i, ends2), w2_ref[...], preferred_element_type=F32)
    o_ref[...] = (_sigmoid(g1_ref[...]) * y1 + _sigmoid(g2_ref[...]) * y2).astype(o_ref.dtype)


def gated_mix(a1, w1, a2, w2, gates, tm=512, tn=1024):
    p1, p2 = _as_pieces(a1), _as_pieces(a2)
    k1, k2 = p1[0].shape[1], p2[0].shape[1]
    m, n = gates.shape[0], w1.shape[1]
    tm, tn = _piece_tile(p1 + p2, tm), _tile(n, tn)
    nj = n // tn
    row_of, zero = (lambda j, i: i), (lambda j, i: 0)
    return pl.pallas_call(
        functools.partial(_mix_body, n1=len(p1), n2=len(p2), ends1=_piece_ends(p1, tm), ends2=_piece_ends(p2, tm)),
        grid=(nj, m // tm),
        in_specs=(_piece_specs(p1, tm, k1, row_of, zero) + [pl.BlockSpec((k1, tn), lambda j, i: (0, j))]
                  + _piece_specs(p2, tm, k2, row_of, zero) + [pl.BlockSpec((k2, tn), lambda j, i: (0, j))]
                  + [pl.BlockSpec((tm, tn), lambda j, i: (i, j)), pl.BlockSpec((tm, tn), lambda j, i: (i, j + nj))]),
        out_specs=pl.BlockSpec((tm, tn), lambda j, i: (i, j)),
        out_shape=jax.ShapeDtypeStruct((m, n), BF16),
        compiler_params=_params(("parallel", "parallel"), 48), name="gated_mix",
    )(*p1, w1, *p2, w2, gates, gates)


def _conv_silu_rows(x_ref, st_ref, w_ref, b_ref, buf, cv, c, ci):
    k = CONV_W - 1

    @pl.when(ci == 0)
    def _():
        buf[SUBLANES - k:SUBLANES, :] = st_ref[0]

    buf[SUBLANES:SUBLANES + c, :] = x_ref[...]
    base = SUBLANES - k
    acc = buf[base:base + c, :] * w_ref[0:1, :]
    for j in range(1, CONV_W):
        acc = acc + buf[base + j:base + j + c, :] * w_ref[j:j + 1, :]
    if b_ref is not None:
        acc = acc + b_ref[...]
    cv[...] = _silu(acc)
    buf[0:SUBLANES, :] = buf[c:c + SUBLANES, :]


def _conv_silu_packed(x_ref, st_ref, w_ref, b_ref, buf, ov, cv, c, ls):
    k = CONV_W - 1
    ov[...] = jnp.zeros_like(ov)
    for j in range(c // ls):
        ov[SUBLANES + j * ls - k:SUBLANES + j * ls, :] = st_ref[j]
    buf[0:SUBLANES, :] = jnp.zeros((SUBLANES, buf.shape[1]), F32)
    buf[SUBLANES:SUBLANES + c, :] = x_ref[...]
    base = SUBLANES - k
    in_seq = lax.broadcasted_iota(jnp.int32, (c, buf.shape[1]), 0) & (ls - 1)
    acc = None
    for j in range(CONV_W):
        win = buf[base + j:base + j + c, :]
        if j < k:
            win = jnp.where(in_seq + j < k, ov[base + j:base + j + c, :], win)
        term = win * w_ref[j:j + 1, :]
        acc = term if acc is None else acc + term
    if b_ref is not None:
        acc = acc + b_ref[...]
    cv[...] = _silu(acc)


def _chunk_cumsum(g, c):
    tt = lax.broadcasted_iota(jnp.int32, (c, c), 0)
    ss = lax.broadcasted_iota(jnp.int32, (c, c), 1)
    tril = jnp.where(tt >= ss, 1.0, 0.0).astype(BF16)
    hi = g.astype(BF16)
    r1 = g - hi.astype(F32)
    mid = r1.astype(BF16)
    lo = (r1 - mid.astype(F32)).astype(BF16)
    dot = functools.partial(jnp.dot, preferred_element_type=F32)
    return dot(tril, hi) + dot(tril, mid) + dot(tril, lo)


def _rows_as_lanes(x, c):
    if c < LANES:
        x = jnp.concatenate([x, jnp.zeros((LANES - c, LANES), F32)], axis=0)
    return x.T


def _live_rows(c, seg):
    lo, hi = seg
    if (lo, hi) == (0, c):
        return None
    row = lax.broadcasted_iota(jnp.int32, (c, LANES), 0)
    return jnp.logical_and(row >= lo, row < hi)


def _by_sequence(vals, segs, c, width):
    out = vals[-1]
    if len(vals) > 1:
        row = lax.broadcasted_iota(jnp.int32, (c, width), 0)
        for s in range(len(vals) - 2, -1, -1):
            out = jnp.where(row < segs[s][1], vals[s], out)
    return out


def _dn_body(x_ref, z_ref, sm_ref, cst_ref, cw_ref, h0_ref, pa_ref, ng_ref, o_ref, ht_ref, h_scr, buf, cv, *ov,
             c, segs, nh, dk, nsq):
    ci = pl.program_id(1)

    @pl.when(ci == 0)
    def _():
        h_scr[...] = h0_ref[...]

    if len(segs) > 1:
        _conv_silu_packed(x_ref, cst_ref, cw_ref, None, buf, ov[0], cv, c, segs[0][1] - segs[0][0])
    else:
        _conv_silu_rows(x_ref, cst_ref, cw_ref, None, buf, cv, c, ci)
    w = nh * dk
    sm = sm_ref[...]
    pa = pa_ref[...]
    beta_raw = _sigmoid(sm)
    g_raw = -jnp.exp(pa[0:1]) * _softplus(sm + pa[1:2])
    beta_all, gc_all, gct, eg_all = [], [], [], []
    for seg in segs:
        live = _live_rows(c, seg)
        beta_all.append(beta_raw if live is None else jnp.where(live, beta_raw, 0.0))
        gc_all.append(_chunk_cumsum(g_raw if live is None else jnp.where(live, g_raw, 0.0), c))
        gct.append(_rows_as_lanes(gc_all[-1], c))
        eg_all.append(jnp.exp(gc_all[-1]))
    tt = lax.broadcasted_iota(jnp.int32, (c, c), 0)
    ss = lax.broadcasted_iota(jnp.int32, (c, c), 1)
    strict = tt > ss
    incl = tt >= ss
    scale = dk ** -0.5
    hs = range(nh)
    units = [(s, h) for s in range(len(segs)) for h in hs]
    us = range(len(units))

    q, k = [], []
    for h in hs:
        qh = cv[:, h * dk:(h + 1) * dk]
        kh = cv[:, w + h * dk:w + (h + 1) * dk]
        q.append(qh * lax.rsqrt(jnp.sum(qh * qh, axis=-1, keepdims=True) + EPS) * scale)
        k.append(kh * lax.rsqrt(jnp.sum(kh * kh, axis=-1, keepdims=True) + EPS))
    kk = [_bdot_nt(k[h], k[h]) for h in hs]
    qk = [_bdot_nt(q[h], k[h]) for h in hs]
    hh = [h_scr[s, h] for s, h in units]
    qh0 = [_bdot(q[h], hh[u]) for u, (s, h) in enumerate(units)]
    gcol = [gc_all[s][:, nh + h:nh + h + 1] for s, h in units]
    diff = [gcol[u] - gct[s][nh + h:nh + h + 1, 0:c] for u, (s, h) in enumerate(units)]
    bcol = [beta_all[s][:, h:h + 1] for s, h in units]
    egc = [eg_all[s][:, nh + h:nh + h + 1] for s, h in units]
    glast = [gc_all[s][c - 1:c, nh + h:nh + h + 1] for s, h in units]
    qm = [-(bcol[u] * kk[h] * jnp.exp(jnp.where(strict, diff[u], NEG))) for u, (s, h) in enumerate(units)]
    pw = [_bdot(qm[u], qm[u]) for u in us] if nsq else None
    for it in range(nsq):
        t = [_bdot(qm[u], pw[u]) for u in us]
        nxt = [_bdot(pw[u], pw[u]) for u in us] if it + 1 < nsq else None
        qm = [qm[u] + pw[u] + t[u] for u in us]
        pw = nxt
    rhs = [jnp.concatenate([cv[:, 2 * w + h * dk:2 * w + (h + 1) * dk] * bcol[u], k[h] * (bcol[u] * egc[u])], axis=1)
           for u, (s, h) in enumerate(units)]
    sol = [rhs[u] + _bdot(qm[u], rhs[u]) for u in us]
    wv = [sol[u][:, :dk] - _bdot(sol[u][:, dk:], hh[u]) for u in us]
    o = [qh0[u] * egc[u] + _bdot(qk[h] * jnp.exp(jnp.where(incl, diff[u], NEG)), wv[u])
         for u, (s, h) in enumerate(units)]
    hn = [jnp.exp(glast[u]) * hh[u] + _bdot_tn(k[h] * jnp.exp(glast[u] - gcol[u]), wv[u])
          for u, (s, h) in enumerate(units)]
    for u, (s, h) in enumerate(units):
        h_scr[s, h] = hn[u]
    for h in hs:
        oh = _by_sequence([o[s * nh + h] for s in range(len(segs))], segs, c, dk)
        oh = oh * lax.rsqrt(jnp.mean(oh * oh, axis=-1, keepdims=True) + EPS) * ng_ref[...]
        o_ref[:, h * dk:(h + 1) * dk] = (oh * _silu(z_ref[:, h * dk:(h + 1) * dk])).astype(o_ref.dtype)

    @pl.when(ci == pl.num_programs(1) - 1)
    def _():
        ht_ref[...] = h_scr[...]


def _num_squarings(segs):
    n, longest = 0, max(hi - lo for lo, hi in segs)
    while (2 << n) < longest:
        n += 1
    return n


def deltanet(seg_a, seg_d, row0, nb, nc, c, segs, conv0, conv_w, h0, pa, ng, nh, dk):
    w = nh * dk
    r0 = row0 // c
    ns = len(segs)
    st = pl.BlockSpec((ns, nh, dk, dk), lambda s, i: (s, 0, 0, 0))
    scratch = [pltpu.VMEM((ns, nh, dk, dk), F32), pltpu.VMEM((c + SUBLANES, 3 * w), F32),
               pltpu.VMEM((c, 3 * w), F32)]
    if ns > 1:
        scratch.append(pltpu.VMEM((c + SUBLANES, 3 * w), F32))
    return pl.pallas_call(
        functools.partial(_dn_body, c=c, segs=segs, nh=nh, dk=dk, nsq=_num_squarings(segs)), grid=(nb, nc),
        in_specs=[pl.BlockSpec((c, 3 * w), lambda s, i: (r0 + s * nc + i, 0)),
                  pl.BlockSpec((c, w), lambda s, i: (r0 + s * nc + i, 3)),
                  pl.BlockSpec((c, LANES), lambda s, i: (r0 + s * nc + i, 0)),
                  pl.BlockSpec((ns, CONV_W - 1, 3 * w), lambda s, i: (s, 0, 0)),
                  pl.BlockSpec((CONV_W, 3 * w), lambda s, i: (0, 0)),
                  st,
                  pl.BlockSpec((SUBLANES, LANES), lambda s, i: (0, 0)),
                  pl.BlockSpec((1, dk), lambda s, i: (0, 0))],
        out_specs=[pl.BlockSpec((c, w), lambda s, i: (s * nc + i, 0)), st],
        out_shape=[jax.ShapeDtypeStruct((nb * nc * c, w), BF16),
                   jax.ShapeDtypeStruct((nb * ns, nh, dk, dk), F32)],
        scratch_shapes=scratch,
        compiler_params=_params(("parallel", "arbitrary"), 40), name="deltanet",
    )(seg_a, seg_a, seg_d, conv0, conv_w, h0, pa, ng.reshape(1, dk))


def _ssd_body(x_ref, z_ref, sm_ref, cst_ref, cw_ref, cb_ref, h0_ref, pa_ref, ng_ref, y_ref, ht_ref,
              h_scr, buf, cv, y_scr, *ov, c, segs, ng, nr, hp, ns, off):
    ci = pl.program_id(1)

    @pl.when(ci == 0)
    def _():
        h_scr[...] = h0_ref[...]

    if len(segs) > 1:
        _conv_silu_packed(x_ref, cst_ref, cw_ref, cb_ref, buf, ov[0], cv, c, segs[0][1] - segs[0][0])
    else:
        _conv_silu_rows(x_ref, cst_ref, cw_ref, cb_ref, buf, cv, c, ci)
    gw = nr * hp
    inner = ng * gw
    shift = hp.bit_length() - 1
    sm = sm_ref[...]
    pa = pa_ref[...]
    dt_raw = _softplus(sm + pa[1:2])
    dt_all, gc_all, gct, eg_all = [], [], [], []
    for seg in segs:
        live = _live_rows(c, seg)
        dt_all.append(dt_raw if live is None else jnp.where(live, dt_raw, 0.0))
        gc_all.append(_chunk_cumsum(dt_all[-1] * (-jnp.exp(pa[0:1])), c))
        gct.append(_rows_as_lanes(gc_all[-1], c))
        eg_all.append(jnp.exp(gc_all[-1]))
    tt = lax.broadcasted_iota(jnp.int32, (c, c), 0)
    ss = lax.broadcasted_iota(jnp.int32, (c, c), 1)
    incl = tt >= ss
    lane_head = lax.broadcasted_iota(jnp.int32, (c, gw), 1) >> shift
    lane_head1 = lax.broadcasted_iota(jnp.int32, (1, gw), 1) >> shift
    row_head = lax.broadcasted_iota(jnp.int32, (gw, ns), 0) >> shift
    gs = range(ng)
    units = [(s, g) for s in range(len(segs)) for g in gs]
    us = range(len(units))

    bg = [cv[:, inner + g * ns:inner + (g + 1) * ns] for g in gs]
    cg = [cv[:, inner + (ng + g) * ns:inner + (ng + g + 1) * ns] for g in gs]
    xg = [cv[:, g * gw:(g + 1) * gw] for g in gs]
    hg = [h_scr[s, g] for s, g in units]
    cb = [_bdot_nt(cg[g], bg[g]) for g in gs]
    ch = [_bdot_nt(cg[g], hg[u]) for u, (s, g) in enumerate(units)]
    xdt, xkd, yg, hdec = [], [], [], []
    for u, (s, g) in enumerate(units):
        dtb = jnp.zeros((c, gw), F32)
        egb = jnp.zeros((c, gw), F32)
        kdb = jnp.zeros((c, gw), F32)
        skip = jnp.zeros((1, gw), F32)
        hd = jnp.zeros((gw, ns), F32)
        for r in range(nr):
            col = off + g * nr + r
            gcol = gc_all[s][:, col:col + 1]
            glast = gc_all[s][c - 1:c, col:col + 1]
            seg = lane_head == r
            dtb = jnp.where(seg, dt_all[s][:, col:col + 1], dtb)
            egb = jnp.where(seg, eg_all[s][:, col:col + 1], egb)
            kdb = jnp.where(seg, jnp.exp(glast - gcol), kdb)
            skip = jnp.where(lane_head1 == r, pa[2:3, col:col + 1], skip)
            hd = jnp.where(row_head == r, jnp.exp(glast), hd)
        xdt.append(xg[g] * dtb)
        xkd.append(xdt[u] * kdb)
        yg.append(ch[u] * egb + skip * xg[g])
        hdec.append(hd)
    for r in range(nr):
        part = []
        for u, (s, g) in enumerate(units):
            col = off + g * nr + r
            lm = jnp.exp(jnp.where(incl, gc_all[s][:, col:col + 1] - gct[s][col:col + 1, 0:c], NEG))
            part.append(_bdot(cb[g] * lm, jnp.where(lane_head == r, xdt[u], 0.0)))
        yg = [yg[u] + part[u] for u in us]
    hn = [hdec[u] * hg[u] + _bdot_tn(xkd[u], bg[g]) for u, (s, g) in enumerate(units)]
    for u, (s, g) in enumerate(units):
        h_scr[s, g] = hn[u]
    for g in gs:
        yv = _by_sequence([yg[s * ng + g] for s in range(len(segs))], segs, c, gw)
        y_scr[:, g * gw:(g + 1) * gw] = yv * _silu(z_ref[:, g * gw:(g + 1) * gw])

    y = y_scr[...]
    y = y * lax.rsqrt(jnp.mean(y * y, axis=-1, keepdims=True) + EPS) * ng_ref[...]
    y_ref[...] = y.astype(y_ref.dtype)

    @pl.when(ci == pl.num_programs(1) - 1)
    def _():
        ht_ref[...] = h_scr[...]


def ssd(seg_b, seg_d, row0, nb, nc, c, segs, conv0, conv_w, conv_b, h0, pa, ngain, ng, nr, hp, ns, off):
    inner = ng * nr * hp
    sconv = inner + 2 * ng * ns
    r0 = row0 // c
    nseq = len(segs)
    glob = lambda s, i: (r0 + s * nc + i, 0)
    st = pl.BlockSpec((nseq, ng, nr * hp, ns), lambda s, i: (s, 0, 0, 0))
    scratch = [pltpu.VMEM((nseq, ng, nr * hp, ns), F32), pltpu.VMEM((c + SUBLANES, sconv), F32),
               pltpu.VMEM((c, sconv), F32), pltpu.VMEM((c, inner), F32)]
    if nseq > 1:
        scratch.append(pltpu.VMEM((c + SUBLANES, sconv), F32))
    return pl.pallas_call(
        functools.partial(_ssd_body, c=c, segs=segs, ng=ng, nr=nr, hp=hp, ns=ns, off=off), grid=(nb, nc),
        in_specs=[pl.BlockSpec((c, sconv), glob),
                  pl.BlockSpec((c, inner), lambda s, i: (r0 + s * nc + i, sconv // inner)),
                  pl.BlockSpec((c, LANES), glob),
                  pl.BlockSpec((nseq, CONV_W - 1, sconv), lambda s, i: (s, 0, 0)),
                  pl.BlockSpec((CONV_W, sconv), lambda s, i: (0, 0)),
                  pl.BlockSpec((1, sconv), lambda s, i: (0, 0)),
                  st,
                  pl.BlockSpec((SUBLANES, LANES), lambda s, i: (0, 0)),
                  pl.BlockSpec((1, inner), lambda s, i: (0, 0))],
        out_specs=[pl.BlockSpec((c, inner), lambda s, i: (s * nc + i, 0)), st],
        out_shape=[jax.ShapeDtypeStruct((nb * nc * c, inner), BF16),
                   jax.ShapeDtypeStruct((nb * nseq, ng, nr * hp, ns), F32)],
        scratch_shapes=scratch,
        compiler_params=_params(("parallel", "arbitrary"), 40), name="ssd",
    )(seg_b, seg_b, seg_d, conv0, conv_w, conv_b.reshape(1, sconv), h0, pa, ngain.reshape(1, inner))


def _ca_body(q_ref, k_ref, v_ref, o_ref, *, nh, dh, nseq, ls):
    scale = dh ** -0.5
    rows = q_ref.shape[0]
    cols = [slice(h * dh, (h + 1) * dh) for h in range(nh)]
    if len(k_ref.shape) == 4:
        mem = k_ref.shape[1]
        qs = jnp.concatenate([q_ref[:, cols[h]] for h in range(nh)], axis=0)
        col_head = lax.rem(lax.broadcasted_iota(jnp.int32, (nh * rows, mem * nh), 1), nh)
        row_head = lax.div(lax.broadcasted_iota(jnp.int32, (nh * rows, mem * nh), 0), rows)
        own = col_head == row_head
        o = []
        for b in range(nseq):
            s = jnp.where(own, _bdot_nt(qs, k_ref[b].reshape(mem * nh, dh)) * scale, NEG)
            e = jnp.exp(s - jnp.max(s, axis=-1, keepdims=True))
            p = e / jnp.sum(e, axis=-1, keepdims=True)
            ob = _bdot(p, v_ref[b].reshape(mem * nh, dh))
            o += [ob[h * rows:(h + 1) * rows] for h in range(nh)]
    else:
        units = [(b, h) for b in range(nseq) for h in range(nh)]
        s = [_bdot_nt(q_ref[:, cols[h]], k_ref[b, :, cols[h]]) * scale for b, h in units]
        e = [jnp.exp(x - jnp.max(x, axis=-1, keepdims=True)) for x in s]
        p = [x / jnp.sum(x, axis=-1, keepdims=True) for x in e]
        o = [_bdot(p[i], v_ref[b, :, cols[h]]) for i, (b, h) in enumerate(units)]
    row = lax.broadcasted_iota(jnp.int32, (rows, dh), 0)
    for h in range(nh):
        val = o[(nseq - 1) * nh + h]
        for b in range(nseq - 2, -1, -1):
            val = jnp.where(row < (b + 1) * ls, o[b * nh + h], val)
        o_ref[:, cols[h]] = val.astype(o_ref.dtype)


def cross_attention(q2d, row0, nrows, tl, nseq, mem_k, mem_v, nh, dh):
    r0 = row0 // tl
    wd = nh * dh
    per = nrows // tl * nseq // mem_k.shape[0]
    if mem_k.ndim == 4:
        kv = pl.BlockSpec((nseq,) + mem_k.shape[1:], lambda i: (i // per, 0, 0, 0))
    else:
        kv = pl.BlockSpec((nseq,) + mem_k.shape[1:], lambda i: (i // per, 0, 0))
    return pl.pallas_call(
        functools.partial(_ca_body, nh=nh, dh=dh, nseq=nseq, ls=tl // nseq), grid=(nrows // tl,),
        in_specs=[pl.BlockSpec((tl, wd), lambda i: (r0 + i, 0)), kv, kv],
        out_specs=pl.BlockSpec((tl, wd), lambda i: (i, 0)),
        out_shape=jax.ShapeDtypeStruct((nrows, wd), BF16),
        compiler_params=_params(("parallel",), 32), name="cross_attention",
    )(q2d, mem_k, mem_v)


def _extract_top(x, n):
    out = []
    for it in range(n):
        m = jnp.max(x, axis=0, keepdims=True)
        out.append(m)
        if it + 1 < n:
            x = jnp.where(x == m, -jnp.inf, x)
    return out


def _pair_candidates(v1, v2, topk):
    v1s = jnp.concatenate(v1, axis=0)
    v2s = jnp.concatenate(v2, axis=0)
    pieces = []
    a = 0
    while a < topk:
        nb = topk // (a + 1)
        if nb == 1 and a % SUBLANES == 0:
            pieces.append(v1s[a:] + v2[0])
            break
        pieces.append(v1[a] + v2s[:min(-(-nb // SUBLANES) * SUBLANES, topk)])
        a += 1
    return jnp.concatenate(pieces, axis=0)


def _peer_route_body(q_ref, keys_ref, s1_ref, e1_ref, s2_ref, e2_ref, thr_ref, *, nh, dq, topk):
    for h in range(nh):
        s1 = _bdot_nt(keys_ref[2 * h], q_ref[:, (2 * h) * dq:(2 * h + 1) * dq])
        s2 = _bdot_nt(keys_ref[2 * h + 1], q_ref[:, (2 * h + 1) * dq:(2 * h + 2) * dq])
        v1 = _extract_top(s1, topk)
        v2 = _extract_top(s2, topk)
        tops = _extract_top(_pair_candidates(v1, v2, topk), topk)
        zsum = jnp.exp(tops[0] - tops[0])
        for a in range(1, topk):
            zsum = zsum + jnp.exp(tops[a] - tops[0])
        s1_ref[h] = s1
        s2_ref[h] = s2
        e1_ref[h] = jnp.exp(s1 - v1[0]) / zsum
        e2_ref[h] = jnp.exp(s2 - v2[0])
        thr_ref[h:h + 1, :] = tops[topk - 1]


def peer_route(q, keys, nh, nk, dq, tm=256):
    m = q.shape[0]
    tm = _tile(m, tm)
    big = pl.BlockSpec((nh, nk, tm), lambda i: (0, 0, i))
    shape = jax.ShapeDtypeStruct((nh, nk, m), F32)
    return pl.pallas_call(
        functools.partial(_peer_route_body, nh=nh, dq=dq, topk=PEER_TOPK), grid=(m // tm,),
        in_specs=[pl.BlockSpec((tm, 2 * nh * dq), lambda i: (i, 0)),
                  pl.BlockSpec((2 * nh, nk, dq), lambda i: (0, 0, 0))],
        out_specs=[big, big, big, big, pl.BlockSpec((nh, tm), lambda i: (0, i))],
        out_shape=[shape, shape, shape, shape, jax.ShapeDtypeStruct((nh, m), F32)],
        compiler_params=_params(("parallel",), 32), name="peer_route",
    )(q, keys)


def _peer_body(hft_ref, u_ref, vt_ref, s1_ref, e1_ref, s2_ref, e2_ref, thr_ref, ot_ref, st0, st1, act0, act1,
               *, nh, nk, nrow, dchunk):
    ei = pl.program_id(1)
    d, tm = hft_ref.shape

    @pl.when(ei == 0)
    def _():
        ot_ref[...] = jnp.zeros_like(ot_ref)
        st1[...] = jnp.zeros_like(st1)
        act0[...] = jnp.zeros_like(act0)

    pieces = nrow
    prow = nrow * nk // pieces

    def step(a):
        st_new, st_old = (st0, st1) if a == 0 else (st1, st0)
        act_old, act_new = (act0, act1) if a == 0 else (act1, act0)

        def scores(p):
            rows = slice(p * prow, (p + 1) * prow)
            st_new[rows, :] = jnp.dot(u_ref[rows, :], hft_ref[...], preferred_element_type=F32)

        def apply_values(j):
            ds = slice(j * dchunk, (j + 1) * dchunk)
            ot_ref[ds, :] += jnp.dot(vt_ref[ds, :], act_old[...], preferred_element_type=F32)

        def gate_block(r, tb):
            ls = slice(tb * LANES, (tb + 1) * LANES)
            rows = slice(r * nk, (r + 1) * nk)
            gate = jnp.zeros((nk, LANES), F32)
            for h in range(nh):
                hit = (s2_ref[h, :, ls] + s1_ref[r, h:h + 1, ls]) >= thr_ref[h:h + 1, ls]
                gate = gate + jnp.where(hit, e2_ref[h, :, ls] * e1_ref[r, h:h + 1, ls], 0.0)
            act_new[rows, ls] = (_gelu_tanh(st_old[rows, ls]) * gate).astype(BF16)

        blocks = [(r, tb) for r in range(nrow) for tb in range(tm // LANES)]
        nd = d // dchunk
        per = -(-len(blocks) // nd)
        every = max(nd // pieces, 1)
        for j in range(max(nd, pieces * every)):
            if j % every == 0 and j // every < pieces:
                scores(j // every)
            if j < nd:
                apply_values(j)
            for r, tb in blocks[j * per:(j + 1) * per]:
                gate_block(r, tb)

    @pl.when(ei % 2 == 0)
    def _():
        step(0)

    @pl.when(ei % 2 == 1)
    def _():
        step(1)


def peer_experts(hft, u, vt, s1g, e1g, s2, e2, thr, nh, nk, tm=512, nrow=4):
    d, m = hft.shape
    tm = _tile(m, tm)
    et = nrow * nk
    once = pl.Buffered(1)
    ne = nk // nrow
    tile = lambda e, lag: jnp.clip(e - lag, 0, ne - 1)
    return pl.pallas_call(
        functools.partial(_peer_body, nh=nh, nk=nk, nrow=nrow, dchunk=_tile(d, 512)), grid=(m // tm, ne + 2),
        in_specs=[pl.BlockSpec((d, tm), lambda t, e: (0, t), pipeline_mode=once),
                  pl.BlockSpec((et, d), lambda t, e: (tile(e, 0), 0)),
                  pl.BlockSpec((d, et), lambda t, e: (0, tile(e, 2))),
                  pl.BlockSpec((nrow, nh, tm), lambda t, e: (tile(e, 1), 0, t)),
                  pl.BlockSpec((nrow, nh, tm), lambda t, e: (tile(e, 1), 0, t)),
                  pl.BlockSpec((nh, nk, tm), lambda t, e: (0, 0, t), pipeline_mode=once),
                  pl.BlockSpec((nh, nk, tm), lambda t, e: (0, 0, t), pipeline_mode=once),
                  pl.BlockSpec((nh, tm), lambda t, e: (0, t))],
        out_specs=pl.BlockSpec((d, tm), lambda t, e: (0, t)),
        out_shape=jax.ShapeDtypeStruct((d, m), F32),
        scratch_shapes=[pltpu.VMEM((et, tm), F32), pltpu.VMEM((et, tm), F32),
                        pltpu.VMEM((et, tm), BF16), pltpu.VMEM((et, tm), BF16)],
        compiler_params=_params(("parallel", "arbitrary"), 56), name="peer_experts",
    )(hft, u, vt, s1g, e1g, s2, e2, thr)


def _lane_row(nrows, pieces):
    out = jnp.zeros((SUBLANES, LANES), F32)
    for row, off, vec in pieces:
        out = out.at[row, off:off + vec.shape[0]].set(vec.astype(F32))
    return out


def kernel(x_prompt, x_sample, cache_mem_k, cache_mem_v, state_dn_conv, state_dn_rec, state_ssm_conv,
           state_ssm_rec, mem_prompt, norm_mix_g, w_in, dn_conv_w, dn_a_log, dn_dt_bias, dn_norm_g, dn_w_out,
           ssm_conv_w, ssm_conv_b, ssm_a_log, ssm_dt_bias, ssm_d, ssm_norm_g, ssm_w_out, w_o, norm_ca_g,
           ca_w_q, ca_w_k, ca_w_v, ca_w_o, norm_ffn_g, peer_w_q, peer_keys, peer_u, peer_v, final_norm_g):
    depth = w_in.shape[0]
    bp, lp, d = x_prompt.shape
    bs, ls, _ = x_sample.shape
    _, _, dnh, dk, _ = state_dn_rec.shape
    _, _, sh, hp, ns = state_ssm_rec.shape
    dnw = dnh * dk
    inner = sh * hp
    sconv = state_ssm_conv.shape[-1]
    sg = (sconv - inner) // (2 * ns)
    sr = sh // sg
    _, _, mem, cah, cad = cache_mem_k.shape
    caw = cah * cad
    _, pnh, _, nk, dq = peer_keys.shape
    packed = SUBLANES % ls == 0 and ls >= CONV_W - 1 and bs % (SUBLANES // ls) == 0
    lrow = ls if packed else -(-ls // SUBLANES) * SUBLANES
    cs = SUBLANES if packed else lrow
    nseq = cs // lrow
    s_segs = tuple((j * lrow, j * lrow + ls) for j in range(nseq))
    assert lp % CHUNK == 0 and cs <= CHUNK and dk == LANES and ns == LANES and nk == LANES and dq == LANES
    assert 2 * dnh + sh <= LANES and hp & (hp - 1) == 0
    mp, ms = bp * lp, bs * lrow
    ssm_off = 2 * dnh

    x = [x_prompt.reshape(mp, d), jnp.pad(x_sample, ((0, 0), (0, lrow - ls), (0, 0))).reshape(ms, d)]

    outs = [[] for _ in range(10)]
    for l in range(depth):
        wl = w_in[l]
        o1 = 3 * dnw + dnw
        o2 = o1 + 2 * dnh
        o3 = o2 + inner
        o4 = o3 + sconv
        o5 = o4 + sh
        w_a, w_b, w_c, w_d = split_in_proj(wl, o1, o2, o3, o4, o5)

        hn = rmsnorm(x, norm_mix_g[l], BF16)
        seg_a = matmul(hn, w_a)
        seg_b = matmul(hn, w_b)
        seg_c = matmul(hn, w_c)
        seg_d = matmul(hn, w_d)

        dn_pa = _lane_row(SUBLANES, [(0, dnh, dn_a_log[l]), (1, dnh, dn_dt_bias[l])])
        ssm_pa = _lane_row(SUBLANES, [(0, ssm_off, ssm_a_log[l]), (1, ssm_off, ssm_dt_bias[l]), (2, ssm_off, ssm_d[l])])
        groups = (
            (0, bp, lp // CHUNK, CHUNK, ((0, CHUNK),), jnp.zeros((bp, CONV_W - 1, 3 * dnw), F32),
             jnp.zeros((bp, dnh, dk, dk), F32), jnp.zeros((bp, CONV_W - 1, sconv), F32),
             jnp.zeros((bp, sg, sr * hp, ns), F32)),
            (mp, bs // nseq, 1, cs, s_segs, state_dn_conv[l], state_dn_rec[l],
             state_ssm_conv[l], state_ssm_rec[l].reshape(bs, sg, sr * hp, ns)),
        )
        o_dn, o_ssm, dn_h, ssm_h = [], [], [], []
        for row0, nb, nc, c, segs, dc0, dh0, sc0, sh0 in groups:
            od, hd = deltanet(seg_a, seg_d, row0, nb, nc, c, segs, dc0, dn_conv_w[l], dh0, dn_pa, dn_norm_g[l],
                              dnh, dk)
            osd, hsd = ssd(seg_b, seg_d, row0, nb, nc, c, segs, sc0, ssm_conv_w[l], ssm_conv_b[l], sh0, ssm_pa,
                           ssm_norm_g[l], sg, sr, hp, ns, ssm_off)
            o_dn.append(od)
            o_ssm.append(osd)
            dn_h.append(hd)
            ssm_h.append(hsd.reshape(-1, sh, hp, ns))

        mixed = gated_mix(o_dn, dn_w_out[l].astype(BF16), o_ssm, ssm_w_out[l].astype(BF16), seg_c)
        x1 = matmul(mixed, w_o[l].astype(BF16), res=x)

        hc = rmsnorm(x1, norm_ca_g[l], BF16)
        qc = matmul(hc, ca_w_q[l].astype(BF16))
        memp = mem_prompt.reshape(bp * mem, d)
        mk = matmul(memp, ca_w_k[l].astype(BF16))
        mv = matmul(memp, ca_w_v[l].astype(BF16))
        oc = [cross_attention(qc, 0, mp, _tile(lp, 512), 1, mk.reshape(bp, mem, caw), mv.reshape(bp, mem, caw),
                              cah, cad),
              cross_attention(qc, mp, ms, cs, nseq, cache_mem_k[l], cache_mem_v[l], cah, cad)]
        x2 = matmul(oc, ca_w_o[l].astype(BF16), res=x1)

        hf, hft = rmsnorm(x2, norm_ffn_g[l], BF16, with_transpose=True)
        pq = matmul(hf, peer_w_q[l].astype(BF16))
        s1, e1, s2, e2, thr = peer_route(pq, peer_keys[l].reshape(2 * pnh, nk, dq).astype(BF16), pnh, nk, dq)
        pot = peer_experts(hft, peer_u[l].astype(BF16), transpose_cast(peer_v[l], BF16),
                           s1.transpose(1, 0, 2), e1.transpose(1, 0, 2), s2, e2, thr, pnh, nk)
        if l == depth - 1:
            y_prompt = add_rows(x2, pot, final_norm_g, 0, mp).reshape(bp, lp, d)
            y_sample = add_rows(x2, pot, final_norm_g, mp, ms).reshape(bs, lrow, d)[:, :ls]
        else:
            x = add_rows(x2, pot, None, 0, mp + ms)

        def tail(seg, row0, nb, sl, nvalid, lo, hi, state0):
            take = min(CONV_W - 1, nvalid)
            rows = row0 + np.arange(nb)[:, None] * sl + np.arange(nvalid - take, nvalid)[None, :]
            u = jnp.take(seg, rows.reshape(-1), axis=0)[:, lo:hi].reshape(nb, take, hi - lo)
            if take < CONV_W - 1:
                u = jnp.concatenate([state0[:, take:], u], axis=1)
            return u

        outs[0].append(mk.reshape(bp, mem, cah, cad))
        outs[1].append(mv.reshape(bp, mem, cah, cad))
        outs[2].append(tail(seg_a, 0, bp, lp, lp, 0, 3 * dnw, groups[0][5]))
        outs[3].append(dn_h[0])
        outs[4].append(tail(seg_b, 0, bp, lp, lp, 0, sconv, groups[0][7]))
        outs[5].append(ssm_h[0])
        outs[6].append(tail(seg_a, mp, bs, lrow, ls, 0, 3 * dnw, state_dn_conv[l]))
        outs[7].append(dn_h[1])
        outs[8].append(tail(seg_b, mp, bs, lrow, ls, 0, sconv, state_ssm_conv[l]))
        outs[9].append(ssm_h[1])

    return (y_prompt, y_sample) + tuple(jnp.stack(o) for o in outs)
```

```python
import functools
import math

import jax
import jax.numpy as jnp
import numpy as np
from jax import lax
from jax.experimental import pallas as pl
from jax.experimental.pallas import tpu as pltpu

F32 = jnp.float32
BF16 = jnp.bfloat16

EPS = 1e-6
CONV_W = 4
CHUNK = 64
PEER_TOPK = 16
LANES = 128
SUBLANES = 8
NEG = -1e30
MIB = 1 << 20


def _params(sem, vmem_mib):
    return pltpu.CompilerParams(dimension_semantics=sem, vmem_limit_bytes=vmem_mib * MIB)


def _tile(n, pref):
    t = min(pref, n)
    while n % t:
        t //= 2
    return t


def _sigmoid(x):
    return 1.0 / (1.0 + jnp.exp(-x))


def _silu(x):
    return x * _sigmoid(x)


def _softplus(x):
    return jnp.maximum(x, 0.0) + jnp.log1p(jnp.exp(-jnp.abs(x)))


def _gelu_tanh(x):
    return x * (0.5 * (1.0 + jnp.tanh(0.7978845608028654 * (x + 0.044715 * (x * x * x)))))


def _bdot(a, b):
    return jnp.dot(a.astype(BF16), b.astype(BF16), preferred_element_type=F32)


def _bdot_nt(a, b):
    return lax.dot_general(a.astype(BF16), b.astype(BF16), (((1,), (1,)), ((), ())),
                           preferred_element_type=F32)


def _bdot_tn(a, b):
    return lax.dot_general(a.astype(BF16), b.astype(BF16), (((0,), (0,)), ((), ())),
                           preferred_element_type=F32)


def _as_pieces(a):
    return list(a) if isinstance(a, (list, tuple)) else [a]


def _piece_tile(pieces, pref):
    return _tile(math.gcd(*[p.shape[0] for p in pieces]), pref)


def _piece_ends(pieces, tm):
    ends, tot = [], 0
    for p in pieces:
        tot += p.shape[0] // tm
        ends.append(tot)
    return tuple(ends)


def _piece_specs(pieces, tm, cols, row_of, col_of):
    specs, start = [], 0
    for p in pieces:
        n = p.shape[0] // tm
        specs.append(pl.BlockSpec(
            (tm, cols), lambda *g, s=start, n=n: (jnp.clip(row_of(*g) - s, 0, n - 1), col_of(*g))))
        start += n
    return specs


def _piece_value(refs, tile, ends):
    val = refs[-1][...]
    for k in range(len(refs) - 2, -1, -1):
        val = jnp.where(tile < ends[k], refs[k][...], val)
    return val


def _rmsnorm_body(*refs, nx, ends, both):
    x = _piece_value(refs[:nx], pl.program_id(0), ends)
    y = x * lax.rsqrt(jnp.mean(x * x, axis=-1, keepdims=True) + EPS) * refs[nx][...]
    refs[nx + 1][...] = y.astype(refs[nx + 1].dtype)
    if both:
        refs[nx + 2][...] = y.T.astype(refs[nx + 2].dtype)


def _addnorm_body(x_ref, pt_ref, g_ref, o_ref):
    x = x_ref[...] + pt_ref[...].T
    y = x * lax.rsqrt(jnp.mean(x * x, axis=-1, keepdims=True) + EPS)
    o_ref[...] = (y * g_ref[...]).astype(o_ref.dtype)


def _add_body(x_ref, pt_ref, o_ref):
    o_ref[...] = x_ref[...] + pt_ref[...].T


def rmsnorm(x, g, out_dtype, tm=256, with_transpose=False):
    xs = _as_pieces(x)
    d = xs[0].shape[1]
    m = sum(p.shape[0] for p in xs)
    tm = _piece_tile(xs, tm)
    row = pl.BlockSpec((tm, d), lambda i: (i, 0))
    outs, shapes = [row], [jax.ShapeDtypeStruct((m, d), out_dtype)]
    if with_transpose:
        outs.append(pl.BlockSpec((d, tm), lambda i: (0, i)))
        shapes.append(jax.ShapeDtypeStruct((d, m), out_dtype))
    res = pl.pallas_call(
        functools.partial(_rmsnorm_body, nx=len(xs), ends=_piece_ends(xs, tm), both=with_transpose),
        grid=(m // tm,),
        in_specs=_piece_specs(xs, tm, d, lambda i: i, lambda i: 0) + [pl.BlockSpec((1, d), lambda i: (0, 0))],
        out_specs=outs, out_shape=shapes,
        compiler_params=_params(("parallel",), 40), name="rmsnorm",
    )(*xs, g.reshape(1, d))
    return res if with_transpose else res[0]


def add_rows(x, pt, g, row0, nrows, tm=256):
    d = x.shape[1]
    tm = _tile(math.gcd(row0, nrows) if row0 else nrows, tm)
    r0 = row0 // tm
    ins = [pl.BlockSpec((tm, d), lambda i: (r0 + i, 0)), pl.BlockSpec((d, tm), lambda i: (0, r0 + i))]
    if g is None:
        body, args = _add_body, (x, pt)
    else:
        body, args = _addnorm_body, (x, pt, g.reshape(1, d))
        ins.append(pl.BlockSpec((1, d), lambda i: (0, 0)))
    return pl.pallas_call(
        body, grid=(nrows // tm,), in_specs=ins, out_specs=pl.BlockSpec((tm, d), lambda i: (i, 0)),
        out_shape=jax.ShapeDtypeStruct((nrows, d), F32),
        compiler_params=_params(("parallel",), 40), name="add_norm",
    )(*args)


def _transpose_cast_body(x_ref, o_ref):
    o_ref[...] = x_ref[...].T.astype(o_ref.dtype)


def transpose_cast(x, dtype, tr=512, tc=1024):
    r, c = x.shape
    tr, tc = _tile(r, tr), _tile(c, tc)
    return pl.pallas_call(
        _transpose_cast_body, grid=(r // tr, c // tc),
        in_specs=[pl.BlockSpec((tr, tc), lambda i, j: (i, j))],
        out_specs=pl.BlockSpec((tc, tr), lambda i, j: (j, i)),
        out_shape=jax.ShapeDtypeStruct((c, r), dtype),
        compiler_params=_params(("parallel", "parallel"), 32), name="transpose_cast",
    )(x)


def _mm_body(*refs, na, nr, a_ends, r_ends, w_rows_out):
    w_ref, o_ref = refs[na], refs[-1]
    i = pl.program_id(1)
    a = _piece_value(refs[:na], i, a_ends).astype(BF16)
    acc = _bdot_nt(a, w_ref[...]) if w_rows_out else jnp.dot(a, w_ref[...], preferred_element_type=F32)
    if nr:
        acc = _piece_value(refs[na + 1:na + 1 + nr], i, r_ends) + acc
    o_ref[...] = acc.astype(o_ref.dtype)


def matmul(a, w, res=None, out_dtype=F32, tm=512, tn=1024, w_rows_out=False):
    a_p = _as_pieces(a)
    r_p = _as_pieces(res) if res is not None else []
    k = a_p[0].shape[1]
    m = sum(p.shape[0] for p in a_p)
    n = w.shape[0] if w_rows_out else w.shape[1]
    tm, tn = _piece_tile(a_p + r_p, tm), _tile(n, tn)
    row_of, zero = (lambda j, i: i), (lambda j, i: 0)
    w_spec = pl.BlockSpec((tn, k), lambda j, i: (j, 0)) if w_rows_out else pl.BlockSpec((k, tn), lambda j, i: (0, j))
    ins = _piece_specs(a_p, tm, k, row_of, zero) + [w_spec] + _piece_specs(r_p, tm, tn, row_of, lambda j, i: j)
    return pl.pallas_call(
        functools.partial(_mm_body, na=len(a_p), nr=len(r_p), a_ends=_piece_ends(a_p, tm),
                          r_ends=_piece_ends(r_p, tm), w_rows_out=w_rows_out),
        grid=(n // tn, m // tm),
        in_specs=ins, out_specs=pl.BlockSpec((tm, tn), lambda j, i: (i, j)),
        out_shape=jax.ShapeDtypeStruct((m, n), out_dtype),
        compiler_params=_params(("parallel", "parallel"), 48), name="matmul",
    )(*a_p, w, *r_p)


def _mix_body(*refs, n1, n2, ends1, ends2):
    w1_ref, w2_ref = refs[n1], refs[n1 + 1 + n2]
    g1_ref, g2_ref, o_ref = refs[-3], refs[-2], refs[-1]
    i = pl.program_id(1)
    y1 = jnp.dot(_piece_value(refs[:n1], i, ends1), w1_ref[...], preferred_element_type=F32)
    y2 = jnp.dot(_piece_value(refs[n1 + 1:n1 + 1 + n2], i, ends2), w2_ref[...], preferred_element_type=F32)
    o_ref[...] = (_sigmoid(g1_ref[...]) * y1 + _sigmoid(g2_ref[...]) * y2).astype(o_ref.dtype)


def gated_mix(a1, w1, a2, w2, gates, tm=512, tn=1024):
    p1, p2 = _as_pieces(a1), _as_pieces(a2)
    k1, k2 = p1[0].shape[1], p2[0].shape[1]
    m, n = gates.shape[0], w1.shape[1]
    tm, tn = _piece_tile(p1 + p2, tm), _tile(n, tn)
    nj = n // tn
    row_of, zero = (lambda j, i: i), (lambda j, i: 0)
    return pl.pallas_call(
        functools.partial(_mix_body, n1=len(p1), n2=len(p2), ends1=_piece_ends(p1, tm), ends2=_piece_ends(p2, tm)),
        grid=(nj, m // tm),
        in_specs=(_piece_specs(p1, tm, k1, row_of, zero) + [pl.BlockSpec((k1, tn), lambda j, i: (0, j))]
                  + _piece_specs(p2, tm, k2, row_of, zero) + [pl.BlockSpec((k2, tn), lambda j, i: (0, j))]
                  + [pl.BlockSpec((tm, tn), lambda j, i: (i, j)), pl.BlockSpec((tm, tn), lambda j, i: (i, j + nj))]),
        out_specs=pl.BlockSpec((tm, tn), lambda j, i: (i, j)),
        out_shape=jax.ShapeDtypeStruct((m, n), BF16),
        compiler_params=_params(("parallel", "parallel"), 48), name="gated_mix",
    )(*p1, w1, *p2, w2, gates, gates)


def _conv_silu_rows(x_ref, st_ref, w_ref, b_ref, buf, cv, c, ci):
    k = CONV_W - 1

    @pl.when(ci == 0)
    def _():
        buf[SUBLANES - k:SUBLANES, :] = st_ref[0]

    buf[SUBLANES:SUBLANES + c, :] = x_ref[...]
    base = SUBLANES - k
    acc = buf[base:base + c, :] * w_ref[0:1, :]
    for j in range(1, CONV_W):
        acc = acc + buf[base + j:base + j + c, :] * w_ref[j:j + 1, :]
    if b_ref is not None:
        acc = acc + b_ref[...]
    cv[...] = _silu(acc)
    buf[0:SUBLANES, :] = buf[c:c + SUBLANES, :]


def _conv_silu_packed(x_ref, st_ref, w_ref, b_ref, buf, ov, cv, c, ls):
    k = CONV_W - 1
    ov[...] = jnp.zeros_like(ov)
    for j in range(c // ls):
        ov[SUBLANES + j * ls - k:SUBLANES + j * ls, :] = st_ref[j]
    buf[0:SUBLANES, :] = jnp.zeros((SUBLANES, buf.shape[1]), F32)
    buf[SUBLANES:SUBLANES + c, :] = x_ref[...]
    base = SUBLANES - k
    in_seq = lax.broadcasted_iota(jnp.int32, (c, buf.shape[1]), 0) & (ls - 1)
    acc = None
    for j in range(CONV_W):
        win = buf[base + j:base + j + c, :]
        if j < k:
            win = jnp.where(in_seq + j < k, ov[base + j:base + j + c, :], win)
        term = win * w_ref[j:j + 1, :]
        acc = term if acc is None else acc + term
    if b_ref is not None:
        acc = acc + b_ref[...]
    cv[...] = _silu(acc)


def _chunk_cumsum(g, c):
    tt = lax.broadcasted_iota(jnp.int32, (c, c), 0)
    ss = lax.broadcasted_iota(jnp.int32, (c, c), 1)
    tril = jnp.where(tt >= ss, 1.0, 0.0).astype(BF16)
    hi = g.astype(BF16)
    r1 = g - hi.astype(F32)
    mid = r1.astype(BF16)
    lo = (r1 - mid.astype(F32)).astype(BF16)
    dot = functools.partial(jnp.dot, preferred_element_type=F32)
    return dot(tril, hi) + dot(tril, mid) + dot(tril, lo)


def _rows_as_lanes(x, c):
    if c < LANES:
        x = jnp.concatenate([x, jnp.zeros((LANES - c, LANES), F32)], axis=0)
    return x.T


def _live_rows(c, seg):
    lo, hi = seg
    if (lo, hi) == (0, c):
        return None
    row = lax.broadcasted_iota(jnp.int32, (c, LANES), 0)
    return jnp.logical_and(row >= lo, row < hi)


def _by_sequence(vals, segs, c, width):
    out = vals[-1]
    if len(vals) > 1:
        row = lax.broadcasted_iota(jnp.int32, (c, width), 0)
        for s in range(len(vals) - 2, -1, -1):
            out = jnp.where(row < segs[s][1], vals[s], out)
    return out


def _dn_body(x_ref, z_ref, sm_ref, cst_ref, cw_ref, h0_ref, pa_ref, ng_ref, o_ref, ht_ref, h_scr, buf, cv, *ov,
             c, segs, nh, dk, nsq):
    ci = pl.program_id(1)

    @pl.when(ci == 0)
    def _():
        h_scr[...] = h0_ref[...]

    if len(segs) > 1:
        _conv_silu_packed(x_ref, cst_ref, cw_ref, None, buf, ov[0], cv, c, segs[0][1] - segs[0][0])
    else:
        _conv_silu_rows(x_ref, cst_ref, cw_ref, None, buf, cv, c, ci)
    w = nh * dk
    sm = sm_ref[...]
    pa = pa_ref[...]
    beta_raw = _sigmoid(sm)
    g_raw = -jnp.exp(pa[0:1]) * _softplus(sm + pa[1:2])
    beta_all, gc_all, gct, eg_all = [], [], [], []
    for seg in segs:
        live = _live_rows(c, seg)
        beta_all.append(beta_raw if live is None else jnp.where(live, beta_raw, 0.0))
        gc_all.append(_chunk_cumsum(g_raw if live is None else jnp.where(live, g_raw, 0.0), c))
        gct.append(_rows_as_lanes(gc_all[-1], c))
        eg_all.append(jnp.exp(gc_all[-1]))
    tt = lax.broadcasted_iota(jnp.int32, (c, c), 0)
    ss = lax.broadcasted_iota(jnp.int32, (c, c), 1)
    strict = tt > ss
    incl = tt >= ss
    scale = dk ** -0.5
    hs = range(nh)
    units = [(s, h) for s in range(len(segs)) for h in hs]
    us = range(len(units))

    q, k = [], []
    for h in hs:
        qh = cv[:, h * dk:(h + 1) * dk]
        kh = cv[:, w + h * dk:w + (h + 1) * dk]
        q.append(qh * lax.rsqrt(jnp.sum(qh * qh, axis=-1, keepdims=True) + EPS) * scale)
        k.append(kh * lax.rsqrt(jnp.sum(kh * kh, axis=-1, keepdims=True) + EPS))
    kk = [_bdot_nt(k[h], k[h]) for h in hs]
    qk = [_bdot_nt(q[h], k[h]) for h in hs]
    hh = [h_scr[s, h] for s, h in units]
    qh0 = [_bdot(q[h], hh[u]) for u, (s, h) in enumerate(units)]
    gcol = [gc_all[s][:, nh + h:nh + h + 1] for s, h in units]
    diff = [gcol[u] - gct[s][nh + h:nh + h + 1, 0:c] for u, (s, h) in enumerate(units)]
    bcol = [beta_all[s][:, h:h + 1] for s, h in units]
    egc = [eg_all[s][:, nh + h:nh + h + 1] for s, h in units]
    glast = [gc_all[s][c - 1:c, nh + h:nh + h + 1] for s, h in units]
    qm = [-(bcol[u] * kk[h] * jnp.exp(jnp.where(strict, diff[u], NEG))) for u, (s, h) in enumerate(units)]
    pw = [_bdot(qm[u], qm[u]) for u in us] if nsq else None
    for it in range(nsq):
        t = [_bdot(qm[u], pw[u]) for u in us]
        nxt = [_bdot(pw[u], pw[u]) for u in us] if it + 1 < nsq else None
        qm = [qm[u] + pw[u] + t[u] for u in us]
        pw = nxt
    rhs = [jnp.concatenate([cv[:, 2 * w + h * dk:2 * w + (h + 1) * dk] * bcol[u], k[h] * (bcol[u] * egc[u])], axis=1)
           for u, (s, h) in enumerate(units)]
    sol = [rhs[u] + _bdot(qm[u], rhs[u]) for u in us]
    wv = [sol[u][:, :dk] - _bdot(sol[u][:, dk:], hh[u]) for u in us]
    o = [qh0[u] * egc[u] + _bdot(qk[h] * jnp.exp(jnp.where(incl, diff[u], NEG)), wv[u])
         for u, (s, h) in enumerate(units)]
    hn = [jnp.exp(glast[u]) * hh[u] + _bdot_tn(k[h] * jnp.exp(glast[u] - gcol[u]), wv[u])
          for u, (s, h) in enumerate(units)]
    for u, (s, h) in enumerate(units):
        h_scr[s, h] = hn[u]
    for h in hs:
        oh = _by_sequence([o[s * nh + h] for s in range(len(segs))], segs, c, dk)
        oh = oh * lax.rsqrt(jnp.mean(oh * oh, axis=-1, keepdims=True) + EPS) * ng_ref[...]
        o_ref[:, h * dk:(h + 1) * dk] = (oh * _silu(z_ref[:, h * dk:(h + 1) * dk])).astype(o_ref.dtype)

    @pl.when(ci == pl.num_programs(1) - 1)
    def _():
        ht_ref[...] = h_scr[...]


def _num_squarings(segs):
    n, longest = 0, max(hi - lo for lo, hi in segs)
    while (2 << n) < longest:
        n += 1
    return n


def deltanet(seg_a, seg_d, row0, nb, nc, c, segs, conv0, conv_w, h0, pa, ng, nh, dk):
    w = nh * dk
    r0 = row0 // c
    ns = len(segs)
    st = pl.BlockSpec((ns, nh, dk, dk), lambda s, i: (s, 0, 0, 0))
    scratch = [pltpu.VMEM((ns, nh, dk, dk), F32), pltpu.VMEM((c + SUBLANES, 3 * w), F32),
               pltpu.VMEM((c, 3 * w), F32)]
    if ns > 1:
        scratch.append(pltpu.VMEM((c + SUBLANES, 3 * w), F32))
    return pl.pallas_call(
        functools.partial(_dn_body, c=c, segs=segs, nh=nh, dk=dk, nsq=_num_squarings(segs)), grid=(nb, nc),
        in_specs=[pl.BlockSpec((c, 3 * w), lambda s, i: (r0 + s * nc + i, 0)),
                  pl.BlockSpec((c, w), lambda s, i: (r0 + s * nc + i, 3)),
                  pl.BlockSpec((c, LANES), lambda s, i: (r0 + s * nc + i, 0)),
                  pl.BlockSpec((ns, CONV_W - 1, 3 * w), lambda s, i: (s, 0, 0)),
                  pl.BlockSpec((CONV_W, 3 * w), lambda s, i: (0, 0)),
                  st,
                  pl.BlockSpec((SUBLANES, LANES), lambda s, i: (0, 0)),
                  pl.BlockSpec((1, dk), lambda s, i: (0, 0))],
        out_specs=[pl.BlockSpec((c, w), lambda s, i: (s * nc + i, 0)), st],
        out_shape=[jax.ShapeDtypeStruct((nb * nc * c, w), BF16),
                   jax.ShapeDtypeStruct((nb * ns, nh, dk, dk), F32)],
        scratch_shapes=scratch,
        compiler_params=_params(("parallel", "arbitrary"), 40), name="deltanet",
    )(seg_a, seg_a, seg_d, conv0, conv_w, h0, pa, ng.reshape(1, dk))


def _ssd_body(x_ref, z_ref, sm_ref, cst_ref, cw_ref, cb_ref, h0_ref, pa_ref, ng_ref, y_ref, ht_ref,
              h_scr, buf, cv, y_scr, *ov, c, segs, ng, nr, hp, ns, off):
    ci = pl.program_id(1)

    @pl.when(ci == 0)
    def _():
        h_scr[...] = h0_ref[...]

    if len(segs) > 1:
        _conv_silu_packed(x_ref, cst_ref, cw_ref, cb_ref, buf, ov[0], cv, c, segs[0][1] - segs[0][0])
    else:
        _conv_silu_rows(x_ref, cst_ref, cw_ref, cb_ref, buf, cv, c, ci)
    gw = nr * hp
    inner = ng * gw
    shift = hp.bit_length() - 1
    sm = sm_ref[...]
    pa = pa_ref[...]
    dt_raw = _softplus(sm + pa[1:2])
    dt_all, gc_all, gct, eg_all = [], [], [], []
    for seg in segs:
        live = _live_rows(c, seg)
        dt_all.append(dt_raw if live is None else jnp.where(live, dt_raw, 0.0))
        gc_all.append(_chunk_cumsum(dt_all[-1] * (-jnp.exp(pa[0:1])), c))
        gct.append(_rows_as_lanes(gc_all[-1], c))
        eg_all.append(jnp.exp(gc_all[-1]))
    tt = lax.broadcasted_iota(jnp.int32, (c, c), 0)
    ss = lax.broadcasted_iota(jnp.int32, (c, c), 1)
    incl = tt >= ss
    lane_head = lax.broadcasted_iota(jnp.int32, (c, gw), 1) >> shift
    lane_head1 = lax.broadcasted_iota(jnp.int32, (1, gw), 1) >> shift
    row_head = lax.broadcasted_iota(jnp.int32, (gw, ns), 0) >> shift
    gs = range(ng)
    units = [(s, g) for s in range(len(segs)) for g in gs]
    us = range(len(units))

    bg = [cv[:, inner + g * ns:inner + (g + 1) * ns] for g in gs]
    cg = [cv[:, inner + (ng + g) * ns:inner + (ng + g + 1) * ns] for g in gs]
    xg = [cv[:, g * gw:(g + 1) * gw] for g in gs]
    hg = [h_scr[s, g] for s, g in units]
    cb = [_bdot_nt(cg[g], bg[g]) for g in gs]
    ch = [_bdot_nt(cg[g], hg[u]) for u, (s, g) in enumerate(units)]
    xdt, xkd, yg, hdec = [], [], [], []
    for u, (s, g) in enumerate(units):
        dtb = jnp.zeros((c, gw), F32)
        egb = jnp.zeros((c, gw), F32)
        kdb = jnp.zeros((c, gw), F32)
        skip = jnp.zeros((1, gw), F32)
        hd = jnp.zeros((gw, ns), F32)
        for r in range(nr):
            col = off + g * nr + r
            gcol = gc_all[s][:, col:col + 1]
            glast = gc_all[s][c - 1:c, col:col + 1]
            seg = lane_head == r
            dtb = jnp.where(seg, dt_all[s][:, col:col + 1], dtb)
            egb = jnp.where(seg, eg_all[s][:, col:col + 1], egb)
            kdb = jnp.where(seg, jnp.exp(glast - gcol), kdb)
            skip = jnp.where(lane_head1 == r, pa[2:3, col:col + 1], skip)
            hd = jnp.where(row_head == r, jnp.exp(glast), hd)
        xdt.append(xg[g] * dtb)
        xkd.append(xdt[u] * kdb)
        yg.append(ch[u] * egb + skip * xg[g])
        hdec.append(hd)
    for r in range(nr):
        part = []
        for u, (s, g) in enumerate(units):
            col = off + g * nr + r
            lm = jnp.exp(jnp.where(incl, gc_all[s][:, col:col + 1] - gct[s][col:col + 1, 0:c], NEG))
            part.append(_bdot(cb[g] * lm, jnp.where(lane_head == r, xdt[u], 0.0)))
        yg = [yg[u] + part[u] for u in us]
    hn = [hdec[u] * hg[u] + _bdot_tn(xkd[u], bg[g]) for u, (s, g) in enumerate(units)]
    for u, (s, g) in enumerate(units):
        h_scr[s, g] = hn[u]
    for g in gs:
        yv = _by_sequence([yg[s * ng + g] for s in range(len(segs))], segs, c, gw)
        y_scr[:, g * gw:(g + 1) * gw] = yv * _silu(z_ref[:, g * gw:(g + 1) * gw])

    y = y_scr[...]
    y = y * lax.rsqrt(jnp.mean(y * y, axis=-1, keepdims=True) + EPS) * ng_ref[...]
    y_ref[...] = y.astype(y_ref.dtype)

    @pl.when(ci == pl.num_programs(1) - 1)
    def _():
        ht_ref[...] = h_scr[...]


def ssd(seg_x, seg_z, seg_d, row0, nb, nc, c, segs, conv0, conv_w, conv_b, h0, pa, ngain, ng, nr, hp, ns, off):
    inner = ng * nr * hp
    sconv = inner + 2 * ng * ns
    r0 = row0 // c
    nseq = len(segs)
    glob = lambda s, i: (r0 + s * nc + i, 0)
    st = pl.BlockSpec((nseq, ng, nr * hp, ns), lambda s, i: (s, 0, 0, 0))
    scratch = [pltpu.VMEM((nseq, ng, nr * hp, ns), F32), pltpu.VMEM((c + SUBLANES, sconv), F32),
               pltpu.VMEM((c, sconv), F32), pltpu.VMEM((c, inner), F32)]
    if nseq > 1:
        scratch.append(pltpu.VMEM((c + SUBLANES, sconv), F32))
    return pl.pallas_call(
        functools.partial(_ssd_body, c=c, segs=segs, ng=ng, nr=nr, hp=hp, ns=ns, off=off), grid=(nb, nc),
        in_specs=[pl.BlockSpec((c, sconv), glob),
                  pl.BlockSpec((c, inner), glob),
                  pl.BlockSpec((c, LANES), glob),
                  pl.BlockSpec((nseq, CONV_W - 1, sconv), lambda s, i: (s, 0, 0)),
                  pl.BlockSpec((CONV_W, sconv), lambda s, i: (0, 0)),
                  pl.BlockSpec((1, sconv), lambda s, i: (0, 0)),
                  st,
                  pl.BlockSpec((SUBLANES, LANES), lambda s, i: (0, 0)),
                  pl.BlockSpec((1, inner), lambda s, i: (0, 0))],
        out_specs=[pl.BlockSpec((c, inner), lambda s, i: (s * nc + i, 0)), st],
        out_shape=[jax.ShapeDtypeStruct((nb * nc * c, inner), BF16),
                   jax.ShapeDtypeStruct((nb * nseq, ng, nr * hp, ns), F32)],
        scratch_shapes=scratch,
        compiler_params=_params(("parallel", "arbitrary"), 40), name="ssd",
    )(seg_x, seg_z, seg_d, conv0, conv_w, conv_b.reshape(1, sconv), h0, pa, ngain.reshape(1, inner))


def _ca_body(q_ref, k_ref, v_ref, o_ref, *, nh, dh, nseq, ls):
    scale = dh ** -0.5
    rows = q_ref.shape[0]
    cols = [slice(h * dh, (h + 1) * dh) for h in range(nh)]
    if len(k_ref.shape) == 4:
        mem = k_ref.shape[1]
        qs = jnp.concatenate([q_ref[:, cols[h]] for h in range(nh)], axis=0)
        col_head = lax.rem(lax.broadcasted_iota(jnp.int32, (nh * rows, mem * nh), 1), nh)
        row_head = lax.div(lax.broadcasted_iota(jnp.int32, (nh * rows, mem * nh), 0), rows)
        own = col_head == row_head
        o = []
        for b in range(nseq):
            s = jnp.where(own, _bdot_nt(qs, k_ref[b].reshape(mem * nh, dh)) * scale, NEG)
            e = jnp.exp(s - jnp.max(s, axis=-1, keepdims=True))
            p = e / jnp.sum(e, axis=-1, keepdims=True)
            ob = _bdot(p, v_ref[b].reshape(mem * nh, dh))
            o += [ob[h * rows:(h + 1) * rows] for h in range(nh)]
    else:
        units = [(b, h) for b in range(nseq) for h in range(nh)]
        s = [_bdot_nt(q_ref[:, cols[h]], k_ref[b, :, cols[h]]) * scale for b, h in units]
        e = [jnp.exp(x - jnp.max(x, axis=-1, keepdims=True)) for x in s]
        p = [x / jnp.sum(x, axis=-1, keepdims=True) for x in e]
        o = [_bdot(p[i], v_ref[b, :, cols[h]]) for i, (b, h) in enumerate(units)]
    row = lax.broadcasted_iota(jnp.int32, (rows, dh), 0)
    for h in range(nh):
        val = o[(nseq - 1) * nh + h]
        for b in range(nseq - 2, -1, -1):
            val = jnp.where(row < (b + 1) * ls, o[b * nh + h], val)
        o_ref[:, cols[h]] = val.astype(o_ref.dtype)


def cross_attention(q2d, row0, nrows, tl, nseq, mem_k, mem_v, nh, dh):
    r0 = row0 // tl
    wd = nh * dh
    per = nrows // tl * nseq // mem_k.shape[0]
    if mem_k.ndim == 4:
        kv = pl.BlockSpec((nseq,) + mem_k.shape[1:], lambda i: (i // per, 0, 0, 0))
    else:
        kv = pl.BlockSpec((nseq,) + mem_k.shape[1:], lambda i: (i // per, 0, 0))
    return pl.pallas_call(
        functools.partial(_ca_body, nh=nh, dh=dh, nseq=nseq, ls=tl // nseq), grid=(nrows // tl,),
        in_specs=[pl.BlockSpec((tl, wd), lambda i: (r0 + i, 0)), kv, kv],
        out_specs=pl.BlockSpec((tl, wd), lambda i: (i, 0)),
        out_shape=jax.ShapeDtypeStruct((nrows, wd), BF16),
        compiler_params=_params(("parallel",), 32), name="cross_attention",
    )(q2d, mem_k, mem_v)


def _extract_top(x, n):
    out = []
    for it in range(n):
        m = jnp.max(x, axis=0, keepdims=True)
        out.append(m)
        if it + 1 < n:
            x = jnp.where(x == m, -jnp.inf, x)
    return out


def _pair_candidates(v1, v2, topk):
    v1s = jnp.concatenate(v1, axis=0)
    v2s = jnp.concatenate(v2, axis=0)
    pieces = []
    a = 0
    while a < topk:
        nb = topk // (a + 1)
        if nb == 1 and a % SUBLANES == 0:
            pieces.append(v1s[a:] + v2[0])
            break
        pieces.append(v1[a] + v2s[:min(-(-nb // SUBLANES) * SUBLANES, topk)])
        a += 1
    return jnp.concatenate(pieces, axis=0)


def _peer_route_body(q_ref, keys_ref, s1_ref, e1_ref, s2_ref, e2_ref, thr_ref, *, nh, dq, topk):
    for h in range(nh):
        s1 = _bdot_nt(keys_ref[2 * h], q_ref[:, (2 * h) * dq:(2 * h + 1) * dq])
        s2 = _bdot_nt(keys_ref[2 * h + 1], q_ref[:, (2 * h + 1) * dq:(2 * h + 2) * dq])
        v1 = _extract_top(s1, topk)
        v2 = _extract_top(s2, topk)
        tops = _extract_top(_pair_candidates(v1, v2, topk), topk)
        zsum = jnp.exp(tops[0] - tops[0])
        for a in range(1, topk):
            zsum = zsum + jnp.exp(tops[a] - tops[0])
        s1_ref[h] = s1
        s2_ref[h] = s2
        e1_ref[h] = jnp.exp(s1 - v1[0]) / zsum
        e2_ref[h] = jnp.exp(s2 - v2[0])
        thr_ref[h:h + 1, :] = tops[topk - 1]


def peer_route(q, keys, nh, nk, dq, tm=256):
    m = q.shape[0]
    tm = _tile(m, tm)
    big = pl.BlockSpec((nh, nk, tm), lambda i: (0, 0, i))
    shape = jax.ShapeDtypeStruct((nh, nk, m), F32)
    return pl.pallas_call(
        functools.partial(_peer_route_body, nh=nh, dq=dq, topk=PEER_TOPK), grid=(m // tm,),
        in_specs=[pl.BlockSpec((tm, 2 * nh * dq), lambda i: (i, 0)),
                  pl.BlockSpec((2 * nh, nk, dq), lambda i: (0, 0, 0))],
        out_specs=[big, big, big, big, pl.BlockSpec((nh, tm), lambda i: (0, i))],
        out_shape=[shape, shape, shape, shape, jax.ShapeDtypeStruct((nh, m), F32)],
        compiler_params=_params(("parallel",), 32), name="peer_route",
    )(q, keys)


def _peer_body(hft_ref, u_ref, vt_ref, s1_ref, e1_ref, s2_ref, e2_ref, thr_ref, ot_ref, st0, st1, act0, act1,
               *, nh, nk, nrow, dchunk):
    ei = pl.program_id(1)
    d, tm = hft_ref.shape

    @pl.when(ei == 0)
    def _():
        ot_ref[...] = jnp.zeros_like(ot_ref)
        st1[...] = jnp.zeros_like(st1)
        act0[...] = jnp.zeros_like(act0)

    pieces = nrow
    prow = nrow * nk // pieces

    def step(a):
        st_new, st_old = (st0, st1) if a == 0 else (st1, st0)
        act_old, act_new = (act0, act1) if a == 0 else (act1, act0)

        def scores(p):
            rows = slice(p * prow, (p + 1) * prow)
            st_new[rows, :] = jnp.dot(u_ref[rows, :], hft_ref[...], preferred_element_type=F32)

        def apply_values(j):
            ds = slice(j * dchunk, (j + 1) * dchunk)
            ot_ref[ds, :] += jnp.dot(vt_ref[ds, :], act_old[...], preferred_element_type=F32)

        def gate_block(r, tb):
            ls = slice(tb * LANES, (tb + 1) * LANES)
            rows = slice(r * nk, (r + 1) * nk)
            gate = jnp.zeros((nk, LANES), F32)
            for h in range(nh):
                hit = (s2_ref[h, :, ls] + s1_ref[r, h:h + 1, ls]) >= thr_ref[h:h + 1, ls]
                gate = gate + jnp.where(hit, e2_ref[h, :, ls] * e1_ref[r, h:h + 1, ls], 0.0)
            act_new[rows, ls] = (_gelu_tanh(st_old[rows, ls]) * gate).astype(BF16)

        blocks = [(r, tb) for r in range(nrow) for tb in range(tm // LANES)]
        nd = d // dchunk
        per = -(-len(blocks) // nd)
        every = max(nd // pieces, 1)
        for j in range(max(nd, pieces * every)):
            if j % every == 0 and j // every < pieces:
                scores(j // every)
            if j < nd:
                apply_values(j)
            for r, tb in blocks[j * per:(j + 1) * per]:
                gate_block(r, tb)

    @pl.when(ei % 2 == 0)
    def _():
        step(0)

    @pl.when(ei % 2 == 1)
    def _():
        step(1)


def peer_experts(hft, u, vt, s1g, e1g, s2, e2, thr, nh, nk, tm=512, nrow=4):
    d, m = hft.shape
    tm = _tile(m, tm)
    et = nrow * nk
    once = pl.Buffered(1)
    ne = nk // nrow
    tile = lambda e, lag: jnp.clip(e - lag, 0, ne - 1)
    return pl.pallas_call(
        functools.partial(_peer_body, nh=nh, nk=nk, nrow=nrow, dchunk=_tile(d, 512)), grid=(m // tm, ne + 2),
        in_specs=[pl.BlockSpec((d, tm), lambda t, e: (0, t), pipeline_mode=once),
                  pl.BlockSpec((et, d), lambda t, e: (tile(e, 0), 0)),
                  pl.BlockSpec((d, et), lambda t, e: (0, tile(e, 2))),
                  pl.BlockSpec((nrow, nh, tm), lambda t, e: (tile(e, 1), 0, t)),
                  pl.BlockSpec((nrow, nh, tm), lambda t, e: (tile(e, 1), 0, t)),
                  pl.BlockSpec((nh, nk, tm), lambda t, e: (0, 0, t), pipeline_mode=once),
                  pl.BlockSpec((nh, nk, tm), lambda t, e: (0, 0, t), pipeline_mode=once),
                  pl.BlockSpec((nh, tm), lambda t, e: (0, t))],
        out_specs=pl.BlockSpec((d, tm), lambda t, e: (0, t)),
        out_shape=jax.ShapeDtypeStruct((d, m), F32),
        scratch_shapes=[pltpu.VMEM((et, tm), F32), pltpu.VMEM((et, tm), F32),
                        pltpu.VMEM((et, tm), BF16), pltpu.VMEM((et, tm), BF16)],
        compiler_params=_params(("parallel", "arbitrary"), 56), name="peer_experts",
    )(hft, u, vt, s1g, e1g, s2, e2, thr)


def _lane_row(nrows, pieces):
    out = jnp.zeros((SUBLANES, LANES), F32)
    for row, off, vec in pieces:
        out = out.at[row, off:off + vec.shape[0]].set(vec.astype(F32))
    return out


def kernel(x_prompt, x_sample, cache_mem_k, cache_mem_v, state_dn_conv, state_dn_rec, state_ssm_conv,
           state_ssm_rec, mem_prompt, norm_mix_g, w_in, dn_conv_w, dn_a_log, dn_dt_bias, dn_norm_g, dn_w_out,
           ssm_conv_w, ssm_conv_b, ssm_a_log, ssm_dt_bias, ssm_d, ssm_norm_g, ssm_w_out, w_o, norm_ca_g,
           ca_w_q, ca_w_k, ca_w_v, ca_w_o, norm_ffn_g, peer_w_q, peer_keys, peer_u, peer_v, final_norm_g):
    depth = w_in.shape[0]
    bp, lp, d = x_prompt.shape
    bs, ls, _ = x_sample.shape
    _, _, dnh, dk, _ = state_dn_rec.shape
    _, _, sh, hp, ns = state_ssm_rec.shape
    dnw = dnh * dk
    inner = sh * hp
    sconv = state_ssm_conv.shape[-1]
    sg = (sconv - inner) // (2 * ns)
    sr = sh // sg
    _, _, mem, cah, cad = cache_mem_k.shape
    caw = cah * cad
    _, pnh, _, nk, dq = peer_keys.shape
    packed = SUBLANES % ls == 0 and ls >= CONV_W - 1 and bs % (SUBLANES // ls) == 0
    lrow = ls if packed else -(-ls // SUBLANES) * SUBLANES
    cs = SUBLANES if packed else lrow
    nseq = cs // lrow
    s_segs = tuple((j * lrow, j * lrow + ls) for j in range(nseq))
    assert lp % CHUNK == 0 and cs <= CHUNK and dk == LANES and ns == LANES and nk == LANES and dq == LANES
    assert 2 * dnh + sh <= LANES and hp & (hp - 1) == 0
    mp, ms = bp * lp, bs * lrow
    ssm_off = 2 * dnh

    x = [x_prompt.reshape(mp, d), jnp.pad(x_sample, ((0, 0), (0, lrow - ls), (0, 0))).reshape(ms, d)]

    outs = [[] for _ in range(10)]
    for l in range(depth):
        wt = w_in[l].T
        o1 = 3 * dnw + dnw
        o2 = o1 + 2 * dnh
        o3 = o2 + inner
        o4 = o3 + sconv
        o5 = o4 + sh
        w_d = jnp.concatenate([wt[o1:o2], wt[o4:o5], jnp.zeros((LANES - 2 * dnh - sh, d), F32)], axis=0)

        hn = rmsnorm(x, norm_mix_g[l], BF16)
        seg_a = matmul(hn, wt[:o1].astype(BF16), w_rows_out=True)
        seg_z = matmul(hn, wt[o2:o3].astype(BF16), w_rows_out=True)
        seg_x = matmul(hn, wt[o3:o4].astype(BF16), w_rows_out=True)
        seg_c = matmul(hn, wt[o5:].astype(BF16), w_rows_out=True)
        seg_d = matmul(hn, w_d.astype(BF16), w_rows_out=True)

        dn_pa = _lane_row(SUBLANES, [(0, dnh, dn_a_log[l]), (1, dnh, dn_dt_bias[l])])
        ssm_pa = _lane_row(SUBLANES, [(0, ssm_off, ssm_a_log[l]), (1, ssm_off, ssm_dt_bias[l]), (2, ssm_off, ssm_d[l])])
        groups = (
            (0, bp, lp // CHUNK, CHUNK, ((0, CHUNK),), jnp.zeros((bp, CONV_W - 1, 3 * dnw), F32),
             jnp.zeros((bp, dnh, dk, dk), F32), jnp.zeros((bp, CONV_W - 1, sconv), F32),
             jnp.zeros((bp, sg, sr * hp, ns), F32)),
            (mp, bs // nseq, 1, cs, s_segs, state_dn_conv[l], state_dn_rec[l],
             state_ssm_conv[l], state_ssm_rec[l].reshape(bs, sg, sr * hp, ns)),
        )
        o_dn, o_ssm, dn_h, ssm_h = [], [], [], []
        for row0, nb, nc, c, segs, dc0, dh0, sc0, sh0 in groups:
            od, hd = deltanet(seg_a, seg_d, row0, nb, nc, c, segs, dc0, dn_conv_w[l], dh0, dn_pa, dn_norm_g[l],
                              dnh, dk)
            osd, hsd = ssd(seg_x, seg_z, seg_d, row0, nb, nc, c, segs, sc0, ssm_conv_w[l], ssm_conv_b[l], sh0, ssm_pa,
                           ssm_norm_g[l], sg, sr, hp, ns, ssm_off)
            o_dn.append(od)
            o_ssm.append(osd)
            dn_h.append(hd)
            ssm_h.append(hsd.reshape(-1, sh, hp, ns))

        mixed = gated_mix(o_dn, dn_w_out[l].astype(BF16), o_ssm, ssm_w_out[l].astype(BF16), seg_c)
        x1 = matmul(mixed, w_o[l].astype(BF16), res=x)

        hc = rmsnorm(x1, norm_ca_g[l], BF16)
        qc = matmul(hc, ca_w_q[l].astype(BF16))
        memp = mem_prompt.reshape(bp * mem, d)
        mk = matmul(memp, ca_w_k[l].astype(BF16))
        mv = matmul(memp, ca_w_v[l].astype(BF16))
        oc = [cross_attention(qc, 0, mp, _tile(lp, 512), 1, mk.reshape(bp, mem, caw), mv.reshape(bp, mem, caw),
                              cah, cad),
              cross_attention(qc, mp, ms, cs, nseq, cache_mem_k[l], cache_mem_v[l], cah, cad)]
        x2 = matmul(oc, ca_w_o[l].astype(BF16), res=x1)

        hf, hft = rmsnorm(x2, norm_ffn_g[l], BF16, with_transpose=True)
        pq = matmul(hf, peer_w_q[l].astype(BF16))
        s1, e1, s2, e2, thr = peer_route(pq, peer_keys[l].reshape(2 * pnh, nk, dq).astype(BF16), pnh, nk, dq)
        pot = peer_experts(hft, peer_u[l].astype(BF16), transpose_cast(peer_v[l], BF16),
                           s1.transpose(1, 0, 2), e1.transpose(1, 0, 2), s2, e2, thr, pnh, nk)
        if l == depth - 1:
            y_prompt = add_rows(x2, pot, final_norm_g, 0, mp).reshape(bp, lp, d)
            y_sample = add_rows(x2, pot, final_norm_g, mp, ms).reshape(bs, lrow, d)[:, :ls]
        else:
            x = add_rows(x2, pot, None, 0, mp + ms)

        def tail(seg, row0, nb, sl, nvalid, lo, hi, state0):
            take = min(CONV_W - 1, nvalid)
            rows = row0 + np.arange(nb)[:, None] * sl + np.arange(nvalid - take, nvalid)[None, :]
            u = jnp.take(seg, rows.reshape(-1), axis=0)[:, lo:hi].reshape(nb, take, hi - lo)
            if take < CONV_W - 1:
                u = jnp.concatenate([state0[:, take:], u], axis=1)
            return u

        outs[0].append(mk.reshape(bp, mem, cah, cad))
        outs[1].append(mv.reshape(bp, mem, cah, cad))
        outs[2].append(tail(seg_a, 0, bp, lp, lp, 0, 3 * dnw, groups[0][5]))
        outs[3].append(dn_h[0])
        outs[4].append(tail(seg_x, 0, bp, lp, lp, 0, sconv, groups[0][7]))
        outs[5].append(ssm_h[0])
        outs[6].append(tail(seg_a, mp, bs, lrow, ls, 0, 3 * dnw, state_dn_conv[l]))
        outs[7].append(dn_h[1])
        outs[8].append(tail(seg_x, mp, bs, lrow, ls, 0, sconv, state_ssm_conv[l]))
        outs[9].append(ssm_h[1])

    return (y_prompt, y_sample) + tuple(jnp.stack(o) for o in outs)
```

```python
import functools
import math

import jax
import jax.numpy as jnp
import numpy as np
from jax import lax
from jax.experimental import pallas as pl
from jax.experimental.pallas import tpu as pltpu

F32 = jnp.float32
BF16 = jnp.bfloat16

EPS = 1e-6
CONV_W = 4
CHUNK = 64
PEER_TOPK = 16
LANES = 128
SUBLANES = 8
NEG = -1e30
MIB = 1 << 20


def _params(sem, vmem_mib):
    return pltpu.CompilerParams(dimension_semantics=sem, vmem_limit_bytes=vmem_mib * MIB)


def _tile(n, pref):
    t = min(pref, n)
    while n % t:
        t //= 2
    return t


def _sigmoid(x):
    return 1.0 / (1.0 + jnp.exp(-x))


def _silu(x):
    return x * _sigmoid(x)


def _softplus(x):
    return jnp.maximum(x, 0.0) + jnp.log1p(jnp.exp(-jnp.abs(x)))


def _gelu_tanh(x):
    return x * (0.5 * (1.0 + jnp.tanh(0.7978845608028654 * (x + 0.044715 * (x * x * x)))))


def _bdot(a, b):
    return jnp.dot(a.astype(BF16), b.astype(BF16), preferred_element_type=F32)


def _bdot_nt(a, b):
    return lax.dot_general(a.astype(BF16), b.astype(BF16), (((1,), (1,)), ((), ())),
                           preferred_element_type=F32)


def _bdot_tn(a, b):
    return lax.dot_general(a.astype(BF16), b.astype(BF16), (((0,), (0,)), ((), ())),
                           preferred_element_type=F32)


def _as_pieces(a):
    return list(a) if isinstance(a, (list, tuple)) else [a]


def _piece_tile(pieces, pref):
    return _tile(math.gcd(*[p.shape[0] for p in pieces]), pref)


def _piece_ends(pieces, tm):
    ends, tot = [], 0
    for p in pieces:
        tot += p.shape[0] // tm
        ends.append(tot)
    return tuple(ends)


def _piece_specs(pieces, tm, cols, row_of, col_of):
    specs, start = [], 0
    for p in pieces:
        n = p.shape[0] // tm
        specs.append(pl.BlockSpec(
            (tm, cols), lambda *g, s=start, n=n: (jnp.clip(row_of(*g) - s, 0, n - 1), col_of(*g))))
        start += n
    return specs


def _piece_value(refs, tile, ends):
    val = refs[-1][...]
    for k in range(len(refs) - 2, -1, -1):
        val = jnp.where(tile < ends[k], refs[k][...], val)
    return val


def _rmsnorm_body(*refs, nx, ends, both):
    x = _piece_value(refs[:nx], pl.program_id(0), ends)
    y = x * lax.rsqrt(jnp.mean(x * x, axis=-1, keepdims=True) + EPS) * refs[nx][...]
    refs[nx + 1][...] = y.astype(refs[nx + 1].dtype)
    if both:
        refs[nx + 2][...] = y.T.astype(refs[nx + 2].dtype)


def _addnorm_body(x_ref, pt_ref, g_ref, o_ref):
    x = x_ref[...] + pt_ref[...].T
    y = x * lax.rsqrt(jnp.mean(x * x, axis=-1, keepdims=True) + EPS)
    o_ref[...] = (y * g_ref[...]).astype(o_ref.dtype)


def _add_body(x_ref, pt_ref, o_ref):
    o_ref[...] = x_ref[...] + pt_ref[...].T


def rmsnorm(x, g, out_dtype, tm=256, with_transpose=False):
    xs = _as_pieces(x)
    d = xs[0].shape[1]
    m = sum(p.shape[0] for p in xs)
    tm = _piece_tile(xs, tm)
    row = pl.BlockSpec((tm, d), lambda i: (i, 0))
    outs, shapes = [row], [jax.ShapeDtypeStruct((m, d), out_dtype)]
    if with_transpose:
        outs.append(pl.BlockSpec((d, tm), lambda i: (0, i)))
        shapes.append(jax.ShapeDtypeStruct((d, m), out_dtype))
    res = pl.pallas_call(
        functools.partial(_rmsnorm_body, nx=len(xs), ends=_piece_ends(xs, tm), both=with_transpose),
        grid=(m // tm,),
        in_specs=_piece_specs(xs, tm, d, lambda i: i, lambda i: 0) + [pl.BlockSpec((1, d), lambda i: (0, 0))],
        out_specs=outs, out_shape=shapes,
        compiler_params=_params(("parallel",), 40), name="rmsnorm",
    )(*xs, g.reshape(1, d))
    return res if with_transpose else res[0]


def add_rows(x, pt, g, row0, nrows, tm=256):
    d = x.shape[1]
    tm = _tile(math.gcd(row0, nrows) if row0 else nrows, tm)
    r0 = row0 // tm
    ins = [pl.BlockSpec((tm, d), lambda i: (r0 + i, 0)), pl.BlockSpec((d, tm), lambda i: (0, r0 + i))]
    if g is None:
        body, args = _add_body, (x, pt)
    else:
        body, args = _addnorm_body, (x, pt, g.reshape(1, d))
        ins.append(pl.BlockSpec((1, d), lambda i: (0, 0)))
    return pl.pallas_call(
        body, grid=(nrows // tm,), in_specs=ins, out_specs=pl.BlockSpec((tm, d), lambda i: (i, 0)),
        out_shape=jax.ShapeDtypeStruct((nrows, d), F32),
        compiler_params=_params(("parallel",), 40), name="add_norm",
    )(*args)


def _transpose_cast_body(x_ref, o_ref):
    o_ref[...] = x_ref[...].T.astype(o_ref.dtype)


def transpose_cast(x, dtype, tr=512, tc=1024):
    r, c = x.shape
    tr, tc = _tile(r, tr), _tile(c, tc)
    return pl.pallas_call(
        _transpose_cast_body, grid=(r // tr, c // tc),
        in_specs=[pl.BlockSpec((tr, tc), lambda i, j: (i, j))],
        out_specs=pl.BlockSpec((tc, tr), lambda i, j: (j, i)),
        out_shape=jax.ShapeDtypeStruct((c, r), dtype),
        compiler_params=_params(("parallel", "parallel"), 32), name="transpose_cast",
    )(x)


def _mm_body(*refs, na, nr, a_ends, r_ends, w_rows_out):
    w_ref, o_ref = refs[na], refs[-1]
    i = pl.program_id(1)
    a = _piece_value(refs[:na], i, a_ends).astype(BF16)
    acc = _bdot_nt(a, w_ref[...]) if w_rows_out else jnp.dot(a, w_ref[...], preferred_element_type=F32)
    if nr:
        acc = _piece_value(refs[na + 1:na + 1 + nr], i, r_ends) + acc
    o_ref[...] = acc.astype(o_ref.dtype)


def matmul(a, w, res=None, out_dtype=F32, tm=512, tn=1024, w_rows_out=False):
    a_p = _as_pieces(a)
    r_p = _as_pieces(res) if res is not None else []
    k = a_p[0].shape[1]
    m = sum(p.shape[0] for p in a_p)
    n = w.shape[0] if w_rows_out else w.shape[1]
    tm, tn = _piece_tile(a_p + r_p, tm), _tile(n, tn)
    row_of, zero = (lambda j, i: i), (lambda j, i: 0)
    w_spec = pl.BlockSpec((tn, k), lambda j, i: (j, 0)) if w_rows_out else pl.BlockSpec((k, tn), lambda j, i: (0, j))
    ins = _piece_specs(a_p, tm, k, row_of, zero) + [w_spec] + _piece_specs(r_p, tm, tn, row_of, lambda j, i: j)
    return pl.pallas_call(
        functools.partial(_mm_body, na=len(a_p), nr=len(r_p), a_ends=_piece_ends(a_p, tm),
                          r_ends=_piece_ends(r_p, tm), w_rows_out=w_rows_out),
        grid=(n // tn, m // tm),
        in_specs=ins, out_specs=pl.BlockSpec((tm, tn), lambda j, i: (i, j)),
        out_shape=jax.ShapeDtypeStruct((m, n), out_dtype),
        compiler_params=_params(("parallel", "parallel"), 48), name="matmul",
    )(*a_p, w, *r_p)


def _mix_body(*refs, n1, n2, ends1, ends2):
    w1_ref, w2_ref = refs[n1], refs[n1 + 1 + n2]
    g1_ref, g2_ref, o_ref = refs[-3], refs[-2], refs[-1]
    i = pl.program_id(1)
    y1 = jnp.dot(_piece_value(refs[:n1], i, ends1), w1_ref[...], preferred_element_type=F32)
    y2 = jnp.dot(_piece_value(refs[n1 + 1:n1 + 1 + n2], i, ends2), w2_ref[...], preferred_element_type=F32)
    o_ref[...] = (_sigmoid(g1_ref[...]) * y1 + _sigmoid(g2_ref[...]) * y2).astype(o_ref.dtype)


def gated_mix(a1, w1, a2, w2, gates, tm=512, tn=1024):
    p1, p2 = _as_pieces(a1), _as_pieces(a2)
    k1, k2 = p1[0].shape[1], p2[0].shape[1]
    m, n = gates.shape[0], w1.shape[1]
    tm, tn = _piece_tile(p1 + p2, tm), _tile(n, tn)
    nj = n // tn
    row_of, zero = (lambda j, i: i), (lambda j, i: 0)
    return pl.pallas_call(
        functools.partial(_mix_body, n1=len(p1), n2=len(p2), ends1=_piece_ends(p1, tm), ends2=_piece_ends(p2, tm)),
        grid=(nj, m // tm),
        in_specs=(_piece_specs(p1, tm, k1, row_of, zero) + [pl.BlockSpec((k1, tn), lambda j, i: (0, j))]
                  + _piece_specs(p2, tm, k2, row_of, zero) + [pl.BlockSpec((k2, tn), lambda j, i: (0, j))]
                  + [pl.BlockSpec((tm, tn), lambda j, i: (i, j)), pl.BlockSpec((tm, tn), lambda j, i: (i, j + nj))]),
        out_specs=pl.BlockSpec((tm, tn), lambda j, i: (i, j)),
        out_shape=jax.ShapeDtypeStruct((m, n), BF16),
        compiler_params=_params(("parallel", "parallel"), 48), name="gated_mix",
    )(*p1, w1, *p2, w2, gates, gates)


def _conv_silu_rows(x_ref, st_ref, w_ref, b_ref, buf, cv, c, ci):
    k = CONV_W - 1

    @pl.when(ci == 0)
    def _():
        buf[SUBLANES - k:SUBLANES, :] = st_ref[0]

    buf[SUBLANES:SUBLANES + c, :] = x_ref[...]
    base = SUBLANES - k
    acc = buf[base:base + c, :] * w_ref[0:1, :]
    for j in range(1, CONV_W):
        acc = acc + buf[base + j:base + j + c, :] * w_ref[j:j + 1, :]
    if b_ref is not None:
        acc = acc + b_ref[...]
    cv[...] = _silu(acc)
    buf[0:SUBLANES, :] = buf[c:c + SUBLANES, :]


def _conv_silu_packed(x_ref, st_ref, w_ref, b_ref, buf, ov, cv, c, ls):
    k = CONV_W - 1
    ov[...] = jnp.zeros_like(ov)
    for j in range(c // ls):
        ov[SUBLANES + j * ls - k:SUBLANES + j * ls, :] = st_ref[j]
    buf[0:SUBLANES, :] = jnp.zeros((SUBLANES, buf.shape[1]), F32)
    buf[SUBLANES:SUBLANES + c, :] = x_ref[...]
    base = SUBLANES - k
    in_seq = lax.broadcasted_iota(jnp.int32, (c, buf.shape[1]), 0) & (ls - 1)
    acc = None
    for j in range(CONV_W):
        win = buf[base + j:base + j + c, :]
        if j < k:
            win = jnp.where(in_seq + j < k, ov[base + j:base + j + c, :], win)
        term = win * w_ref[j:j + 1, :]
        acc = term if acc is None else acc + term
    if b_ref is not None:
        acc = acc + b_ref[...]
    cv[...] = _silu(acc)


def _chunk_cumsum(g, c):
    tt = lax.broadcasted_iota(jnp.int32, (c, c), 0)
    ss = lax.broadcasted_iota(jnp.int32, (c, c), 1)
    tril = jnp.where(tt >= ss, 1.0, 0.0).astype(BF16)
    hi = g.astype(BF16)
    r1 = g - hi.astype(F32)
    mid = r1.astype(BF16)
    lo = (r1 - mid.astype(F32)).astype(BF16)
    dot = functools.partial(jnp.dot, preferred_element_type=F32)
    return dot(tril, hi) + dot(tril, mid) + dot(tril, lo)


def _rows_as_lanes(x, c):
    if c < LANES:
        x = jnp.concatenate([x, jnp.zeros((LANES - c, LANES), F32)], axis=0)
    return x.T


def _live_rows(c, seg):
    lo, hi = seg
    if (lo, hi) == (0, c):
        return None
    row = lax.broadcasted_iota(jnp.int32, (c, LANES), 0)
    return jnp.logical_and(row >= lo, row < hi)


def _by_sequence(vals, segs, c, width):
    out = vals[-1]
    if len(vals) > 1:
        row = lax.broadcasted_iota(jnp.int32, (c, width), 0)
        for s in range(len(vals) - 2, -1, -1):
            out = jnp.where(row < segs[s][1], vals[s], out)
    return out


def _dn_body(x_ref, z_ref, sm_ref, cst_ref, cw_ref, h0_ref, pa_ref, ng_ref, o_ref, ht_ref, h_scr, buf, cv, *ov,
             c, segs, nh, dk, nsq):
    ci = pl.program_id(1)

    @pl.when(ci == 0)
    def _():
        h_scr[...] = h0_ref[...]

    if len(segs) > 1:
        _conv_silu_packed(x_ref, cst_ref, cw_ref, None, buf, ov[0], cv, c, segs[0][1] - segs[0][0])
    else:
        _conv_silu_rows(x_ref, cst_ref, cw_ref, None, buf, cv, c, ci)
    w = nh * dk
    sm = sm_ref[...]
    pa = pa_ref[...]
    beta_raw = _sigmoid(sm)
    g_raw = -jnp.exp(pa[0:1]) * _softplus(sm + pa[1:2])
    beta_all, gc_all, gct, eg_all = [], [], [], []
    for seg in segs:
        live = _live_rows(c, seg)
        beta_all.append(beta_raw if live is None else jnp.where(live, beta_raw, 0.0))
        gc_all.append(_chunk_cumsum(g_raw if live is None else jnp.where(live, g_raw, 0.0), c))
        gct.append(_rows_as_lanes(gc_all[-1], c))
        eg_all.append(jnp.exp(gc_all[-1]))
    tt = lax.broadcasted_iota(jnp.int32, (c, c), 0)
    ss = lax.broadcasted_iota(jnp.int32, (c, c), 1)
    strict = tt > ss
    incl = tt >= ss
    scale = dk ** -0.5
    hs = range(nh)
    units = [(s, h) for s in range(len(segs)) for h in hs]
    us = range(len(units))

    q, k = [], []
    for h in hs:
        qh = cv[:, h * dk:(h + 1) * dk]
        kh = cv[:, w + h * dk:w + (h + 1) * dk]
        q.append(qh * lax.rsqrt(jnp.sum(qh * qh, axis=-1, keepdims=True) + EPS) * scale)
        k.append(kh * lax.rsqrt(jnp.sum(kh * kh, axis=-1, keepdims=True) + EPS))
    kk = [_bdot_nt(k[h], k[h]) for h in hs]
    qk = [_bdot_nt(q[h], k[h]) for h in hs]
    hh = [h_scr[s, h] for s, h in units]
    qh0 = [_bdot(q[h], hh[u]) for u, (s, h) in enumerate(units)]
    gcol = [gc_all[s][:, nh + h:nh + h + 1] for s, h in units]
    diff = [gcol[u] - gct[s][nh + h:nh + h + 1, 0:c] for u, (s, h) in enumerate(units)]
    bcol = [beta_all[s][:, h:h + 1] for s, h in units]
    egc = [eg_all[s][:, nh + h:nh + h + 1] for s, h in units]
    glast = [gc_all[s][c - 1:c, nh + h:nh + h + 1] for s, h in units]
    qm = [-(bcol[u] * kk[h] * jnp.exp(jnp.where(strict, diff[u], NEG))) for u, (s, h) in enumerate(units)]
    pw = [_bdot(qm[u], qm[u]) for u in us] if nsq else None
    for it in range(nsq):
        t = [_bdot(qm[u], pw[u]) for u in us]
        nxt = [_bdot(pw[u], pw[u]) for u in us] if it + 1 < nsq else None
        qm = [qm[u] + pw[u] + t[u] for u in us]
        pw = nxt
    rhs = [jnp.concatenate([cv[:, 2 * w + h * dk:2 * w + (h + 1) * dk] * bcol[u], k[h] * (bcol[u] * egc[u])], axis=1)
           for u, (s, h) in enumerate(units)]
    sol = [rhs[u] + _bdot(qm[u], rhs[u]) for u in us]
    wv = [sol[u][:, :dk] - _bdot(sol[u][:, dk:], hh[u]) for u in us]
    o = [qh0[u] * egc[u] + _bdot(qk[h] * jnp.exp(jnp.where(incl, diff[u], NEG)), wv[u])
         for u, (s, h) in enumerate(units)]
    hn = [jnp.exp(glast[u]) * hh[u] + _bdot_tn(k[h] * jnp.exp(glast[u] - gcol[u]), wv[u])
          for u, (s, h) in enumerate(units)]
    for u, (s, h) in enumerate(units):
        h_scr[s, h] = hn[u]
    for h in hs:
        oh = _by_sequence([o[s * nh + h] for s in range(len(segs))], segs, c, dk)
        oh = oh * lax.rsqrt(jnp.mean(oh * oh, axis=-1, keepdims=True) + EPS) * ng_ref[...]
        o_ref[:, h * dk:(h + 1) * dk] = (oh * _silu(z_ref[:, h * dk:(h + 1) * dk])).astype(o_ref.dtype)

    @pl.when(ci == pl.num_programs(1) - 1)
    def _():
        ht_ref[...] = h_scr[...]


def _num_squarings(segs):
    n, longest = 0, max(hi - lo for lo, hi in segs)
    while (2 << n) < longest:
        n += 1
    return n


def deltanet(seg_a, seg_d, row0, nb, nc, c, segs, conv0, conv_w, h0, pa, ng, nh, dk):
    w = nh * dk
    r0 = row0 // c
    ns = len(segs)
    st = pl.BlockSpec((ns, nh, dk, dk), lambda s, i: (s, 0, 0, 0))
    scratch = [pltpu.VMEM((ns, nh, dk, dk), F32), pltpu.VMEM((c + SUBLANES, 3 * w), F32),
               pltpu.VMEM((c, 3 * w), F32)]
    if ns > 1:
        scratch.append(pltpu.VMEM((c + SUBLANES, 3 * w), F32))
    return pl.pallas_call(
        functools.partial(_dn_body, c=c, segs=segs, nh=nh, dk=dk, nsq=_num_squarings(segs)), grid=(nb, nc),
        in_specs=[pl.BlockSpec((c, 3 * w), lambda s, i: (r0 + s * nc + i, 0)),
                  pl.BlockSpec((c, w), lambda s, i: (r0 + s * nc + i, 3)),
                  pl.BlockSpec((c, LANES), lambda s, i: (r0 + s * nc + i, 0)),
                  pl.BlockSpec((ns, CONV_W - 1, 3 * w), lambda s, i: (s, 0, 0)),
                  pl.BlockSpec((CONV_W, 3 * w), lambda s, i: (0, 0)),
                  st,
                  pl.BlockSpec((SUBLANES, LANES), lambda s, i: (0, 0)),
                  pl.BlockSpec((1, dk), lambda s, i: (0, 0))],
        out_specs=[pl.BlockSpec((c, w), lambda s, i: (s * nc + i, 0)), st],
        out_shape=[jax.ShapeDtypeStruct((nb * nc * c, w), BF16),
                   jax.ShapeDtypeStruct((nb * ns, nh, dk, dk), F32)],
        scratch_shapes=scratch,
        compiler_params=_params(("parallel", "arbitrary"), 40), name="deltanet",
    )(seg_a, seg_a, seg_d, conv0, conv_w, h0, pa, ng.reshape(1, dk))


def _ssd_body(x_ref, z_ref, sm_ref, cst_ref, cw_ref, cb_ref, h0_ref, pa_ref, ng_ref, y_ref, ht_ref,
              h_scr, buf, cv, y_scr, *ov, c, segs, ng, nr, hp, ns, off):
    ci = pl.program_id(1)

    @pl.when(ci == 0)
    def _():
        h_scr[...] = h0_ref[...]

    if len(segs) > 1:
        _conv_silu_packed(x_ref, cst_ref, cw_ref, cb_ref, buf, ov[0], cv, c, segs[0][1] - segs[0][0])
    else:
        _conv_silu_rows(x_ref, cst_ref, cw_ref, cb_ref, buf, cv, c, ci)
    gw = nr * hp
    inner = ng * gw
    shift = hp.bit_length() - 1
    sm = sm_ref[...]
    pa = pa_ref[...]
    dt_raw = _softplus(sm + pa[1:2])
    dt_all, gc_all, gct, eg_all = [], [], [], []
    for seg in segs:
        live = _live_rows(c, seg)
        dt_all.append(dt_raw if live is None else jnp.where(live, dt_raw, 0.0))
        gc_all.append(_chunk_cumsum(dt_all[-1] * (-jnp.exp(pa[0:1])), c))
        gct.append(_rows_as_lanes(gc_all[-1], c))
        eg_all.append(jnp.exp(gc_all[-1]))
    tt = lax.broadcasted_iota(jnp.int32, (c, c), 0)
    ss = lax.broadcasted_iota(jnp.int32, (c, c), 1)
    incl = tt >= ss
    lane_head = lax.broadcasted_iota(jnp.int32, (c, gw), 1) >> shift
    lane_head1 = lax.broadcasted_iota(jnp.int32, (1, gw), 1) >> shift
    row_head = lax.broadcasted_iota(jnp.int32, (gw, ns), 0) >> shift
    gs = range(ng)
    units = [(s, g) for s in range(len(segs)) for g in gs]
    us = range(len(units))

    bg = [cv[:, inner + g * ns:inner + (g + 1) * ns] for g in gs]
    cg = [cv[:, inner + (ng + g) * ns:inner + (ng + g + 1) * ns] for g in gs]
    xg = [cv[:, g * gw:(g + 1) * gw] for g in gs]
    hg = [h_scr[s, g] for s, g in units]
    cb = [_bdot_nt(cg[g], bg[g]) for g in gs]
    ch = [_bdot_nt(cg[g], hg[u]) for u, (s, g) in enumerate(units)]
    xdt, xkd, yg, hdec = [], [], [], []
    for u, (s, g) in enumerate(units):
        dtb = jnp.zeros((c, gw), F32)
        egb = jnp.zeros((c, gw), F32)
        kdb = jnp.zeros((c, gw), F32)
        skip = jnp.zeros((1, gw), F32)
        hd = jnp.zeros((gw, ns), F32)
        for r in range(nr):
            col = off + g * nr + r
            gcol = gc_all[s][:, col:col + 1]
            glast = gc_all[s][c - 1:c, col:col + 1]
            seg = lane_head == r
            dtb = jnp.where(seg, dt_all[s][:, col:col + 1], dtb)
            egb = jnp.where(seg, eg_all[s][:, col:col + 1], egb)
            kdb = jnp.where(seg, jnp.exp(glast - gcol), kdb)
            skip = jnp.where(lane_head1 == r, pa[2:3, col:col + 1], skip)
            hd = jnp.where(row_head == r, jnp.exp(glast), hd)
        xdt.append(xg[g] * dtb)
        xkd.append(xdt[u] * kdb)
        yg.append(ch[u] * egb + skip * xg[g])
        hdec.append(hd)
    for r in range(nr):
        part = []
        for u, (s, g) in enumerate(units):
            col = off + g * nr + r
            lm = jnp.exp(jnp.where(incl, gc_all[s][:, col:col + 1] - gct[s][col:col + 1, 0:c], NEG))
            part.append(_bdot(cb[g] * lm, jnp.where(lane_head == r, xdt[u], 0.0)))
        yg = [yg[u] + part[u] for u in us]
    hn = [hdec[u] * hg[u] + _bdot_tn(xkd[u], bg[g]) for u, (s, g) in enumerate(units)]
    for u, (s, g) in enumerate(units):
        h_scr[s, g] = hn[u]
    for g in gs:
        yv = _by_sequence([yg[s * ng + g] for s in range(len(segs))], segs, c, gw)
        y_scr[:, g * gw:(g + 1) * gw] = yv * _silu(z_ref[:, g * gw:(g + 1) * gw])

    y = y_scr[...]
    y = y * lax.rsqrt(jnp.mean(y * y, axis=-1, keepdims=True) + EPS) * ng_ref[...]
    y_ref[...] = y.astype(y_ref.dtype)

    @pl.when(ci == pl.num_programs(1) - 1)
    def _():
        ht_ref[...] = h_scr[...]


def ssd(seg_x, seg_z, seg_d, row0, nb, nc, c, segs, conv0, conv_w, conv_b, h0, pa, ngain, ng, nr, hp, ns, off):
    inner = ng * nr * hp
    sconv = inner + 2 * ng * ns
    r0 = row0 // c
    nseq = len(segs)
    glob = lambda s, i: (r0 + s * nc + i, 0)
    st = pl.BlockSpec((nseq, ng, nr * hp, ns), lambda s, i: (s, 0, 0, 0))
    scratch = [pltpu.VMEM((nseq, ng, nr * hp, ns), F32), pltpu.VMEM((c + SUBLANES, sconv), F32),
               pltpu.VMEM((c, sconv), F32), pltpu.VMEM((c, inner), F32)]
    if nseq > 1:
        scratch.append(pltpu.VMEM((c + SUBLANES, sconv), F32))
    return pl.pallas_call(
        functools.partial(_ssd_body, c=c, segs=segs, ng=ng, nr=nr, hp=hp, ns=ns, off=off), grid=(nb, nc),
        in_specs=[pl.BlockSpec((c, sconv), glob),
                  pl.BlockSpec((c, inner), glob),
                  pl.BlockSpec((c, LANES), glob),
                  pl.BlockSpec((nseq, CONV_W - 1, sconv), lambda s, i: (s, 0, 0)),
                  pl.BlockSpec((CONV_W, sconv), lambda s, i: (0, 0)),
                  pl.BlockSpec((1, sconv), lambda s, i: (0, 0)),
                  st,
                  pl.BlockSpec((SUBLANES, LANES), lambda s, i: (0, 0)),
                  pl.BlockSpec((1, inner), lambda s, i: (0, 0))],
        out_specs=[pl.BlockSpec((c, inner), lambda s, i: (s * nc + i, 0)), st],
        out_shape=[jax.ShapeDtypeStruct((nb * nc * c, inner), BF16),
                   jax.ShapeDtypeStruct((nb * nseq, ng, nr * hp, ns), F32)],
        scratch_shapes=scratch,
        compiler_params=_params(("parallel", "arbitrary"), 40), name="ssd",
    )(seg_x, seg_z, seg_d, conv0, conv_w, conv_b.reshape(1, sconv), h0, pa, ngain.reshape(1, inner))


def _ca_body(q_ref, k_ref, v_ref, o_ref, *, nh, dh, nseq, ls):
    scale = dh ** -0.5
    rows = q_ref.shape[0]
    cols = [slice(h * dh, (h + 1) * dh) for h in range(nh)]
    if len(k_ref.shape) == 4:
        mem = k_ref.shape[1]
        qs = jnp.concatenate([q_ref[:, cols[h]] for h in range(nh)], axis=0)
        col_head = lax.rem(lax.broadcasted_iota(jnp.int32, (nh * rows, mem * nh), 1), nh)
        row_head = lax.div(lax.broadcasted_iota(jnp.int32, (nh * rows, mem * nh), 0), rows)
        own = col_head == row_head
        o = []
        for b in range(nseq):
            s = jnp.where(own, _bdot_nt(qs, k_ref[b].reshape(mem * nh, dh)) * scale, NEG)
            e = jnp.exp(s - jnp.max(s, axis=-1, keepdims=True))
            p = e / jnp.sum(e, axis=-1, keepdims=True)
            ob = _bdot(p, v_ref[b].reshape(mem * nh, dh))
            o += [ob[h * rows:(h + 1) * rows] for h in range(nh)]
    else:
        units = [(b, h) for b in range(nseq) for h in range(nh)]
        s = [_bdot_nt(q_ref[:, cols[h]], k_ref[b, :, cols[h]]) * scale for b, h in units]
        e = [jnp.exp(x - jnp.max(x, axis=-1, keepdims=True)) for x in s]
        p = [x / jnp.sum(x, axis=-1, keepdims=True) for x in e]
        o = [_bdot(p[i], v_ref[b, :, cols[h]]) for i, (b, h) in enumerate(units)]
    row = lax.broadcasted_iota(jnp.int32, (rows, dh), 0)
    for h in range(nh):
        val = o[(nseq - 1) * nh + h]
        for b in range(nseq - 2, -1, -1):
            val = jnp.where(row < (b + 1) * ls, o[b * nh + h], val)
        o_ref[:, cols[h]] = val.astype(o_ref.dtype)


def cross_attention(q2d, row0, nrows, tl, nseq, mem_k, mem_v, nh, dh):
    r0 = row0 // tl
    wd = nh * dh
    per = nrows // tl * nseq // mem_k.shape[0]
    if mem_k.ndim == 4:
        kv = pl.BlockSpec((nseq,) + mem_k.shape[1:], lambda i: (i // per, 0, 0, 0))
    else:
        kv = pl.BlockSpec((nseq,) + mem_k.shape[1:], lambda i: (i // per, 0, 0))
    return pl.pallas_call(
        functools.partial(_ca_body, nh=nh, dh=dh, nseq=nseq, ls=tl // nseq), grid=(nrows // tl,),
        in_specs=[pl.BlockSpec((tl, wd), lambda i: (r0 + i, 0)), kv, kv],
        out_specs=pl.BlockSpec((tl, wd), lambda i: (i, 0)),
        out_shape=jax.ShapeDtypeStruct((nrows, wd), BF16),
        compiler_params=_params(("parallel",), 32), name="cross_attention",
    )(q2d, mem_k, mem_v)


def _extract_top(x, n):
    out = []
    for it in range(n):
        m = jnp.max(x, axis=0, keepdims=True)
        out.append(m)
        if it + 1 < n:
            x = jnp.where(x == m, -jnp.inf, x)
    return out


def _pair_candidates(v1, v2, topk):
    v1s = jnp.concatenate(v1, axis=0)
    v2s = jnp.concatenate(v2, axis=0)
    pieces = []
    a = 0
    while a < topk:
        nb = topk // (a + 1)
        if nb == 1 and a % SUBLANES == 0:
            pieces.append(v1s[a:] + v2[0])
            break
        pieces.append(v1[a] + v2s[:min(-(-nb // SUBLANES) * SUBLANES, topk)])
        a += 1
    return jnp.concatenate(pieces, axis=0)


def _peer_route_body(q_ref, keys_ref, s1_ref, e1_ref, s2_ref, e2_ref, thr_ref, *, nh, dq, topk):
    for h in range(nh):
        s1 = _bdot_nt(keys_ref[2 * h], q_ref[:, (2 * h) * dq:(2 * h + 1) * dq])
        s2 = _bdot_nt(keys_ref[2 * h + 1], q_ref[:, (2 * h + 1) * dq:(2 * h + 2) * dq])
        v1 = _extract_top(s1, topk)
        v2 = _extract_top(s2, topk)
        tops = _extract_top(_pair_candidates(v1, v2, topk), topk)
        zsum = jnp.exp(tops[0] - tops[0])
        for a in range(1, topk):
            zsum = zsum + jnp.exp(tops[a] - tops[0])
        s1_ref[h] = s1
        s2_ref[h] = s2
        e1_ref[h] = jnp.exp(s1 - v1[0]) / zsum
        e2_ref[h] = jnp.exp(s2 - v2[0])
        thr_ref[h:h + 1, :] = tops[topk - 1]


def peer_route(q, keys, nh, nk, dq, tm=256):
    m = q.shape[0]
    tm = _tile(m, tm)
    big = pl.BlockSpec((nh, nk, tm), lambda i: (0, 0, i))
    shape = jax.ShapeDtypeStruct((nh, nk, m), F32)
    return pl.pallas_call(
        functools.partial(_peer_route_body, nh=nh, dq=dq, topk=PEER_TOPK), grid=(m // tm,),
        in_specs=[pl.BlockSpec((tm, 2 * nh * dq), lambda i: (i, 0)),
                  pl.BlockSpec((2 * nh, nk, dq), lambda i: (0, 0, 0))],
        out_specs=[big, big, big, big, pl.BlockSpec((nh, tm), lambda i: (0, i))],
        out_shape=[shape, shape, shape, shape, jax.ShapeDtypeStruct((nh, m), F32)],
        compiler_params=_params(("parallel",), 32), name="peer_route",
    )(q, keys)


def _peer_body(hft_ref, u_ref, vt_ref, s1_ref, e1_ref, s2_ref, e2_ref, thr_ref, ot_ref, st0, st1, act0, act1,
               *, nh, nk, nrow, dchunk):
    ei = pl.program_id(1)
    d, tm = hft_ref.shape
    ne = nk // nrow

    @pl.when(ei == 0)
    def _():
        ot_ref[...] = jnp.zeros_like(ot_ref)

    pieces = nrow
    prow = nrow * nk // pieces

    def step(a, do_scores=True, do_gates=True, do_values=True):
        st_new, st_old = (st0, st1) if a == 0 else (st1, st0)
        act_old, act_new = (act0, act1) if a == 0 else (act1, act0)

        def scores(p):
            rows = slice(p * prow, (p + 1) * prow)
            st_new[rows, :] = jnp.dot(u_ref[rows, :], hft_ref[...], preferred_element_type=F32)

        def apply_values(j):
            ds = slice(j * dchunk, (j + 1) * dchunk)
            ot_ref[ds, :] += jnp.dot(vt_ref[ds, :], act_old[...], preferred_element_type=F32)

        def gate_block(r, tb):
            ls = slice(tb * LANES, (tb + 1) * LANES)
            rows = slice(r * nk, (r + 1) * nk)
            gate = jnp.zeros((nk, LANES), F32)
            for h in range(nh):
                hit = (s2_ref[h, :, ls] + s1_ref[r, h:h + 1, ls]) >= thr_ref[h:h + 1, ls]
                gate = gate + jnp.where(hit, e2_ref[h, :, ls] * e1_ref[r, h:h + 1, ls], 0.0)
            act_new[rows, ls] = (_gelu_tanh(st_old[rows, ls]) * gate).astype(BF16)

        blocks = [(r, tb) for r in range(nrow) for tb in range(tm // LANES)]
        nd = d // dchunk
        per = -(-len(blocks) // nd)
        every = max(nd // pieces, 1)
        for j in range(max(nd, pieces * every)):
            if do_scores and j % every == 0 and j // every < pieces:
                scores(j // every)
            if do_values and j < nd:
                apply_values(j)
            if do_gates:
                for r, tb in blocks[j * per:(j + 1) * per]:
                    gate_block(r, tb)

    steady = jnp.logical_and(ei >= 2, ei < ne)
    for a in range(2):
        pl.when(jnp.logical_and(steady, ei % 2 == a))(functools.partial(step, a))
    for e in sorted({0, 1, ne, ne + 1}):
        if not 2 <= e < ne:
            pl.when(ei == e)(functools.partial(step, e % 2, e < ne, 1 <= e <= ne, e >= 2))


def peer_experts(hft, u, vt, s1g, e1g, s2, e2, thr, nh, nk, tm=512, nrow=4):
    d, m = hft.shape
    tm = _tile(m, tm)
    et = nrow * nk
    once = pl.Buffered(1)
    ne = nk // nrow
    tile = lambda e, lag: jnp.clip(e - lag, 0, ne - 1)
    return pl.pallas_call(
        functools.partial(_peer_body, nh=nh, nk=nk, nrow=nrow, dchunk=_tile(d, 512)), grid=(m // tm, ne + 2),
        in_specs=[pl.BlockSpec((d, tm), lambda t, e: (0, t), pipeline_mode=once),
                  pl.BlockSpec((et, d), lambda t, e: (tile(e, 0), 0)),
                  pl.BlockSpec((d, et), lambda t, e: (0, tile(e, 2))),
                  pl.BlockSpec((nrow, nh, tm), lambda t, e: (tile(e, 1), 0, t)),
                  pl.BlockSpec((nrow, nh, tm), lambda t, e: (tile(e, 1), 0, t)),
                  pl.BlockSpec((nh, nk, tm), lambda t, e: (0, 0, t), pipeline_mode=once),
                  pl.BlockSpec((nh, nk, tm), lambda t, e: (0, 0, t), pipeline_mode=once),
                  pl.BlockSpec((nh, tm), lambda t, e: (0, t))],
        out_specs=pl.BlockSpec((d, tm), lambda t, e: (0, t)),
        out_shape=jax.ShapeDtypeStruct((d, m), F32),
        scratch_shapes=[pltpu.VMEM((et, tm), F32), pltpu.VMEM((et, tm), F32),
                        pltpu.VMEM((et, tm), BF16), pltpu.VMEM((et, tm), BF16)],
        compiler_params=_params(("parallel", "arbitrary"), 56), name="peer_experts",
    )(hft, u, vt, s1g, e1g, s2, e2, thr)


def _param_tile(pieces):
    out = jnp.zeros((SUBLANES, LANES), F32)
    for row, off, vec in pieces:
        out = out.at[row, off:off + vec.shape[0]].set(vec.astype(F32))
    return out


def kernel(x_prompt, x_sample, cache_mem_k, cache_mem_v, state_dn_conv, state_dn_rec, state_ssm_conv,
           state_ssm_rec, mem_prompt, norm_mix_g, w_in, dn_conv_w, dn_a_log, dn_dt_bias, dn_norm_g, dn_w_out,
           ssm_conv_w, ssm_conv_b, ssm_a_log, ssm_dt_bias, ssm_d, ssm_norm_g, ssm_w_out, w_o, norm_ca_g,
           ca_w_q, ca_w_k, ca_w_v, ca_w_o, norm_ffn_g, peer_w_q, peer_keys, peer_u, peer_v, final_norm_g):
    depth = w_in.shape[0]
    bp, lp, d = x_prompt.shape
    bs, ls, _ = x_sample.shape
    _, _, dnh, dk, _ = state_dn_rec.shape
    _, _, sh, hp, ns = state_ssm_rec.shape
    dnw = dnh * dk
    inner = sh * hp
    sconv = state_ssm_conv.shape[-1]
    sg = (sconv - inner) // (2 * ns)
    sr = sh // sg
    _, _, mem, cah, cad = cache_mem_k.shape
    caw = cah * cad
    _, pnh, _, nk, dq = peer_keys.shape
    packed = SUBLANES % ls == 0 and ls >= CONV_W - 1 and bs % (SUBLANES // ls) == 0
    lrow = ls if packed else -(-ls // SUBLANES) * SUBLANES
    cs = SUBLANES if packed else lrow
    nseq = cs // lrow
    s_segs = tuple((j * lrow, j * lrow + ls) for j in range(nseq))
    assert lp % CHUNK == 0 and cs <= CHUNK and dk == LANES and ns == LANES and nk == LANES and dq == LANES
    assert 2 * dnh + sh <= LANES and hp & (hp - 1) == 0
    mp, ms = bp * lp, bs * lrow
    ssm_off = 2 * dnh

    x = [x_prompt.reshape(mp, d), jnp.pad(x_sample, ((0, 0), (0, lrow - ls), (0, 0))).reshape(ms, d)]

    outs = [[] for _ in range(10)]
    for l in range(depth):
        wt = w_in[l].T
        o1 = 3 * dnw + dnw
        o2 = o1 + 2 * dnh
        o3 = o2 + inner
        o4 = o3 + sconv
        o5 = o4 + sh
        w_d = jnp.concatenate([wt[o1:o2], wt[o4:o5], jnp.zeros((LANES - 2 * dnh - sh, d), F32)], axis=0)

        hn = rmsnorm(x, norm_mix_g[l], BF16)
        seg_a = matmul(hn, wt[:o1].astype(BF16), w_rows_out=True)
        seg_z = matmul(hn, wt[o2:o3].astype(BF16), w_rows_out=True)
        seg_x = matmul(hn, wt[o3:o4].astype(BF16), w_rows_out=True)
        seg_c = matmul(hn, wt[o5:].astype(BF16), w_rows_out=True)
        seg_d = matmul(hn, w_d.astype(BF16), w_rows_out=True)

        dn_pa = _param_tile([(0, dnh, dn_a_log[l]), (1, dnh, dn_dt_bias[l])])
        ssm_pa = _param_tile([(0, ssm_off, ssm_a_log[l]), (1, ssm_off, ssm_dt_bias[l]), (2, ssm_off, ssm_d[l])])
        groups = (
            (0, bp, lp // CHUNK, CHUNK, ((0, CHUNK),), jnp.zeros((bp, CONV_W - 1, 3 * dnw), F32),
             jnp.zeros((bp, dnh, dk, dk), F32), jnp.zeros((bp, CONV_W - 1, sconv), F32),
             jnp.zeros((bp, sg, sr * hp, ns), F32)),
            (mp, bs // nseq, 1, cs, s_segs, state_dn_conv[l], state_dn_rec[l],
             state_ssm_conv[l], state_ssm_rec[l].reshape(bs, sg, sr * hp, ns)),
        )
        o_dn, o_ssm, dn_h, ssm_h = [], [], [], []
        for row0, nb, nc, c, segs, dc0, dh0, sc0, sh0 in groups:
            od, hd = deltanet(seg_a, seg_d, row0, nb, nc, c, segs, dc0, dn_conv_w[l], dh0, dn_pa, dn_norm_g[l],
                              dnh, dk)
            osd, hsd = ssd(seg_x, seg_z, seg_d, row0, nb, nc, c, segs, sc0, ssm_conv_w[l], ssm_conv_b[l], sh0, ssm_pa,
                           ssm_norm_g[l], sg, sr, hp, ns, ssm_off)
            o_dn.append(od)
            o_ssm.append(osd)
            dn_h.append(hd)
            ssm_h.append(hsd.reshape(-1, sh, hp, ns))

        mixed = gated_mix(o_dn, dn_w_out[l].astype(BF16), o_ssm, ssm_w_out[l].astype(BF16), seg_c)
        x1 = matmul(mixed, w_o[l].astype(BF16), res=x)

        hc = rmsnorm(x1, norm_ca_g[l], BF16)
        qc = matmul(hc, ca_w_q[l].astype(BF16))
        memp = mem_prompt.reshape(bp * mem, d)
        mk = matmul(memp, ca_w_k[l].astype(BF16))
        mv = matmul(memp, ca_w_v[l].astype(BF16))
        oc = [cross_attention(qc, 0, mp, _tile(lp, 512), 1, mk.reshape(bp, mem, caw), mv.reshape(bp, mem, caw),
                              cah, cad),
              cross_attention(qc, mp, ms, cs, nseq, cache_mem_k[l], cache_mem_v[l], cah, cad)]
        x2 = matmul(oc, ca_w_o[l].astype(BF16), res=x1)

        hf, hft = rmsnorm(x2, norm_ffn_g[l], BF16, with_transpose=True)
        pq = matmul(hf, peer_w_q[l].astype(BF16))
        s1, e1, s2, e2, thr = peer_route(pq, peer_keys[l].reshape(2 * pnh, nk, dq).astype(BF16), pnh, nk, dq)
        pot = peer_experts(hft, peer_u[l].astype(BF16), transpose_cast(peer_v[l], BF16),
                           s1.transpose(1, 0, 2), e1.transpose(1, 0, 2), s2, e2, thr, pnh, nk)
        if l == depth - 1:
            y_prompt = add_rows(x2, pot, final_norm_g, 0, mp).reshape(bp, lp, d)
            y_sample = add_rows(x2, pot, final_norm_g, mp, ms).reshape(bs, lrow, d)[:, :ls]
        else:
            x = add_rows(x2, pot, None, 0, mp + ms)

        def tail(seg, row0, nb, sl, nvalid, lo, hi, state0):
            take = min(CONV_W - 1, nvalid)
            rows = row0 + np.arange(nb)[:, None] * sl + np.arange(nvalid - take, nvalid)[None, :]
            u = jnp.take(seg, rows.reshape(-1), axis=0)[:, lo:hi].reshape(nb, take, hi - lo)
            if take < CONV_W - 1:
                u = jnp.concatenate([state0[:, take:], u], axis=1)
            return u

        outs[0].append(mk.reshape(bp, mem, cah, cad))
        outs[1].append(mv.reshape(bp, mem, cah, cad))
        outs[2].append(tail(seg_a, 0, bp, lp, lp, 0, 3 * dnw, groups[0][5]))
        outs[3].append(dn_h[0])
        outs[4].append(tail(seg_x, 0, bp, lp, lp, 0, sconv, groups[0][7]))
        outs[5].append(ssm_h[0])
        outs[6].append(tail(seg_a, mp, bs, lrow, ls, 0, 3 * dnw, state_dn_conv[l]))
        outs[7].append(dn_h[1])
        outs[8].append(tail(seg_x, mp, bs, lrow, ls, 0, sconv, state_ssm_conv[l]))
        outs[9].append(ssm_h[1])

    return (y_prompt, y_sample) + tuple(jnp.stack(o) for o in outs)
```

```python
import functools
import math

import jax
import jax.numpy as jnp
import numpy as np
from jax import lax
from jax.experimental import pallas as pl
from jax.experimental.pallas import tpu as pltpu

F32 = jnp.float32
BF16 = jnp.bfloat16

EPS = 1e-6
CONV_W = 4
CHUNK = 64
PEER_TOPK = 16
LANES = 128
SUBLANES = 8
NEG = -1e30
MIB = 1 << 20


def _params(sem, vmem_mib):
    return pltpu.CompilerParams(dimension_semantics=sem, vmem_limit_bytes=vmem_mib * MIB)


def _tile(n, pref):
    t = min(pref, n)
    while n % t:
        t //= 2
    return t


def _sigmoid(x):
    return 1.0 / (1.0 + jnp.exp(-x))


def _silu(x):
    return x * _sigmoid(x)


def _softplus(x):
    return jnp.maximum(x, 0.0) + jnp.log1p(jnp.exp(-jnp.abs(x)))


def _gelu_tanh(x):
    return x * (0.5 * (1.0 + jnp.tanh(0.7978845608028654 * (x + 0.044715 * (x * x * x)))))


def _bdot(a, b):
    return jnp.dot(a.astype(BF16), b.astype(BF16), preferred_element_type=F32)


def _bdot_nt(a, b):
    return lax.dot_general(a.astype(BF16), b.astype(BF16), (((1,), (1,)), ((), ())),
                           preferred_element_type=F32)


def _bdot_tn(a, b):
    return lax.dot_general(a.astype(BF16), b.astype(BF16), (((0,), (0,)), ((), ())),
                           preferred_element_type=F32)


def _as_pieces(a):
    return list(a) if isinstance(a, (list, tuple)) else [a]


def _piece_tile(pieces, pref):
    return _tile(math.gcd(*[p.shape[0] for p in pieces]), pref)


def _piece_ends(pieces, tm):
    ends, tot = [], 0
    for p in pieces:
        tot += p.shape[0] // tm
        ends.append(tot)
    return tuple(ends)


def _piece_specs(pieces, tm, cols, row_of, col_of):
    specs, start = [], 0
    for p in pieces:
        n = p.shape[0] // tm
        specs.append(pl.BlockSpec(
            (tm, cols), lambda *g, s=start, n=n: (jnp.clip(row_of(*g) - s, 0, n - 1), col_of(*g))))
        start += n
    return specs


def _piece_value(refs, tile, ends):
    val = refs[-1][...]
    for k in range(len(refs) - 2, -1, -1):
        val = jnp.where(tile < ends[k], refs[k][...], val)
    return val


def _rmsnorm_body(*refs, nx, ends, both):
    x = _piece_value(refs[:nx], pl.program_id(0), ends)
    y = x * lax.rsqrt(jnp.mean(x * x, axis=-1, keepdims=True) + EPS) * refs[nx][...]
    refs[nx + 1][...] = y.astype(refs[nx + 1].dtype)
    if both:
        refs[nx + 2][...] = y.T.astype(refs[nx + 2].dtype)


def _addnorm_body(x_ref, pt_ref, g_ref, o_ref):
    x = x_ref[...] + pt_ref[...].T
    y = x * lax.rsqrt(jnp.mean(x * x, axis=-1, keepdims=True) + EPS)
    o_ref[...] = (y * g_ref[...]).astype(o_ref.dtype)


def _add_body(x_ref, pt_ref, o_ref):
    o_ref[...] = x_ref[...] + pt_ref[...].T


def rmsnorm(x, g, out_dtype, tm=256, with_transpose=False):
    xs = _as_pieces(x)
    d = xs[0].shape[1]
    m = sum(p.shape[0] for p in xs)
    tm = _piece_tile(xs, tm)
    row = pl.BlockSpec((tm, d), lambda i: (i, 0))
    outs, shapes = [row], [jax.ShapeDtypeStruct((m, d), out_dtype)]
    if with_transpose:
        outs.append(pl.BlockSpec((d, tm), lambda i: (0, i)))
        shapes.append(jax.ShapeDtypeStruct((d, m), out_dtype))
    res = pl.pallas_call(
        functools.partial(_rmsnorm_body, nx=len(xs), ends=_piece_ends(xs, tm), both=with_transpose),
        grid=(m // tm,),
        in_specs=_piece_specs(xs, tm, d, lambda i: i, lambda i: 0) + [pl.BlockSpec((1, d), lambda i: (0, 0))],
        out_specs=outs, out_shape=shapes,
        compiler_params=_params(("parallel",), 40), name="rmsnorm",
    )(*xs, g.reshape(1, d))
    return res if with_transpose else res[0]


def add_rows(x, pt, g, row0, nrows, tm=256):
    d = x.shape[1]
    tm = _tile(math.gcd(row0, nrows) if row0 else nrows, tm)
    r0 = row0 // tm
    ins = [pl.BlockSpec((tm, d), lambda i: (r0 + i, 0)), pl.BlockSpec((d, tm), lambda i: (0, r0 + i))]
    if g is None:
        body, args = _add_body, (x, pt)
    else:
        body, args = _addnorm_body, (x, pt, g.reshape(1, d))
        ins.append(pl.BlockSpec((1, d), lambda i: (0, 0)))
    return pl.pallas_call(
        body, grid=(nrows // tm,), in_specs=ins, out_specs=pl.BlockSpec((tm, d), lambda i: (i, 0)),
        out_shape=jax.ShapeDtypeStruct((nrows, d), F32),
        compiler_params=_params(("parallel",), 40), name="add_norm",
    )(*args)


def _transpose_cast_body(x_ref, o_ref):
    o_ref[...] = x_ref[...].T.astype(o_ref.dtype)


def transpose_cast(x, dtype, tr=512, tc=1024):
    r, c = x.shape
    tr, tc = _tile(r, tr), _tile(c, tc)
    return pl.pallas_call(
        _transpose_cast_body, grid=(r // tr, c // tc),
        in_specs=[pl.BlockSpec((tr, tc), lambda i, j: (i, j))],
        out_specs=pl.BlockSpec((tc, tr), lambda i, j: (j, i)),
        out_shape=jax.ShapeDtypeStruct((c, r), dtype),
        compiler_params=_params(("parallel", "parallel"), 32), name="transpose_cast",
    )(x)


def _mm_body(*refs, na, nr, a_ends, r_ends, w_rows_out, cast_w):
    w_ref = refs[na]
    i = pl.program_id(1)
    if cast_w:
        o_ref, w_bf = refs[-2], refs[-1]

        @pl.when(i == 0)
        def _():
            w_bf[...] = w_ref[...].astype(BF16)

        w = w_bf[...]
    else:
        o_ref, w = refs[-1], w_ref[...]
    a = _piece_value(refs[:na], i, a_ends).astype(BF16)
    acc = _bdot_nt(a, w) if w_rows_out else jnp.dot(a, w, preferred_element_type=F32)
    if nr:
        acc = _piece_value(refs[na + 1:na + 1 + nr], i, r_ends) + acc
    o_ref[...] = acc.astype(o_ref.dtype)


def matmul(a, w, res=None, out_dtype=F32, tm=512, tn=1024, w_rows_out=False, w_rows=None):
    a_p = _as_pieces(a)
    r_p = _as_pieces(res) if res is not None else []
    k = a_p[0].shape[1]
    m = sum(p.shape[0] for p in a_p)
    n = w_rows[1] if w_rows else w.shape[0] if w_rows_out else w.shape[1]
    tm, tn = _piece_tile(a_p + r_p, tm), _tile(n, tn)
    row_of, zero = (lambda j, i: i), (lambda j, i: 0)
    scratch = []
    if w_rows:
        assert w_rows_out and w_rows[0] % SUBLANES == 0
        w_spec = pl.BlockSpec((pl.Element(tn), pl.Element(k)),
                              lambda j, i: (pl.multiple_of(w_rows[0] + j * tn, SUBLANES), 0),
                              pipeline_mode=pl.Buffered(1))
        scratch = [pltpu.VMEM((tn, k), BF16)]
    elif w_rows_out:
        w_spec = pl.BlockSpec((tn, k), lambda j, i: (j, 0))
    else:
        w_spec = pl.BlockSpec((k, tn), lambda j, i: (0, j))
    ins = _piece_specs(a_p, tm, k, row_of, zero) + [w_spec] + _piece_specs(r_p, tm, tn, row_of, lambda j, i: j)
    return pl.pallas_call(
        functools.partial(_mm_body, na=len(a_p), nr=len(r_p), a_ends=_piece_ends(a_p, tm),
                          r_ends=_piece_ends(r_p, tm), w_rows_out=w_rows_out, cast_w=bool(w_rows)),
        grid=(n // tn, m // tm),
        in_specs=ins, out_specs=pl.BlockSpec((tm, tn), lambda j, i: (i, j)),
        out_shape=jax.ShapeDtypeStruct((m, n), out_dtype), scratch_shapes=scratch,
        compiler_params=_params(("parallel", "arbitrary" if w_rows else "parallel"), 48), name="matmul",
    )(*a_p, w, *r_p)


def _mix_body(*refs, n1, n2, ends1, ends2):
    w1_ref, w2_ref = refs[n1], refs[n1 + 1 + n2]
    g1_ref, g2_ref, o_ref = refs[-3], refs[-2], refs[-1]
    i = pl.program_id(1)
    y1 = jnp.dot(_piece_value(refs[:n1], i, ends1), w1_ref[...], preferred_element_type=F32)
    y2 = jnp.dot(_piece_value(refs[n1 + 1:n1 + 1 + n2], i, ends2), w2_ref[...], preferred_element_type=F32)
    o_ref[...] = (_sigmoid(g1_ref[...]) * y1 + _sigmoid(g2_ref[...]) * y2).astype(o_ref.dtype)


def gated_mix(a1, w1, a2, w2, gates, tm=512, tn=1024):
    p1, p2 = _as_pieces(a1), _as_pieces(a2)
    k1, k2 = p1[0].shape[1], p2[0].shape[1]
    m, n = gates.shape[0], w1.shape[1]
    tm, tn = _piece_tile(p1 + p2, tm), _tile(n, tn)
    nj = n // tn
    row_of, zero = (lambda j, i: i), (lambda j, i: 0)
    return pl.pallas_call(
        functools.partial(_mix_body, n1=len(p1), n2=len(p2), ends1=_piece_ends(p1, tm), ends2=_piece_ends(p2, tm)),
        grid=(nj, m // tm),
        in_specs=(_piece_specs(p1, tm, k1, row_of, zero) + [pl.BlockSpec((k1, tn), lambda j, i: (0, j))]
                  + _piece_specs(p2, tm, k2, row_of, zero) + [pl.BlockSpec((k2, tn), lambda j, i: (0, j))]
                  + [pl.BlockSpec((tm, tn), lambda j, i: (i, j)), pl.BlockSpec((tm, tn), lambda j, i: (i, j + nj))]),
        out_specs=pl.BlockSpec((tm, tn), lambda j, i: (i, j)),
        out_shape=jax.ShapeDtypeStruct((m, n), BF16),
        compiler_params=_params(("parallel", "parallel"), 48), name="gated_mix",
    )(*p1, w1, *p2, w2, gates, gates)


def _conv_silu_rows(x_ref, st_ref, w_ref, b_ref, buf, cv, c, ci):
    k = CONV_W - 1

    @pl.when(ci == 0)
    def _():
        buf[SUBLANES - k:SUBLANES, :] = st_ref[0]

    buf[SUBLANES:SUBLANES + c, :] = x_ref[...]
    base = SUBLANES - k
    acc = buf[base:base + c, :] * w_ref[0:1, :]
    for j in range(1, CONV_W):
        acc = acc + buf[base + j:base + j + c, :] * w_ref[j:j + 1, :]
    if b_ref is not None:
        acc = acc + b_ref[...]
    cv[...] = _silu(acc)
    buf[0:SUBLANES, :] = buf[c:c + SUBLANES, :]


def _conv_silu_packed(x_ref, st_ref, w_ref, b_ref, buf, ov, cv, c, ls):
    k = CONV_W - 1
    ov[...] = jnp.zeros_like(ov)
    for j in range(c // ls):
        ov[SUBLANES + j * ls - k:SUBLANES + j * ls, :] = st_ref[j]
    buf[0:SUBLANES, :] = jnp.zeros((SUBLANES, buf.shape[1]), F32)
    buf[SUBLANES:SUBLANES + c, :] = x_ref[...]
    base = SUBLANES - k
    in_seq = lax.broadcasted_iota(jnp.int32, (c, buf.shape[1]), 0) & (ls - 1)
    acc = None
    for j in range(CONV_W):
        win = buf[base + j:base + j + c, :]
        if j < k:
            win = jnp.where(in_seq + j < k, ov[base + j:base + j + c, :], win)
        term = win * w_ref[j:j + 1, :]
        acc = term if acc is None else acc + term
    if b_ref is not None:
        acc = acc + b_ref[...]
    cv[...] = _silu(acc)


def _chunk_cumsum(g, c):
    tt = lax.broadcasted_iota(jnp.int32, (c, c), 0)
    ss = lax.broadcasted_iota(jnp.int32, (c, c), 1)
    tril = jnp.where(tt >= ss, 1.0, 0.0).astype(BF16)
    hi = g.astype(BF16)
    r1 = g - hi.astype(F32)
    mid = r1.astype(BF16)
    lo = (r1 - mid.astype(F32)).astype(BF16)
    dot = functools.partial(jnp.dot, preferred_element_type=F32)
    return dot(tril, hi) + dot(tril, mid) + dot(tril, lo)


def _rows_as_lanes(x, c):
    if c < LANES:
        x = jnp.concatenate([x, jnp.zeros((LANES - c, LANES), F32)], axis=0)
    return x.T


def _live_rows(c, seg):
    lo, hi = seg
    if (lo, hi) == (0, c):
        return None
    row = lax.broadcasted_iota(jnp.int32, (c, LANES), 0)
    return jnp.logical_and(row >= lo, row < hi)


def _by_sequence(vals, segs, c, width):
    out = vals[-1]
    if len(vals) > 1:
        row = lax.broadcasted_iota(jnp.int32, (c, width), 0)
        for s in range(len(vals) - 2, -1, -1):
            out = jnp.where(row < segs[s][1], vals[s], out)
    return out


def _dn_body(x_ref, z_ref, sm_ref, cst_ref, cw_ref, h0_ref, pa_ref, ng_ref, o_ref, ht_ref, h_scr, buf, cv, *ov,
             c, segs, nh, dk, nsq):
    ci = pl.program_id(1)

    @pl.when(ci == 0)
    def _():
        h_scr[...] = h0_ref[...]

    if len(segs) > 1:
        _conv_silu_packed(x_ref, cst_ref, cw_ref, None, buf, ov[0], cv, c, segs[0][1] - segs[0][0])
    else:
        _conv_silu_rows(x_ref, cst_ref, cw_ref, None, buf, cv, c, ci)
    w = nh * dk
    sm = sm_ref[...]
    pa = pa_ref[...]
    beta_raw = _sigmoid(sm)
    g_raw = -jnp.exp(pa[0:1]) * _softplus(sm + pa[1:2])
    beta_all, gc_all, gct, eg_all = [], [], [], []
    for seg in segs:
        live = _live_rows(c, seg)
        beta_all.append(beta_raw if live is None else jnp.where(live, beta_raw, 0.0))
        gc_all.append(_chunk_cumsum(g_raw if live is None else jnp.where(live, g_raw, 0.0), c))
        gct.append(_rows_as_lanes(gc_all[-1], c))
        eg_all.append(jnp.exp(gc_all[-1]))
    tt = lax.broadcasted_iota(jnp.int32, (c, c), 0)
    ss = lax.broadcasted_iota(jnp.int32, (c, c), 1)
    strict = tt > ss
    incl = tt >= ss
    scale = dk ** -0.5
    hs = range(nh)
    units = [(s, h) for s in range(len(segs)) for h in hs]
    us = range(len(units))

    q, k = [], []
    for h in hs:
        qh = cv[:, h * dk:(h + 1) * dk]
        kh = cv[:, w + h * dk:w + (h + 1) * dk]
        q.append(qh * lax.rsqrt(jnp.sum(qh * qh, axis=-1, keepdims=True) + EPS) * scale)
        k.append(kh * lax.rsqrt(jnp.sum(kh * kh, axis=-1, keepdims=True) + EPS))
    kk = [_bdot_nt(k[h], k[h]) for h in hs]
    qk = [_bdot_nt(q[h], k[h]) for h in hs]
    hh = [h_scr[s, h] for s, h in units]
    qh0 = [_bdot(q[h], hh[u]) for u, (s, h) in enumerate(units)]
    gcol = [gc_all[s][:, nh + h:nh + h + 1] for s, h in units]
    diff = [gcol[u] - gct[s][nh + h:nh + h + 1, 0:c] for u, (s, h) in enumerate(units)]
    bcol = [beta_all[s][:, h:h + 1] for s, h in units]
    egc = [eg_all[s][:, nh + h:nh + h + 1] for s, h in units]
    glast = [gc_all[s][c - 1:c, nh + h:nh + h + 1] for s, h in units]
    qm = [-(bcol[u] * kk[h] * jnp.exp(jnp.where(strict, diff[u], NEG))) for u, (s, h) in enumerate(units)]
    pw = [_bdot(qm[u], qm[u]) for u in us] if nsq else None
    for it in range(nsq):
        t = [_bdot(qm[u], pw[u]) for u in us]
        nxt = [_bdot(pw[u], pw[u]) for u in us] if it + 1 < nsq else None
        qm = [qm[u] + pw[u] + t[u] for u in us]
        pw = nxt
    rhs = [jnp.concatenate([cv[:, 2 * w + h * dk:2 * w + (h + 1) * dk] * bcol[u], k[h] * (bcol[u] * egc[u])], axis=1)
           for u, (s, h) in enumerate(units)]
    sol = [rhs[u] + _bdot(qm[u], rhs[u]) for u in us]
    wv = [sol[u][:, :dk] - _bdot(sol[u][:, dk:], hh[u]) for u in us]
    o = [qh0[u] * egc[u] + _bdot(qk[h] * jnp.exp(jnp.where(incl, diff[u], NEG)), wv[u])
         for u, (s, h) in enumerate(units)]
    hn = [jnp.exp(glast[u]) * hh[u] + _bdot_tn(k[h] * jnp.exp(glast[u] - gcol[u]), wv[u])
          for u, (s, h) in enumerate(units)]
    for u, (s, h) in enumerate(units):
        h_scr[s, h] = hn[u]
    for h in hs:
        oh = _by_sequence([o[s * nh + h] for s in range(len(segs))], segs, c, dk)
        oh = oh * lax.rsqrt(jnp.mean(oh * oh, axis=-1, keepdims=True) + EPS) * ng_ref[...]
        o_ref[:, h * dk:(h + 1) * dk] = (oh * _silu(z_ref[:, h * dk:(h + 1) * dk])).astype(o_ref.dtype)

    @pl.when(ci == pl.num_programs(1) - 1)
    def _():
        ht_ref[...] = h_scr[...]


def _num_squarings(segs):
    n, longest = 0, max(hi - lo for lo, hi in segs)
    while (2 << n) < longest:
        n += 1
    return n


def deltanet(seg_a, seg_d, row0, nb, nc, c, segs, conv0, conv_w, h0, pa, ng, nh, dk):
    w = nh * dk
    r0 = row0 // c
    ns = len(segs)
    st = pl.BlockSpec((ns, nh, dk, dk), lambda s, i: (s, 0, 0, 0))
    scratch = [pltpu.VMEM((ns, nh, dk, dk), F32), pltpu.VMEM((c + SUBLANES, 3 * w), F32),
               pltpu.VMEM((c, 3 * w), F32)]
    if ns > 1:
        scratch.append(pltpu.VMEM((c + SUBLANES, 3 * w), F32))
    return pl.pallas_call(
        functools.partial(_dn_body, c=c, segs=segs, nh=nh, dk=dk, nsq=_num_squarings(segs)), grid=(nb, nc),
        in_specs=[pl.BlockSpec((c, 3 * w), lambda s, i: (r0 + s * nc + i, 0)),
                  pl.BlockSpec((c, w), lambda s, i: (r0 + s * nc + i, 3)),
                  pl.BlockSpec((c, LANES), lambda s, i: (r0 + s * nc + i, 0)),
                  pl.BlockSpec((ns, CONV_W - 1, 3 * w), lambda s, i: (s, 0, 0)),
                  pl.BlockSpec((CONV_W, 3 * w), lambda s, i: (0, 0)),
                  st,
                  pl.BlockSpec((SUBLANES, LANES), lambda s, i: (0, 0)),
                  pl.BlockSpec((1, dk), lambda s, i: (0, 0))],
        out_specs=[pl.BlockSpec((c, w), lambda s, i: (s * nc + i, 0)), st],
        out_shape=[jax.ShapeDtypeStruct((nb * nc * c, w), BF16),
                   jax.ShapeDtypeStruct((nb * ns, nh, dk, dk), F32)],
        scratch_shapes=scratch,
        compiler_params=_params(("parallel", "arbitrary"), 40), name="deltanet",
    )(seg_a, seg_a, seg_d, conv0, conv_w, h0, pa, ng.reshape(1, dk))


def _ssd_body(x_ref, z_ref, sm_ref, cst_ref, cw_ref, cb_ref, h0_ref, pa_ref, ng_ref, y_ref, ht_ref,
              h_scr, buf, cv, y_scr, *ov, c, segs, ng, nr, hp, ns, off):
    ci = pl.program_id(1)

    @pl.when(ci == 0)
    def _():
        h_scr[...] = h0_ref[...]

    if len(segs) > 1:
        _conv_silu_packed(x_ref, cst_ref, cw_ref, cb_ref, buf, ov[0], cv, c, segs[0][1] - segs[0][0])
    else:
        _conv_silu_rows(x_ref, cst_ref, cw_ref, cb_ref, buf, cv, c, ci)
    gw = nr * hp
    inner = ng * gw
    shift = hp.bit_length() - 1
    sm = sm_ref[...]
    pa = pa_ref[...]
    dt_raw = _softplus(sm + pa[1:2])
    dt_all, gc_all, gct, eg_all = [], [], [], []
    for seg in segs:
        live = _live_rows(c, seg)
        dt_all.append(dt_raw if live is None else jnp.where(live, dt_raw, 0.0))
        gc_all.append(_chunk_cumsum(dt_all[-1] * (-jnp.exp(pa[0:1])), c))
        gct.append(_rows_as_lanes(gc_all[-1], c))
        eg_all.append(jnp.exp(gc_all[-1]))
    tt = lax.broadcasted_iota(jnp.int32, (c, c), 0)
    ss = lax.broadcasted_iota(jnp.int32, (c, c), 1)
    incl = tt >= ss
    lane_head = lax.broadcasted_iota(jnp.int32, (c, gw), 1) >> shift
    lane_head1 = lax.broadcasted_iota(jnp.int32, (1, gw), 1) >> shift
    row_head = lax.broadcasted_iota(jnp.int32, (gw, ns), 0) >> shift
    gs = range(ng)
    units = [(s, g) for s in range(len(segs)) for g in gs]
    us = range(len(units))

    bg = [cv[:, inner + g * ns:inner + (g + 1) * ns] for g in gs]
    cg = [cv[:, inner + (ng + g) * ns:inner + (ng + g + 1) * ns] for g in gs]
    xg = [cv[:, g * gw:(g + 1) * gw] for g in gs]
    hg = [h_scr[s, g] for s, g in units]
    cb = [_bdot_nt(cg[g], bg[g]) for g in gs]
    ch = [_bdot_nt(cg[g], hg[u]) for u, (s, g) in enumerate(units)]
    xdt, xkd, yg, hdec = [], [], [], []
    for u, (s, g) in enumerate(units):
        dtb = jnp.zeros((c, gw), F32)
        egb = jnp.zeros((c, gw), F32)
        kdb = jnp.zeros((c, gw), F32)
        skip = jnp.zeros((1, gw), F32)
        hd = jnp.zeros((gw, ns), F32)
        for r in range(nr):
            col = off + g * nr + r
            gcol = gc_all[s][:, col:col + 1]
            glast = gc_all[s][c - 1:c, col:col + 1]
            seg = lane_head == r
            dtb = jnp.where(seg, dt_all[s][:, col:col + 1], dtb)
            egb = jnp.where(seg, eg_all[s][:, col:col + 1], egb)
            kdb = jnp.where(seg, jnp.exp(glast - gcol), kdb)
            skip = jnp.where(lane_head1 == r, pa[2:3, col:col + 1], skip)
            hd = jnp.where(row_head == r, jnp.exp(glast), hd)
        xdt.append(xg[g] * dtb)
        xkd.append(xdt[u] * kdb)
        yg.append(ch[u] * egb + skip * xg[g])
        hdec.append(hd)
    for r in range(nr):
        part = []
        for u, (s, g) in enumerate(units):
            col = off + g * nr + r
            lm = jnp.exp(jnp.where(incl, gc_all[s][:, col:col + 1] - gct[s][col:col + 1, 0:c], NEG))
            part.append(_bdot(cb[g] * lm, jnp.where(lane_head == r, xdt[u], 0.0)))
        yg = [yg[u] + part[u] for u in us]
    hn = [hdec[u] * hg[u] + _bdot_tn(xkd[u], bg[g]) for u, (s, g) in enumerate(units)]
    for u, (s, g) in enumerate(units):
        h_scr[s, g] = hn[u]
    for g in gs:
        yv = _by_sequence([yg[s * ng + g] for s in range(len(segs))], segs, c, gw)
        y_scr[:, g * gw:(g + 1) * gw] = yv * _silu(z_ref[:, g * gw:(g + 1) * gw])

    y = y_scr[...]
    y = y * lax.rsqrt(jnp.mean(y * y, axis=-1, keepdims=True) + EPS) * ng_ref[...]
    y_ref[...] = y.astype(y_ref.dtype)

    @pl.when(ci == pl.num_programs(1) - 1)
    def _():
        ht_ref[...] = h_scr[...]


def ssd(seg_x, seg_z, seg_d, row0, nb, nc, c, segs, conv0, conv_w, conv_b, h0, pa, ngain, ng, nr, hp, ns, off):
    inner = ng * nr * hp
    sconv = inner + 2 * ng * ns
    r0 = row0 // c
    nseq = len(segs)
    glob = lambda s, i: (r0 + s * nc + i, 0)
    st = pl.BlockSpec((nseq, ng, nr * hp, ns), lambda s, i: (s, 0, 0, 0))
    scratch = [pltpu.VMEM((nseq, ng, nr * hp, ns), F32), pltpu.VMEM((c + SUBLANES, sconv), F32),
               pltpu.VMEM((c, sconv), F32), pltpu.VMEM((c, inner), F32)]
    if nseq > 1:
        scratch.append(pltpu.VMEM((c + SUBLANES, sconv), F32))
    return pl.pallas_call(
        functools.partial(_ssd_body, c=c, segs=segs, ng=ng, nr=nr, hp=hp, ns=ns, off=off), grid=(nb, nc),
        in_specs=[pl.BlockSpec((c, sconv), glob),
                  pl.BlockSpec((c, inner), glob),
                  pl.BlockSpec((c, LANES), glob),
                  pl.BlockSpec((nseq, CONV_W - 1, sconv), lambda s, i: (s, 0, 0)),
                  pl.BlockSpec((CONV_W, sconv), lambda s, i: (0, 0)),
                  pl.BlockSpec((1, sconv), lambda s, i: (0, 0)),
                  st,
                  pl.BlockSpec((SUBLANES, LANES), lambda s, i: (0, 0)),
                  pl.BlockSpec((1, inner), lambda s, i: (0, 0))],
        out_specs=[pl.BlockSpec((c, inner), lambda s, i: (s * nc + i, 0)), st],
        out_shape=[jax.ShapeDtypeStruct((nb * nc * c, inner), BF16),
                   jax.ShapeDtypeStruct((nb * nseq, ng, nr * hp, ns), F32)],
        scratch_shapes=scratch,
        compiler_params=_params(("parallel", "arbitrary"), 40), name="ssd",
    )(seg_x, seg_z, seg_d, conv0, conv_w, conv_b.reshape(1, sconv), h0, pa, ngain.reshape(1, inner))


def _ca_body(q_ref, k_ref, v_ref, o_ref, *, nh, dh, nseq, ls):
    scale = dh ** -0.5
    rows = q_ref.shape[0]
    cols = [slice(h * dh, (h + 1) * dh) for h in range(nh)]
    if len(k_ref.shape) == 4:
        mem = k_ref.shape[1]
        qs = jnp.concatenate([q_ref[:, cols[h]] for h in range(nh)], axis=0)
        col_head = lax.rem(lax.broadcasted_iota(jnp.int32, (nh * rows, mem * nh), 1), nh)
        row_head = lax.div(lax.broadcasted_iota(jnp.int32, (nh * rows, mem * nh), 0), rows)
        own = col_head == row_head
        o = []
        for b in range(nseq):
            s = jnp.where(own, _bdot_nt(qs, k_ref[b].reshape(mem * nh, dh)) * scale, NEG)
            e = jnp.exp(s - jnp.max(s, axis=-1, keepdims=True))
            p = e / jnp.sum(e, axis=-1, keepdims=True)
            ob = _bdot(p, v_ref[b].reshape(mem * nh, dh))
            o += [ob[h * rows:(h + 1) * rows] for h in range(nh)]
    else:
        units = [(b, h) for b in range(nseq) for h in range(nh)]
        s = [_bdot_nt(q_ref[:, cols[h]], k_ref[b, :, cols[h]]) * scale for b, h in units]
        e = [jnp.exp(x - jnp.max(x, axis=-1, keepdims=True)) for x in s]
        p = [x / jnp.sum(x, axis=-1, keepdims=True) for x in e]
        o = [_bdot(p[i], v_ref[b, :, cols[h]]) for i, (b, h) in enumerate(units)]
    row = lax.broadcasted_iota(jnp.int32, (rows, dh), 0)
    for h in range(nh):
        val = o[(nseq - 1) * nh + h]
        for b in range(nseq - 2, -1, -1):
            val = jnp.where(row < (b + 1) * ls, o[b * nh + h], val)
        o_ref[:, cols[h]] = val.astype(o_ref.dtype)


def cross_attention(q2d, row0, nrows, tl, nseq, mem_k, mem_v, nh, dh):
    r0 = row0 // tl
    wd = nh * dh
    per = nrows // tl * nseq // mem_k.shape[0]
    if mem_k.ndim == 4:
        kv = pl.BlockSpec((nseq,) + mem_k.shape[1:], lambda i: (i // per, 0, 0, 0))
    else:
        kv = pl.BlockSpec((nseq,) + mem_k.shape[1:], lambda i: (i // per, 0, 0))
    return pl.pallas_call(
        functools.partial(_ca_body, nh=nh, dh=dh, nseq=nseq, ls=tl // nseq), grid=(nrows // tl,),
        in_specs=[pl.BlockSpec((tl, wd), lambda i: (r0 + i, 0)), kv, kv],
        out_specs=pl.BlockSpec((tl, wd), lambda i: (i, 0)),
        out_shape=jax.ShapeDtypeStruct((nrows, wd), BF16),
        compiler_params=_params(("parallel",), 32), name="cross_attention",
    )(q2d, mem_k, mem_v)


def _extract_top(x, n):
    out = []
    for it in range(n):
        m = jnp.max(x, axis=0, keepdims=True)
        out.append(m)
        if it + 1 < n:
            x = jnp.where(x == m, -jnp.inf, x)
    return out


def _pair_candidates(v1, v2, topk):
    v1s = jnp.concatenate(v1, axis=0)
    v2s = jnp.concatenate(v2, axis=0)
    pieces = []
    a = 0
    while a < topk:
        nb = topk // (a + 1)
        if nb == 1 and a % SUBLANES == 0:
            pieces.append(v1s[a:] + v2[0])
            break
        pieces.append(v1[a] + v2s[:min(-(-nb // SUBLANES) * SUBLANES, topk)])
        a += 1
    return jnp.concatenate(pieces, axis=0)


def _peer_route_body(q_ref, keys_ref, s1_ref, e1_ref, s2_ref, e2_ref, thr_ref, *, nh, dq, topk):
    for h in range(nh):
        s1 = _bdot_nt(keys_ref[2 * h], q_ref[:, (2 * h) * dq:(2 * h + 1) * dq])
        s2 = _bdot_nt(keys_ref[2 * h + 1], q_ref[:, (2 * h + 1) * dq:(2 * h + 2) * dq])
        v1 = _extract_top(s1, topk)
        v2 = _extract_top(s2, topk)
        tops = _extract_top(_pair_candidates(v1, v2, topk), topk)
        zsum = jnp.exp(tops[0] - tops[0])
        for a in range(1, topk):
            zsum = zsum + jnp.exp(tops[a] - tops[0])
        s1_ref[h] = s1
        s2_ref[h] = s2
        e1_ref[h] = jnp.exp(s1 - v1[0]) / zsum
        e2_ref[h] = jnp.exp(s2 - v2[0])
        thr_ref[h:h + 1, :] = tops[topk - 1]


def peer_route(q, keys, nh, nk, dq, tm=256):
    m = q.shape[0]
    tm = _tile(m, tm)
    big = pl.BlockSpec((nh, nk, tm), lambda i: (0, 0, i))
    shape = jax.ShapeDtypeStruct((nh, nk, m), F32)
    return pl.pallas_call(
        functools.partial(_peer_route_body, nh=nh, dq=dq, topk=PEER_TOPK), grid=(m // tm,),
        in_specs=[pl.BlockSpec((tm, 2 * nh * dq), lambda i: (i, 0)),
                  pl.BlockSpec((2 * nh, nk, dq), lambda i: (0, 0, 0))],
        out_specs=[big, big, big, big, pl.BlockSpec((nh, tm), lambda i: (0, i))],
        out_shape=[shape, shape, shape, shape, jax.ShapeDtypeStruct((nh, m), F32)],
        compiler_params=_params(("parallel",), 32), name="peer_route",
    )(q, keys)


def _peer_body(hft_ref, u_ref, vt_ref, s1_ref, e1_ref, s2_ref, e2_ref, thr_ref, ot_ref, st0, st1, act0, act1,
               *, nh, nk, nrow, dchunk):
    ei = pl.program_id(1)
    d, tm = hft_ref.shape
    ne = nk // nrow

    @pl.when(ei == 0)
    def _():
        ot_ref[...] = jnp.zeros_like(ot_ref)

    pieces = nrow
    prow = nrow * nk // pieces

    def step(a, do_scores=True, do_gates=True, do_values=True):
        st_new, st_old = (st0, st1) if a == 0 else (st1, st0)
        act_old, act_new = (act0, act1) if a == 0 else (act1, act0)

        def scores(p):
            rows = slice(p * prow, (p + 1) * prow)
            st_new[rows, :] = jnp.dot(u_ref[rows, :], hft_ref[...], preferred_element_type=F32)

        def apply_values(j):
            ds = slice(j * dchunk, (j + 1) * dchunk)
            ot_ref[ds, :] += jnp.dot(vt_ref[ds, :], act_old[...], preferred_element_type=F32)

        def gate_block(r, tb):
            ls = slice(tb * LANES, (tb + 1) * LANES)
            rows = slice(r * nk, (r + 1) * nk)
            gate = jnp.zeros((nk, LANES), F32)
            for h in range(nh):
                hit = (s2_ref[h, :, ls] + s1_ref[r, h:h + 1, ls]) >= thr_ref[h:h + 1, ls]
                gate = gate + jnp.where(hit, e2_ref[h, :, ls] * e1_ref[r, h:h + 1, ls], 0.0)
            act_new[rows, ls] = (_gelu_tanh(st_old[rows, ls]) * gate).astype(BF16)

        blocks = [(r, tb) for r in range(nrow) for tb in range(tm // LANES)]
        nd = d // dchunk
        per = -(-len(blocks) // nd)
        every = max(nd // pieces, 1)
        for j in range(max(nd, pieces * every)):
            if do_scores and j % every == 0 and j // every < pieces:
                scores(j // every)
            if do_values and j < nd:
                apply_values(j)
            if do_gates:
                for r, tb in blocks[j * per:(j + 1) * per]:
                    gate_block(r, tb)

    steady = jnp.logical_and(ei >= 2, ei < ne)
    for a in range(2):
        pl.when(jnp.logical_and(steady, ei % 2 == a))(functools.partial(step, a))
    for e in sorted({0, 1, ne, ne + 1}):
        if not 2 <= e < ne:
            pl.when(ei == e)(functools.partial(step, e % 2, e < ne, 1 <= e <= ne, e >= 2))


def peer_experts(hft, u, vt, s1g, e1g, s2, e2, thr, nh, nk, tm=512, nrow=4):
    d, m = hft.shape
    tm = _tile(m, tm)
    et = nrow * nk
    once = pl.Buffered(1)
    ne = nk // nrow
    tile = lambda e, lag: jnp.clip(e - lag, 0, ne - 1)
    return pl.pallas_call(
        functools.partial(_peer_body, nh=nh, nk=nk, nrow=nrow, dchunk=_tile(d, 512)), grid=(m // tm, ne + 2),
        in_specs=[pl.BlockSpec((d, tm), lambda t, e: (0, t), pipeline_mode=once),
                  pl.BlockSpec((et, d), lambda t, e: (tile(e, 0), 0)),
                  pl.BlockSpec((d, et), lambda t, e: (0, tile(e, 2))),
                  pl.BlockSpec((nrow, nh, tm), lambda t, e: (tile(e, 1), 0, t)),
                  pl.BlockSpec((nrow, nh, tm), lambda t, e: (tile(e, 1), 0, t)),
                  pl.BlockSpec((nh, nk, tm), lambda t, e: (0, 0, t), pipeline_mode=once),
                  pl.BlockSpec((nh, nk, tm), lambda t, e: (0, 0, t), pipeline_mode=once),
                  pl.BlockSpec((nh, tm), lambda t, e: (0, t))],
        out_specs=pl.BlockSpec((d, tm), lambda t, e: (0, t)),
        out_shape=jax.ShapeDtypeStruct((d, m), F32),
        scratch_shapes=[pltpu.VMEM((et, tm), F32), pltpu.VMEM((et, tm), F32),
                        pltpu.VMEM((et, tm), BF16), pltpu.VMEM((et, tm), BF16)],
        compiler_params=_params(("parallel", "arbitrary"), 56), name="peer_experts",
    )(hft, u, vt, s1g, e1g, s2, e2, thr)


def _param_tile(pieces):
    out = jnp.zeros((SUBLANES, LANES), F32)
    for row, off, vec in pieces:
        out = out.at[row, off:off + vec.shape[0]].set(vec.astype(F32))
    return out


def kernel(x_prompt, x_sample, cache_mem_k, cache_mem_v, state_dn_conv, state_dn_rec, state_ssm_conv,
           state_ssm_rec, mem_prompt, norm_mix_g, w_in, dn_conv_w, dn_a_log, dn_dt_bias, dn_norm_g, dn_w_out,
           ssm_conv_w, ssm_conv_b, ssm_a_log, ssm_dt_bias, ssm_d, ssm_norm_g, ssm_w_out, w_o, norm_ca_g,
           ca_w_q, ca_w_k, ca_w_v, ca_w_o, norm_ffn_g, peer_w_q, peer_keys, peer_u, peer_v, final_norm_g):
    depth = w_in.shape[0]
    bp, lp, d = x_prompt.shape
    bs, ls, _ = x_sample.shape
    _, _, dnh, dk, _ = state_dn_rec.shape
    _, _, sh, hp, ns = state_ssm_rec.shape
    dnw = dnh * dk
    inner = sh * hp
    sconv = state_ssm_conv.shape[-1]
    sg = (sconv - inner) // (2 * ns)
    sr = sh // sg
    _, _, mem, cah, cad = cache_mem_k.shape
    caw = cah * cad
    _, pnh, _, nk, dq = peer_keys.shape
    packed = SUBLANES % ls == 0 and ls >= CONV_W - 1 and bs % (SUBLANES // ls) == 0
    lrow = ls if packed else -(-ls // SUBLANES) * SUBLANES
    cs = SUBLANES if packed else lrow
    nseq = cs // lrow
    s_segs = tuple((j * lrow, j * lrow + ls) for j in range(nseq))
    assert lp % CHUNK == 0 and cs <= CHUNK and dk == LANES and ns == LANES and nk == LANES and dq == LANES
    assert 2 * dnh + sh <= LANES and hp & (hp - 1) == 0
    mp, ms = bp * lp, bs * lrow
    ssm_off = 2 * dnh

    x = [x_prompt.reshape(mp, d), jnp.pad(x_sample, ((0, 0), (0, lrow - ls), (0, 0))).reshape(ms, d)]

    outs = [[] for _ in range(10)]
    for l in range(depth):
        wt = w_in[l].T
        o1 = 3 * dnw + dnw
        o2 = o1 + 2 * dnh
        o3 = o2 + inner
        o4 = o3 + sconv
        o5 = o4 + sh
        w_d = jnp.concatenate([wt[o1:o2], wt[o4:o5], jnp.zeros((LANES - 2 * dnh - sh, d), F32)], axis=0)

        hn = rmsnorm(x, norm_mix_g[l], BF16)
        def proj(start, n):
            if start % SUBLANES:
                return matmul(hn, wt[start:start + n].astype(BF16), w_rows_out=True)
            return matmul(hn, wt, w_rows_out=True, w_rows=(start, n))

        seg_a = proj(0, o1)
        seg_z = proj(o2, o3 - o2)
        seg_x = proj(o3, o4 - o3)
        seg_c = proj(o5, 2 * d)
        seg_d = matmul(hn, w_d.astype(BF16), w_rows_out=True)

        dn_pa = _param_tile([(0, dnh, dn_a_log[l]), (1, dnh, dn_dt_bias[l])])
        ssm_pa = _param_tile([(0, ssm_off, ssm_a_log[l]), (1, ssm_off, ssm_dt_bias[l]), (2, ssm_off, ssm_d[l])])
        groups = (
            (0, bp, lp // CHUNK, CHUNK, ((0, CHUNK),), jnp.zeros((bp, CONV_W - 1, 3 * dnw), F32),
             jnp.zeros((bp, dnh, dk, dk), F32), jnp.zeros((bp, CONV_W - 1, sconv), F32),
             jnp.zeros((bp, sg, sr * hp, ns), F32)),
            (mp, bs // nseq, 1, cs, s_segs, state_dn_conv[l], state_dn_rec[l],
             state_ssm_conv[l], state_ssm_rec[l].reshape(bs, sg, sr * hp, ns)),
        )
        o_dn, o_ssm, dn_h, ssm_h = [], [], [], []
        for row0, nb, nc, c, segs, dc0, dh0, sc0, sh0 in groups:
            od, hd = deltanet(seg_a, seg_d, row0, nb, nc, c, segs, dc0, dn_conv_w[l], dh0, dn_pa, dn_norm_g[l],
                              dnh, dk)
            osd, hsd = ssd(seg_x, seg_z, seg_d, row0, nb, nc, c, segs, sc0, ssm_conv_w[l], ssm_conv_b[l], sh0, ssm_pa,
                           ssm_norm_g[l], sg, sr, hp, ns, ssm_off)
            o_dn.append(od)
            o_ssm.append(osd)
            dn_h.append(hd)
            ssm_h.append(hsd.reshape(-1, sh, hp, ns))

        mixed = gated_mix(o_dn, dn_w_out[l].astype(BF16), o_ssm, ssm_w_out[l].astype(BF16), seg_c)
        x1 = matmul(mixed, w_o[l].astype(BF16), res=x)

        hc = rmsnorm(x1, norm_ca_g[l], BF16)
        qc = matmul(hc, ca_w_q[l].astype(BF16))
        memp = mem_prompt.reshape(bp * mem, d)
        mk = matmul(memp, ca_w_k[l].astype(BF16))
        mv = matmul(memp, ca_w_v[l].astype(BF16))
        oc = [cross_attention(qc, 0, mp, _tile(lp, 512), 1, mk.reshape(bp, mem, caw), mv.reshape(bp, mem, caw),
                              cah, cad),
              cross_attention(qc, mp, ms, cs, nseq, cache_mem_k[l], cache_mem_v[l], cah, cad)]
        x2 = matmul(oc, ca_w_o[l].astype(BF16), res=x1)

        hf, hft = rmsnorm(x2, norm_ffn_g[l], BF16, with_transpose=True)
        pq = matmul(hf, peer_w_q[l].astype(BF16))
        s1, e1, s2, e2, thr = peer_route(pq, peer_keys[l].reshape(2 * pnh, nk, dq).astype(BF16), pnh, nk, dq)
        pot = peer_experts(hft, peer_u[l].astype(BF16), transpose_cast(peer_v[l], BF16),
                           s1.transpose(1, 0, 2), e1.transpose(1, 0, 2), s2, e2, thr, pnh, nk)
        if l == depth - 1:
            y_prompt = add_rows(x2, pot, final_norm_g, 0, mp).reshape(bp, lp, d)
            y_sample = add_rows(x2, pot, final_norm_g, mp, ms).reshape(bs, lrow, d)[:, :ls]
        else:
            x = add_rows(x2, pot, None, 0, mp + ms)

        def tail(seg, row0, nb, sl, nvalid, lo, hi, state0):
            take = min(CONV_W - 1, nvalid)
            rows = row0 + np.arange(nb)[:, None] * sl + np.arange(nvalid - take, nvalid)[None, :]
            u = jnp.take(seg, rows.reshape(-1), axis=0)[:, lo:hi].reshape(nb, take, hi - lo)
            if take < CONV_W - 1:
                u = jnp.concatenate([state0[:, take:], u], axis=1)
            return u

        outs[0].append(mk.reshape(bp, mem, cah, cad))
        outs[1].append(mv.reshape(bp, mem, cah, cad))
        outs[2].append(tail(seg_a, 0, bp, lp, lp, 0, 3 * dnw, groups[0][5]))
        outs[3].append(dn_h[0])
        outs[4].append(tail(seg_x, 0, bp, lp, lp, 0, sconv, groups[0][7]))
        outs[5].append(ssm_h[0])
        outs[6].append(tail(seg_a, mp, bs, lrow, ls, 0, 3 * dnw, state_dn_conv[l]))
        outs[7].append(dn_h[1])
        outs[8].append(tail(seg_x, mp, bs, lrow, ls, 0, sconv, state_ssm_conv[l]))
        outs[9].append(ssm_h[1])

    return (y_prompt, y_sample) + tuple(jnp.stack(o) for o in outs)
```

```python
import functools
import math

import jax
import jax.numpy as jnp
import numpy as np
from jax import lax
from jax.experimental import pallas as pl
from jax.experimental.pallas import tpu as pltpu

F32 = jnp.float32
BF16 = jnp.bfloat16

EPS = 1e-6
CONV_W = 4
CHUNK = 64
PEER_TOPK = 16
LANES = 128
SUBLANES = 8
NEG = -1e30
MIB = 1 << 20


def _params(sem, vmem_mib):
    return pltpu.CompilerParams(dimension_semantics=sem, vmem_limit_bytes=vmem_mib * MIB)


def _tile(n, pref):
    t = min(pref, n)
    while n % t:
        t //= 2
    return t


def _sigmoid(x):
    return 1.0 / (1.0 + jnp.exp(-x))


def _silu(x):
    return x * _sigmoid(x)


def _softplus(x):
    return jnp.maximum(x, 0.0) + jnp.log1p(jnp.exp(-jnp.abs(x)))


def _gelu_tanh(x):
    return x * (0.5 * (1.0 + jnp.tanh(0.7978845608028654 * (x + 0.044715 * (x * x * x)))))


def _bdot(a, b):
    return jnp.dot(a.astype(BF16), b.astype(BF16), preferred_element_type=F32)


def _bdot_nt(a, b):
    return lax.dot_general(a.astype(BF16), b.astype(BF16), (((1,), (1,)), ((), ())),
                           preferred_element_type=F32)


def _bdot_tn(a, b):
    return lax.dot_general(a.astype(BF16), b.astype(BF16), (((0,), (0,)), ((), ())),
                           preferred_element_type=F32)


def _as_pieces(a):
    return list(a) if isinstance(a, (list, tuple)) else [a]


def _piece_tile(pieces, pref):
    return _tile(math.gcd(*[p.shape[0] for p in pieces]), pref)


def _piece_ends(pieces, tm):
    ends, tot = [], 0
    for p in pieces:
        tot += p.shape[0] // tm
        ends.append(tot)
    return tuple(ends)


def _piece_specs(pieces, tm, cols, row_of, col_of):
    specs, start = [], 0
    for p in pieces:
        n = p.shape[0] // tm
        specs.append(pl.BlockSpec(
            (tm, cols), lambda *g, s=start, n=n: (jnp.clip(row_of(*g) - s, 0, n - 1), col_of(*g))))
        start += n
    return specs


def _piece_value(refs, tile, ends):
    val = refs[-1][...]
    for k in range(len(refs) - 2, -1, -1):
        val = jnp.where(tile < ends[k], refs[k][...], val)
    return val


def _rmsnorm_body(*refs, nx, ends, both):
    x = _piece_value(refs[:nx], pl.program_id(0), ends)
    y = x * lax.rsqrt(jnp.mean(x * x, axis=-1, keepdims=True) + EPS) * refs[nx][...]
    refs[nx + 1][...] = y.astype(refs[nx + 1].dtype)
    if both:
        refs[nx + 2][...] = y.T.astype(refs[nx + 2].dtype)


def _addnorm_body(x_ref, pt_ref, g_ref, o_ref):
    x = x_ref[...] + pt_ref[...].T
    y = x * lax.rsqrt(jnp.mean(x * x, axis=-1, keepdims=True) + EPS)
    o_ref[...] = (y * g_ref[...]).astype(o_ref.dtype)


def _add_body(x_ref, pt_ref, o_ref):
    o_ref[...] = x_ref[...] + pt_ref[...].T


def rmsnorm(x, g, out_dtype, tm=256, with_transpose=False):
    xs = _as_pieces(x)
    d = xs[0].shape[1]
    m = sum(p.shape[0] for p in xs)
    tm = _piece_tile(xs, tm)
    row = pl.BlockSpec((tm, d), lambda i: (i, 0))
    outs, shapes = [row], [jax.ShapeDtypeStruct((m, d), out_dtype)]
    if with_transpose:
        outs.append(pl.BlockSpec((d, tm), lambda i: (0, i)))
        shapes.append(jax.ShapeDtypeStruct((d, m), out_dtype))
    res = pl.pallas_call(
        functools.partial(_rmsnorm_body, nx=len(xs), ends=_piece_ends(xs, tm), both=with_transpose),
        grid=(m // tm,),
        in_specs=_piece_specs(xs, tm, d, lambda i: i, lambda i: 0) + [pl.BlockSpec((1, d), lambda i: (0, 0))],
        out_specs=outs, out_shape=shapes,
        compiler_params=_params(("parallel",), 40), name="rmsnorm",
    )(*xs, g.reshape(1, d))
    return res if with_transpose else res[0]


def add_rows(x, pt, g, row0, nrows, tm=256):
    d = x.shape[1]
    tm = _tile(math.gcd(row0, nrows) if row0 else nrows, tm)
    r0 = row0 // tm
    ins = [pl.BlockSpec((tm, d), lambda i: (r0 + i, 0)), pl.BlockSpec((d, tm), lambda i: (0, r0 + i))]
    if g is None:
        body, args = _add_body, (x, pt)
    else:
        body, args = _addnorm_body, (x, pt, g.reshape(1, d))
        ins.append(pl.BlockSpec((1, d), lambda i: (0, 0)))
    return pl.pallas_call(
        body, grid=(nrows // tm,), in_specs=ins, out_specs=pl.BlockSpec((tm, d), lambda i: (i, 0)),
        out_shape=jax.ShapeDtypeStruct((nrows, d), F32),
        compiler_params=_params(("parallel",), 40), name="add_norm",
    )(*args)


def _transpose_cast_body(x_ref, o_ref):
    o_ref[...] = x_ref[...].T.astype(o_ref.dtype)


def transpose_cast(x, dtype, tr=512, tc=1024):
    r, c = x.shape
    tr, tc = _tile(r, tr), _tile(c, tc)
    return pl.pallas_call(
        _transpose_cast_body, grid=(r // tr, c // tc),
        in_specs=[pl.BlockSpec((tr, tc), lambda i, j: (i, j))],
        out_specs=pl.BlockSpec((tc, tr), lambda i, j: (j, i)),
        out_shape=jax.ShapeDtypeStruct((c, r), dtype),
        compiler_params=_params(("parallel", "parallel"), 32), name="transpose_cast",
    )(x)


def _mm_body(*refs, na, nr, a_ends, r_ends, w_rows_out, cast_w):
    w_ref = refs[na]
    i = pl.program_id(1)
    if cast_w:
        o_ref, w_bf = refs[-2], refs[-1]

        @pl.when(i == 0)
        def _():
            w_bf[...] = w_ref[...].astype(BF16)

        w = w_bf[...]
    else:
        o_ref, w = refs[-1], w_ref[...]
    a = _piece_value(refs[:na], i, a_ends).astype(BF16)
    acc = _bdot_nt(a, w) if w_rows_out else jnp.dot(a, w, preferred_element_type=F32)
    if nr:
        acc = _piece_value(refs[na + 1:na + 1 + nr], i, r_ends) + acc
    o_ref[...] = acc.astype(o_ref.dtype)


def matmul(a, w, res=None, out_dtype=F32, tm=512, tn=1024, w_rows_out=False, w_rows=None):
    a_p = _as_pieces(a)
    r_p = _as_pieces(res) if res is not None else []
    k = a_p[0].shape[1]
    m = sum(p.shape[0] for p in a_p)
    n = w_rows[1] if w_rows else w.shape[0] if w_rows_out else w.shape[1]
    tm, tn = _piece_tile(a_p + r_p, tm), _tile(n, tn)
    row_of, zero = (lambda j, i: i), (lambda j, i: 0)
    scratch = []
    if w_rows:
        assert w_rows_out and w_rows[0] % SUBLANES == 0
        w_spec = pl.BlockSpec((pl.Element(tn), pl.Element(k)),
                              lambda j, i: (pl.multiple_of(w_rows[0] + j * tn, SUBLANES), 0),
                              pipeline_mode=pl.Buffered(1))
        scratch = [pltpu.VMEM((tn, k), BF16)]
    elif w_rows_out:
        w_spec = pl.BlockSpec((tn, k), lambda j, i: (j, 0))
    else:
        w_spec = pl.BlockSpec((k, tn), lambda j, i: (0, j))
    ins = _piece_specs(a_p, tm, k, row_of, zero) + [w_spec] + _piece_specs(r_p, tm, tn, row_of, lambda j, i: j)
    return pl.pallas_call(
        functools.partial(_mm_body, na=len(a_p), nr=len(r_p), a_ends=_piece_ends(a_p, tm),
                          r_ends=_piece_ends(r_p, tm), w_rows_out=w_rows_out, cast_w=bool(w_rows)),
        grid=(n // tn, m // tm),
        in_specs=ins, out_specs=pl.BlockSpec((tm, tn), lambda j, i: (i, j)),
        out_shape=jax.ShapeDtypeStruct((m, n), out_dtype), scratch_shapes=scratch,
        compiler_params=_params(("parallel", "arbitrary" if w_rows else "parallel"), 48), name="matmul",
    )(*a_p, w, *r_p)


def _mix_body(*refs, n1, n2, ends1, ends2):
    w1_ref, w2_ref = refs[n1], refs[n1 + 1 + n2]
    g1_ref, g2_ref, o_ref = refs[-3], refs[-2], refs[-1]
    i = pl.program_id(1)
    y1 = jnp.dot(_piece_value(refs[:n1], i, ends1), w1_ref[...], preferred_element_type=F32)
    y2 = jnp.dot(_piece_value(refs[n1 + 1:n1 + 1 + n2], i, ends2), w2_ref[...], preferred_element_type=F32)
    o_ref[...] = (_sigmoid(g1_ref[...]) * y1 + _sigmoid(g2_ref[...]) * y2).astype(o_ref.dtype)


def gated_mix(a1, w1, a2, w2, gates, tm=512, tn=1024):
    p1, p2 = _as_pieces(a1), _as_pieces(a2)
    k1, k2 = p1[0].shape[1], p2[0].shape[1]
    m, n = gates.shape[0], w1.shape[1]
    tm, tn = _piece_tile(p1 + p2, tm), _tile(n, tn)
    nj = n // tn
    row_of, zero = (lambda j, i: i), (lambda j, i: 0)
    return pl.pallas_call(
        functools.partial(_mix_body, n1=len(p1), n2=len(p2), ends1=_piece_ends(p1, tm), ends2=_piece_ends(p2, tm)),
        grid=(nj, m // tm),
        in_specs=(_piece_specs(p1, tm, k1, row_of, zero) + [pl.BlockSpec((k1, tn), lambda j, i: (0, j))]
                  + _piece_specs(p2, tm, k2, row_of, zero) + [pl.BlockSpec((k2, tn), lambda j, i: (0, j))]
                  + [pl.BlockSpec((tm, tn), lambda j, i: (i, j)), pl.BlockSpec((tm, tn), lambda j, i: (i, j + nj))]),
        out_specs=pl.BlockSpec((tm, tn), lambda j, i: (i, j)),
        out_shape=jax.ShapeDtypeStruct((m, n), BF16),
        compiler_params=_params(("parallel", "parallel"), 48), name="gated_mix",
    )(*p1, w1, *p2, w2, gates, gates)


def _conv_silu_rows(x_ref, st_ref, w_ref, b_ref, buf, cv, c, ci):
    k = CONV_W - 1

    @pl.when(ci == 0)
    def _():
        buf[SUBLANES - k:SUBLANES, :] = st_ref[0]

    buf[SUBLANES:SUBLANES + c, :] = x_ref[...]
    base = SUBLANES - k
    acc = buf[base:base + c, :] * w_ref[0:1, :]
    for j in range(1, CONV_W):
        acc = acc + buf[base + j:base + j + c, :] * w_ref[j:j + 1, :]
    if b_ref is not None:
        acc = acc + b_ref[...]
    cv[...] = _silu(acc)
    buf[0:SUBLANES, :] = buf[c:c + SUBLANES, :]


def _conv_silu_packed(x_ref, st_ref, w_ref, b_ref, buf, ov, cv, c, ls):
    k = CONV_W - 1
    ov[...] = jnp.zeros_like(ov)
    for j in range(c // ls):
        ov[SUBLANES + j * ls - k:SUBLANES + j * ls, :] = st_ref[j]
    buf[0:SUBLANES, :] = jnp.zeros((SUBLANES, buf.shape[1]), F32)
    buf[SUBLANES:SUBLANES + c, :] = x_ref[...]
    base = SUBLANES - k
    in_seq = lax.broadcasted_iota(jnp.int32, (c, buf.shape[1]), 0) & (ls - 1)
    acc = None
    for j in range(CONV_W):
        win = buf[base + j:base + j + c, :]
        if j < k:
            win = jnp.where(in_seq + j < k, ov[base + j:base + j + c, :], win)
        term = win * w_ref[j:j + 1, :]
        acc = term if acc is None else acc + term
    if b_ref is not None:
        acc = acc + b_ref[...]
    cv[...] = _silu(acc)


def _chunk_cumsum(g, c):
    tt = lax.broadcasted_iota(jnp.int32, (c, c), 0)
    ss = lax.broadcasted_iota(jnp.int32, (c, c), 1)
    tril = jnp.where(tt >= ss, 1.0, 0.0).astype(BF16)
    hi = g.astype(BF16)
    r1 = g - hi.astype(F32)
    mid = r1.astype(BF16)
    lo = (r1 - mid.astype(F32)).astype(BF16)
    dot = functools.partial(jnp.dot, preferred_element_type=F32)
    return dot(tril, hi) + dot(tril, mid) + dot(tril, lo)


def _rows_as_lanes(x, c):
    if c < LANES:
        x = jnp.concatenate([x, jnp.zeros((LANES - c, LANES), F32)], axis=0)
    return x.T


def _live_rows(c, seg):
    lo, hi = seg
    if (lo, hi) == (0, c):
        return None
    row = lax.broadcasted_iota(jnp.int32, (c, LANES), 0)
    return jnp.logical_and(row >= lo, row < hi)


def _by_sequence(vals, segs, c, width):
    out = vals[-1]
    if len(vals) > 1:
        row = lax.broadcasted_iota(jnp.int32, (c, width), 0)
        for s in range(len(vals) - 2, -1, -1):
            out = jnp.where(row < segs[s][1], vals[s], out)
    return out


def _dn_body(x_ref, z_ref, sm_ref, cst_ref, cw_ref, h0_ref, pa_ref, ng_ref, o_ref, ht_ref, h_scr, buf, cv, *ov,
             c, segs, nh, dk, nsq):
    ci = pl.program_id(1)

    @pl.when(ci == 0)
    def _():
        h_scr[...] = h0_ref[...]

    if len(segs) > 1:
        _conv_silu_packed(x_ref, cst_ref, cw_ref, None, buf, ov[0], cv, c, segs[0][1] - segs[0][0])
    else:
        _conv_silu_rows(x_ref, cst_ref, cw_ref, None, buf, cv, c, ci)
    w = nh * dk
    sm = sm_ref[...]
    pa = pa_ref[...]
    beta_raw = _sigmoid(sm)
    g_raw = -jnp.exp(pa[0:1]) * _softplus(sm + pa[1:2])
    beta_all, gc_all, gct, eg_all = [], [], [], []
    for seg in segs:
        live = _live_rows(c, seg)
        beta_all.append(beta_raw if live is None else jnp.where(live, beta_raw, 0.0))
        gc_all.append(_chunk_cumsum(g_raw if live is None else jnp.where(live, g_raw, 0.0), c))
        gct.append(_rows_as_lanes(gc_all[-1], c))
        eg_all.append(jnp.exp(gc_all[-1]))
    tt = lax.broadcasted_iota(jnp.int32, (c, c), 0)
    ss = lax.broadcasted_iota(jnp.int32, (c, c), 1)
    strict = tt > ss
    incl = tt >= ss
    scale = dk ** -0.5
    hs = range(nh)
    units = [(s, h) for s in range(len(segs)) for h in hs]
    us = range(len(units))

    q, k = [], []
    for h in hs:
        qh = cv[:, h * dk:(h + 1) * dk]
        kh = cv[:, w + h * dk:w + (h + 1) * dk]
        q.append(qh * lax.rsqrt(jnp.sum(qh * qh, axis=-1, keepdims=True) + EPS) * scale)
        k.append(kh * lax.rsqrt(jnp.sum(kh * kh, axis=-1, keepdims=True) + EPS))
    kk = [_bdot_nt(k[h], k[h]) for h in hs]
    qk = [_bdot_nt(q[h], k[h]) for h in hs]
    hh = [h_scr[s, h] for s, h in units]
    qh0 = [_bdot(q[h], hh[u]) for u, (s, h) in enumerate(units)]
    gcol = [gc_all[s][:, nh + h:nh + h + 1] for s, h in units]
    diff = [gcol[u] - gct[s][nh + h:nh + h + 1, 0:c] for u, (s, h) in enumerate(units)]
    bcol = [beta_all[s][:, h:h + 1] for s, h in units]
    egc = [eg_all[s][:, nh + h:nh + h + 1] for s, h in units]
    glast = [gc_all[s][c - 1:c, nh + h:nh + h + 1] for s, h in units]
    qm = [-(bcol[u] * kk[h] * jnp.exp(jnp.where(strict, diff[u], NEG))) for u, (s, h) in enumerate(units)]
    pw = [_bdot(qm[u], qm[u]) for u in us] if nsq else None
    for it in range(nsq):
        t = [_bdot(qm[u], pw[u]) for u in us]
        nxt = [_bdot(pw[u], pw[u]) for u in us] if it + 1 < nsq else None
        qm = [qm[u] + pw[u] + t[u] for u in us]
        pw = nxt
    rhs = [jnp.concatenate([cv[:, 2 * w + h * dk:2 * w + (h + 1) * dk] * bcol[u], k[h] * (bcol[u] * egc[u])], axis=1)
           for u, (s, h) in enumerate(units)]
    sol = [rhs[u] + _bdot(qm[u], rhs[u]) for u in us]
    wv = [sol[u][:, :dk] - _bdot(sol[u][:, dk:], hh[u]) for u in us]
    o = [qh0[u] * egc[u] + _bdot(qk[h] * jnp.exp(jnp.where(incl, diff[u], NEG)), wv[u])
         for u, (s, h) in enumerate(units)]
    hn = [jnp.exp(glast[u]) * hh[u] + _bdot_tn(k[h] * jnp.exp(glast[u] - gcol[u]), wv[u])
          for u, (s, h) in enumerate(units)]
    for u, (s, h) in enumerate(units):
        h_scr[s, h] = hn[u]
    for h in hs:
        oh = _by_sequence([o[s * nh + h] for s in range(len(segs))], segs, c, dk)
        oh = oh * lax.rsqrt(jnp.mean(oh * oh, axis=-1, keepdims=True) + EPS) * ng_ref[...]
        o_ref[:, h * dk:(h + 1) * dk] = (oh * _silu(z_ref[:, h * dk:(h + 1) * dk])).astype(o_ref.dtype)

    @pl.when(ci == pl.num_programs(1) - 1)
    def _():
        ht_ref[...] = h_scr[...]


def _num_squarings(segs):
    n, longest = 0, max(hi - lo for lo, hi in segs)
    while (2 << n) < longest:
        n += 1
    return n


def deltanet(seg_a, seg_d, row0, nb, nc, c, segs, conv0, conv_w, h0, pa, ng, nh, dk):
    w = nh * dk
    r0 = row0 // c
    ns = len(segs)
    st = pl.BlockSpec((ns, nh, dk, dk), lambda s, i: (s, 0, 0, 0))
    scratch = [pltpu.VMEM((ns, nh, dk, dk), F32), pltpu.VMEM((c + SUBLANES, 3 * w), F32),
               pltpu.VMEM((c, 3 * w), F32)]
    if ns > 1:
        scratch.append(pltpu.VMEM((c + SUBLANES, 3 * w), F32))
    return pl.pallas_call(
        functools.partial(_dn_body, c=c, segs=segs, nh=nh, dk=dk, nsq=_num_squarings(segs)), grid=(nb, nc),
        in_specs=[pl.BlockSpec((c, 3 * w), lambda s, i: (r0 + s * nc + i, 0)),
                  pl.BlockSpec((c, w), lambda s, i: (r0 + s * nc + i, 3)),
                  pl.BlockSpec((c, LANES), lambda s, i: (r0 + s * nc + i, 0)),
                  pl.BlockSpec((ns, CONV_W - 1, 3 * w), lambda s, i: (s, 0, 0)),
                  pl.BlockSpec((CONV_W, 3 * w), lambda s, i: (0, 0)),
                  st,
                  pl.BlockSpec((SUBLANES, LANES), lambda s, i: (0, 0)),
                  pl.BlockSpec((1, dk), lambda s, i: (0, 0))],
        out_specs=[pl.BlockSpec((c, w), lambda s, i: (s * nc + i, 0)), st],
        out_shape=[jax.ShapeDtypeStruct((nb * nc * c, w), BF16),
                   jax.ShapeDtypeStruct((nb * ns, nh, dk, dk), F32)],
        scratch_shapes=scratch,
        compiler_params=_params(("parallel", "arbitrary"), 40), name="deltanet",
    )(seg_a, seg_a, seg_d, conv0, conv_w, h0, pa, ng.reshape(1, dk))


def _ssd_body(x_ref, z_ref, sm_ref, cst_ref, cw_ref, cb_ref, h0_ref, pa_ref, ng_ref, y_ref, ht_ref,
              h_scr, buf, cv, y_scr, *ov, c, segs, ng, nr, hp, ns, off):
    ci = pl.program_id(1)

    @pl.when(ci == 0)
    def _():
        h_scr[...] = h0_ref[...]

    if len(segs) > 1:
        _conv_silu_packed(x_ref, cst_ref, cw_ref, cb_ref, buf, ov[0], cv, c, segs[0][1] - segs[0][0])
    else:
        _conv_silu_rows(x_ref, cst_ref, cw_ref, cb_ref, buf, cv, c, ci)
    gw = nr * hp
    inner = ng * gw
    shift = hp.bit_length() - 1
    sm = sm_ref[...]
    pa = pa_ref[...]
    dt_raw = _softplus(sm + pa[1:2])
    dt_all, gc_all, gct, eg_all = [], [], [], []
    for seg in segs:
        live = _live_rows(c, seg)
        dt_all.append(dt_raw if live is None else jnp.where(live, dt_raw, 0.0))
        gc_all.append(_chunk_cumsum(dt_all[-1] * (-jnp.exp(pa[0:1])), c))
        gct.append(_rows_as_lanes(gc_all[-1], c))
        eg_all.append(jnp.exp(gc_all[-1]))
    tt = lax.broadcasted_iota(jnp.int32, (c, c), 0)
    ss = lax.broadcasted_iota(jnp.int32, (c, c), 1)
    incl = tt >= ss
    lane_head = lax.broadcasted_iota(jnp.int32, (c, gw), 1) >> shift
    lane_head1 = lax.broadcasted_iota(jnp.int32, (1, gw), 1) >> shift
    row_head = lax.broadcasted_iota(jnp.int32, (gw, ns), 0) >> shift
    gs = range(ng)
    units = [(s, g) for s in range(len(segs)) for g in gs]
    us = range(len(units))

    bg = [cv[:, inner + g * ns:inner + (g + 1) * ns] for g in gs]
    cg = [cv[:, inner + (ng + g) * ns:inner + (ng + g + 1) * ns] for g in gs]
    xg = [cv[:, g * gw:(g + 1) * gw] for g in gs]
    hg = [h_scr[s, g] for s, g in units]
    cb = [_bdot_nt(cg[g], bg[g]) for g in gs]
    ch = [_bdot_nt(cg[g], hg[u]) for u, (s, g) in enumerate(units)]
    xdt, xkd, yg, hdec = [], [], [], []
    for u, (s, g) in enumerate(units):
        dtb = jnp.zeros((c, gw), F32)
        egb = jnp.zeros((c, gw), F32)
        kdb = jnp.zeros((c, gw), F32)
        skip = jnp.zeros((1, gw), F32)
        hd = jnp.zeros((gw, ns), F32)
        for r in range(nr):
            col = off + g * nr + r
            gcol = gc_all[s][:, col:col + 1]
            glast = gc_all[s][c - 1:c, col:col + 1]
            seg = lane_head == r
            dtb = jnp.where(seg, dt_all[s][:, col:col + 1], dtb)
            egb = jnp.where(seg, eg_all[s][:, col:col + 1], egb)
            kdb = jnp.where(seg, jnp.exp(glast - gcol), kdb)
            skip = jnp.where(lane_head1 == r, pa[2:3, col:col + 1], skip)
            hd = jnp.where(row_head == r, jnp.exp(glast), hd)
        xdt.append(xg[g] * dtb)
        xkd.append(xdt[u] * kdb)
        yg.append(ch[u] * egb + skip * xg[g])
        hdec.append(hd)
    for r in range(nr):
        part = []
        for u, (s, g) in enumerate(units):
            col = off + g * nr + r
            lm = jnp.exp(jnp.where(incl, gc_all[s][:, col:col + 1] - gct[s][col:col + 1, 0:c], NEG))
            part.append(_bdot(cb[g] * lm, jnp.where(lane_head == r, xdt[u], 0.0)))
        yg = [yg[u] + part[u] for u in us]
    hn = [hdec[u] * hg[u] + _bdot_tn(xkd[u], bg[g]) for u, (s, g) in enumerate(units)]
    for u, (s, g) in enumerate(units):
        h_scr[s, g] = hn[u]
    for g in gs:
        yv = _by_sequence([yg[s * ng + g] for s in range(len(segs))], segs, c, gw)
        y_scr[:, g * gw:(g + 1) * gw] = yv * _silu(z_ref[:, g * gw:(g + 1) * gw])

    y = y_scr[...]
    y = y * lax.rsqrt(jnp.mean(y * y, axis=-1, keepdims=True) + EPS) * ng_ref[...]
    y_ref[...] = y.astype(y_ref.dtype)

    @pl.when(ci == pl.num_programs(1) - 1)
    def _():
        ht_ref[...] = h_scr[...]


def ssd(seg_x, seg_z, seg_d, row0, nb, nc, c, segs, conv0, conv_w, conv_b, h0, pa, ngain, ng, nr, hp, ns, off):
    inner = ng * nr * hp
    sconv = inner + 2 * ng * ns
    r0 = row0 // c
    nseq = len(segs)
    glob = lambda s, i: (r0 + s * nc + i, 0)
    st = pl.BlockSpec((nseq, ng, nr * hp, ns), lambda s, i: (s, 0, 0, 0))
    scratch = [pltpu.VMEM((nseq, ng, nr * hp, ns), F32), pltpu.VMEM((c + SUBLANES, sconv), F32),
               pltpu.VMEM((c, sconv), F32), pltpu.VMEM((c, inner), F32)]
    if nseq > 1:
        scratch.append(pltpu.VMEM((c + SUBLANES, sconv), F32))
    return pl.pallas_call(
        functools.partial(_ssd_body, c=c, segs=segs, ng=ng, nr=nr, hp=hp, ns=ns, off=off), grid=(nb, nc),
        in_specs=[pl.BlockSpec((c, sconv), glob),
                  pl.BlockSpec((c, inner), glob),
                  pl.BlockSpec((c, LANES), glob),
                  pl.BlockSpec((nseq, CONV_W - 1, sconv), lambda s, i: (s, 0, 0)),
                  pl.BlockSpec((CONV_W, sconv), lambda s, i: (0, 0)),
                  pl.BlockSpec((1, sconv), lambda s, i: (0, 0)),
                  st,
                  pl.BlockSpec((SUBLANES, LANES), lambda s, i: (0, 0)),
                  pl.BlockSpec((1, inner), lambda s, i: (0, 0))],
        out_specs=[pl.BlockSpec((c, inner), lambda s, i: (s * nc + i, 0)), st],
        out_shape=[jax.ShapeDtypeStruct((nb * nc * c, inner), BF16),
                   jax.ShapeDtypeStruct((nb * nseq, ng, nr * hp, ns), F32)],
        scratch_shapes=scratch,
        compiler_params=_params(("parallel", "arbitrary"), 40), name="ssd",
    )(seg_x, seg_z, seg_d, conv0, conv_w, conv_b.reshape(1, sconv), h0, pa, ngain.reshape(1, inner))


def _ca_body(q_ref, k_ref, v_ref, o_ref, *, nh, dh, nseq, ls):
    scale = dh ** -0.5
    rows = q_ref.shape[0]
    cols = [slice(h * dh, (h + 1) * dh) for h in range(nh)]
    if len(k_ref.shape) == 4:
        mem = k_ref.shape[1]
        qs = jnp.concatenate([q_ref[:, cols[h]] for h in range(nh)], axis=0)
        col_head = lax.rem(lax.broadcasted_iota(jnp.int32, (nh * rows, mem * nh), 1), nh)
        row_head = lax.div(lax.broadcasted_iota(jnp.int32, (nh * rows, mem * nh), 0), rows)
        own = col_head == row_head
        o = []
        for b in range(nseq):
            s = jnp.where(own, _bdot_nt(qs, k_ref[b].reshape(mem * nh, dh)) * scale, NEG)
            e = jnp.exp(s - jnp.max(s, axis=-1, keepdims=True))
            p = e / jnp.sum(e, axis=-1, keepdims=True)
            ob = _bdot(p, v_ref[b].reshape(mem * nh, dh))
            o += [ob[h * rows:(h + 1) * rows] for h in range(nh)]
    else:
        units = [(b, h) for b in range(nseq) for h in range(nh)]
        s = [_bdot_nt(q_ref[:, cols[h]], k_ref[b, :, cols[h]]) * scale for b, h in units]
        e = [jnp.exp(x - jnp.max(x, axis=-1, keepdims=True)) for x in s]
        p = [x / jnp.sum(x, axis=-1, keepdims=True) for x in e]
        o = [_bdot(p[i], v_ref[b, :, cols[h]]) for i, (b, h) in enumerate(units)]
    row = lax.broadcasted_iota(jnp.int32, (rows, dh), 0)
    for h in range(nh):
        val = o[(nseq - 1) * nh + h]
        for b in range(nseq - 2, -1, -1):
            val = jnp.where(row < (b + 1) * ls, o[b * nh + h], val)
        o_ref[:, cols[h]] = val.astype(o_ref.dtype)


def cross_attention(q2d, row0, nrows, tl, nseq, mem_k, mem_v, nh, dh):
    r0 = row0 // tl
    wd = nh * dh
    per = nrows // tl * nseq // mem_k.shape[0]
    if mem_k.ndim == 4:
        kv = pl.BlockSpec((nseq,) + mem_k.shape[1:], lambda i: (i // per, 0, 0, 0))
    else:
        kv = pl.BlockSpec((nseq,) + mem_k.shape[1:], lambda i: (i // per, 0, 0))
    return pl.pallas_call(
        functools.partial(_ca_body, nh=nh, dh=dh, nseq=nseq, ls=tl // nseq), grid=(nrows // tl,),
        in_specs=[pl.BlockSpec((tl, wd), lambda i: (r0 + i, 0)), kv, kv],
        out_specs=pl.BlockSpec((tl, wd), lambda i: (i, 0)),
        out_shape=jax.ShapeDtypeStruct((nrows, wd), BF16),
        compiler_params=_params(("parallel",), 32), name="cross_attention",
    )(q2d, mem_k, mem_v)


def _extract_top(x, n):
    out = []
    for it in range(n):
        m = jnp.max(x, axis=0, keepdims=True)
        out.append(m)
        if it + 1 < n:
            x = jnp.where(x == m, -jnp.inf, x)
    return out


def _pair_candidates(v1, v2, topk):
    v1s = jnp.concatenate(v1, axis=0)
    v2s = jnp.concatenate(v2, axis=0)
    pieces = []
    a = 0
    while a < topk:
        nb = topk // (a + 1)
        if nb == 1 and a % SUBLANES == 0:
            pieces.append(v1s[a:] + v2[0])
            break
        pieces.append(v1[a] + v2s[:min(-(-nb // SUBLANES) * SUBLANES, topk)])
        a += 1
    return jnp.concatenate(pieces, axis=0)


def _peer_route_body(q_ref, keys_ref, s1_ref, e1_ref, s2_ref, e2_ref, thr_ref, *, nh, dq, topk):
    for h in range(nh):
        s1 = _bdot_nt(keys_ref[2 * h], q_ref[:, (2 * h) * dq:(2 * h + 1) * dq])
        s2 = _bdot_nt(keys_ref[2 * h + 1], q_ref[:, (2 * h + 1) * dq:(2 * h + 2) * dq])
        v1 = _extract_top(s1, topk)
        v2 = _extract_top(s2, topk)
        tops = _extract_top(_pair_candidates(v1, v2, topk), topk)
        zsum = jnp.exp(tops[0] - tops[0])
        for a in range(1, topk):
            zsum = zsum + jnp.exp(tops[a] - tops[0])
        s1_ref[h] = s1
        s2_ref[h] = s2
        e1_ref[h] = jnp.exp(s1 - v1[0]) / zsum
        e2_ref[h] = jnp.exp(s2 - v2[0])
        thr_ref[h:h + 1, :] = tops[topk - 1]


def peer_route(q, keys, nh, nk, dq, tm=256):
    m = q.shape[0]
    tm = _tile(m, tm)
    big = pl.BlockSpec((nh, nk, tm), lambda i: (0, 0, i))
    shape = jax.ShapeDtypeStruct((nh, nk, m), F32)
    return pl.pallas_call(
        functools.partial(_peer_route_body, nh=nh, dq=dq, topk=PEER_TOPK), grid=(m // tm,),
        in_specs=[pl.BlockSpec((tm, 2 * nh * dq), lambda i: (i, 0)),
                  pl.BlockSpec((2 * nh, nk, dq), lambda i: (0, 0, 0))],
        out_specs=[big, big, big, big, pl.BlockSpec((nh, tm), lambda i: (0, i))],
        out_shape=[shape, shape, shape, shape, jax.ShapeDtypeStruct((nh, m), F32)],
        compiler_params=_params(("parallel",), 32), name="peer_route",
    )(q, keys)


def _peer_body(hft_ref, u_ref, vt_ref, s1_ref, e1_ref, s2_ref, e2_ref, thr_ref, ot_ref, st0, st1, act0, act1,
               *, nh, nk, nrow, dchunk):
    ei = pl.program_id(1)
    d, tm = hft_ref.shape
    ne = nk // nrow

    @pl.when(ei == 0)
    def _():
        ot_ref[...] = jnp.zeros_like(ot_ref)

    pieces = nrow
    prow = nrow * nk // pieces

    def step(a, do_scores=True, do_gates=True, do_values=True):
        st_new, st_old = (st0, st1) if a == 0 else (st1, st0)
        act_old, act_new = (act0, act1) if a == 0 else (act1, act0)

        def scores(p):
            rows = slice(p * prow, (p + 1) * prow)
            st_new[rows, :] = jnp.dot(u_ref[rows, :], hft_ref[...], preferred_element_type=F32)

        def apply_values(j):
            ds = slice(j * dchunk, (j + 1) * dchunk)
            ot_ref[ds, :] += jnp.dot(vt_ref[ds, :], act_old[...], preferred_element_type=F32)

        def gate_block(r, tb):
            ls = slice(tb * LANES, (tb + 1) * LANES)
            rows = slice(r * nk, (r + 1) * nk)
            gate = jnp.zeros((nk, LANES), F32)
            for h in range(nh):
                hit = (s2_ref[h, :, ls] + s1_ref[r, h:h + 1, ls]) >= thr_ref[h:h + 1, ls]
                gate = gate + jnp.where(hit, e2_ref[h, :, ls] * e1_ref[r, h:h + 1, ls], 0.0)
            act_new[rows, ls] = (_gelu_tanh(st_old[rows, ls]) * gate).astype(BF16)

        blocks = [(r, tb) for r in range(nrow) for tb in range(tm // LANES)]
        nd = d // dchunk
        per = -(-len(blocks) // nd)
        every = max(nd // pieces, 1)
        for j in range(max(nd, pieces * every)):
            if do_scores and j % every == 0 and j // every < pieces:
                scores(j // every)
            if do_values and j < nd:
                apply_values(j)
            if do_gates:
                for r, tb in blocks[j * per:(j + 1) * per]:
                    gate_block(r, tb)

    steady = jnp.logical_and(ei >= 2, ei < ne)
    for a in range(2):
        pl.when(jnp.logical_and(steady, ei % 2 == a))(functools.partial(step, a))
    for e in sorted({0, 1, ne, ne + 1}):
        if not 2 <= e < ne:
            pl.when(ei == e)(functools.partial(step, e % 2, e < ne, 1 <= e <= ne, e >= 2))


def peer_experts(hft, u, vt, s1g, e1g, s2, e2, thr, nh, nk, tm=512, nrow=4):
    d, m = hft.shape
    tm = _tile(m, tm)
    et = nrow * nk
    once = pl.Buffered(1)
    ne = nk // nrow
    tile = lambda e, lag: jnp.clip(e - lag, 0, ne - 1)
    return pl.pallas_call(
        functools.partial(_peer_body, nh=nh, nk=nk, nrow=nrow, dchunk=_tile(d, 512)), grid=(m // tm, ne + 2),
        in_specs=[pl.BlockSpec((d, tm), lambda t, e: (0, t), pipeline_mode=once),
                  pl.BlockSpec((et, d), lambda t, e: (tile(e, 0), 0)),
                  pl.BlockSpec((d, et), lambda t, e: (0, tile(e, 2))),
                  pl.BlockSpec((nrow, nh, tm), lambda t, e: (tile(e, 1), 0, t)),
                  pl.BlockSpec((nrow, nh, tm), lambda t, e: (tile(e, 1), 0, t)),
                  pl.BlockSpec((nh, nk, tm), lambda t, e: (0, 0, t), pipeline_mode=once),
                  pl.BlockSpec((nh, nk, tm), lambda t, e: (0, 0, t), pipeline_mode=once),
                  pl.BlockSpec((nh, tm), lambda t, e: (0, t))],
        out_specs=pl.BlockSpec((d, tm), lambda t, e: (0, t)),
        out_shape=jax.ShapeDtypeStruct((d, m), F32),
        scratch_shapes=[pltpu.VMEM((et, tm), F32), pltpu.VMEM((et, tm), F32),
                        pltpu.VMEM((et, tm), BF16), pltpu.VMEM((et, tm), BF16)],
        compiler_params=_params(("parallel", "arbitrary"), 56), name="peer_experts",
    )(hft, u, vt, s1g, e1g, s2, e2, thr)


def _param_tile(pieces):
    out = jnp.zeros((SUBLANES, LANES), F32)
    for row, off, vec in pieces:
        out = out.at[row, off:off + vec.shape[0]].set(vec.astype(F32))
    return out


def kernel(x_prompt, x_sample, cache_mem_k, cache_mem_v, state_dn_conv, state_dn_rec, state_ssm_conv,
           state_ssm_rec, mem_prompt, norm_mix_g, w_in, dn_conv_w, dn_a_log, dn_dt_bias, dn_norm_g, dn_w_out,
           ssm_conv_w, ssm_conv_b, ssm_a_log, ssm_dt_bias, ssm_d, ssm_norm_g, ssm_w_out, w_o, norm_ca_g,
           ca_w_q, ca_w_k, ca_w_v, ca_w_o, norm_ffn_g, peer_w_q, peer_keys, peer_u, peer_v, final_norm_g):
    depth = w_in.shape[0]
    bp, lp, d = x_prompt.shape
    bs, ls, _ = x_sample.shape
    _, _, dnh, dk, _ = state_dn_rec.shape
    _, _, sh, hp, ns = state_ssm_rec.shape
    dnw = dnh * dk
    inner = sh * hp
    sconv = state_ssm_conv.shape[-1]
    sg = (sconv - inner) // (2 * ns)
    sr = sh // sg
    _, _, mem, cah, cad = cache_mem_k.shape
    caw = cah * cad
    _, pnh, _, nk, dq = peer_keys.shape
    packed = SUBLANES % ls == 0 and ls >= CONV_W - 1 and bs % (SUBLANES // ls) == 0
    lrow = ls if packed else -(-ls // SUBLANES) * SUBLANES
    cs = SUBLANES if packed else lrow
    nseq = cs // lrow
    s_segs = tuple((j * lrow, j * lrow + ls) for j in range(nseq))
    assert lp % CHUNK == 0 and cs <= CHUNK and dk == LANES and ns == LANES and nk == LANES and dq == LANES
    assert 2 * dnh + sh <= LANES and hp & (hp - 1) == 0
    mp, ms = bp * lp, bs * lrow
    ssm_off = 2 * dnh

    x = [x_prompt.reshape(mp, d), jnp.pad(x_sample, ((0, 0), (0, lrow - ls), (0, 0))).reshape(ms, d)]

    outs = [[] for _ in range(10)]
    for l in range(depth):
        wt = w_in[l].T
        o1 = 3 * dnw + dnw
        o2 = o1 + 2 * dnh
        o3 = o2 + inner
        o4 = o3 + sconv
        o5 = o4 + sh
        w_d = jnp.concatenate([wt[o1:o2], wt[o4:o5], jnp.zeros((LANES - 2 * dnh - sh, d), F32)], axis=0)

        hn = rmsnorm(x, norm_mix_g[l], BF16)
        def proj(start, n):
            if start % SUBLANES:
                return matmul(hn, wt[start:start + n].astype(BF16), w_rows_out=True)
            return matmul(hn, wt, w_rows_out=True, w_rows=(start, n))

        seg_a = proj(0, o1)
        seg_z = proj(o2, o3 - o2)
        seg_x = proj(o3, o4 - o3)
        seg_c = proj(o5, 2 * d)
        seg_d = matmul(hn, w_d, w_rows_out=True, w_rows=(0, LANES))

        dn_pa = _param_tile([(0, dnh, dn_a_log[l]), (1, dnh, dn_dt_bias[l])])
        ssm_pa = _param_tile([(0, ssm_off, ssm_a_log[l]), (1, ssm_off, ssm_dt_bias[l]), (2, ssm_off, ssm_d[l])])
        groups = (
            (0, bp, lp // CHUNK, CHUNK, ((0, CHUNK),), jnp.zeros((bp, CONV_W - 1, 3 * dnw), F32),
             jnp.zeros((bp, dnh, dk, dk), F32), jnp.zeros((bp, CONV_W - 1, sconv), F32),
             jnp.zeros((bp, sg, sr * hp, ns), F32)),
            (mp, bs // nseq, 1, cs, s_segs, state_dn_conv[l], state_dn_rec[l],
             state_ssm_conv[l], state_ssm_rec[l].reshape(bs, sg, sr * hp, ns)),
        )
        o_dn, o_ssm, dn_h, ssm_h = [], [], [], []
        for row0, nb, nc, c, segs, dc0, dh0, sc0, sh0 in groups:
            od, hd = deltanet(seg_a, seg_d, row0, nb, nc, c, segs, dc0, dn_conv_w[l], dh0, dn_pa, dn_norm_g[l],
                              dnh, dk)
            osd, hsd = ssd(seg_x, seg_z, seg_d, row0, nb, nc, c, segs, sc0, ssm_conv_w[l], ssm_conv_b[l], sh0, ssm_pa,
                           ssm_norm_g[l], sg, sr, hp, ns, ssm_off)
            o_dn.append(od)
            o_ssm.append(osd)
            dn_h.append(hd)
            ssm_h.append(hsd.reshape(-1, sh, hp, ns))

        mixed = gated_mix(o_dn, dn_w_out[l].astype(BF16), o_ssm, ssm_w_out[l].astype(BF16), seg_c)
        x1 = matmul(mixed, w_o[l].astype(BF16), res=x)

        hc = rmsnorm(x1, norm_ca_g[l], BF16)
        qc = matmul(hc, ca_w_q[l].astype(BF16))
        memp = mem_prompt.reshape(bp * mem, d)
        mk = matmul(memp, ca_w_k[l].astype(BF16))
        mv = matmul(memp, ca_w_v[l].astype(BF16))
        oc = [cross_attention(qc, 0, mp, _tile(lp, 512), 1, mk.reshape(bp, mem, caw), mv.reshape(bp, mem, caw),
                              cah, cad),
              cross_attention(qc, mp, ms, cs, nseq, cache_mem_k[l], cache_mem_v[l], cah, cad)]
        x2 = matmul(oc, ca_w_o[l].astype(BF16), res=x1)

        hf, hft = rmsnorm(x2, norm_ffn_g[l], BF16, with_transpose=True)
        pq = matmul(hf, peer_w_q[l].astype(BF16))
        s1, e1, s2, e2, thr = peer_route(pq, peer_keys[l].reshape(2 * pnh, nk, dq).astype(BF16), pnh, nk, dq)
        pot = peer_experts(hft, peer_u[l].astype(BF16), transpose_cast(peer_v[l], BF16),
                           s1.transpose(1, 0, 2), e1.transpose(1, 0, 2), s2, e2, thr, pnh, nk)
        if l == depth - 1:
            y_prompt = add_rows(x2, pot, final_norm_g, 0, mp).reshape(bp, lp, d)
            y_sample = add_rows(x2, pot, final_norm_g, mp, ms).reshape(bs, lrow, d)[:, :ls]
        else:
            x = add_rows(x2, pot, None, 0, mp + ms)

        def tail(seg, row0, nb, sl, nvalid, lo, hi, state0):
            take = min(CONV_W - 1, nvalid)
            rows = row0 + np.arange(nb)[:, None] * sl + np.arange(nvalid - take, nvalid)[None, :]
            u = jnp.take(seg, rows.reshape(-1), axis=0)[:, lo:hi].reshape(nb, take, hi - lo)
            if take < CONV_W - 1:
                u = jnp.concatenate([state0[:, take:], u], axis=1)
            return u

        outs[0].append(mk.reshape(bp, mem, cah, cad))
        outs[1].append(mv.reshape(bp, mem, cah, cad))
        outs[2].append(tail(seg_a, 0, bp, lp, lp, 0, 3 * dnw, groups[0][5]))
        outs[3].append(dn_h[0])
        outs[4].append(tail(seg_x, 0, bp, lp, lp, 0, sconv, groups[0][7]))
        outs[5].append(ssm_h[0])
        outs[6].append(tail(seg_a, mp, bs, lrow, ls, 0, 3 * dnw, state_dn_conv[l]))
        outs[7].append(dn_h[1])
        outs[8].append(tail(seg_x, mp, bs, lrow, ls, 0, sconv, state_ssm_conv[l]))
        outs[9].append(ssm_h[1])

    return (y_prompt, y_sample) + tuple(jnp.stack(o) for o in outs)
```

```python
import functools
import math

import jax
import jax.numpy as jnp
import numpy as np
from jax import lax
from jax.experimental import pallas as pl
from jax.experimental.pallas import tpu as pltpu

F32 = jnp.float32
BF16 = jnp.bfloat16

EPS = 1e-6
CONV_W = 4
CHUNK = 64
PEER_TOPK = 16
LANES = 128
SUBLANES = 8
NEG = -1e30
MIB = 1 << 20


def _params(sem, vmem_mib):
    return pltpu.CompilerParams(dimension_semantics=sem, vmem_limit_bytes=vmem_mib * MIB)


def _tile(n, pref):
    t = min(pref, n)
    while n % t:
        t //= 2
    return t


def _sigmoid(x):
    return 1.0 / (1.0 + jnp.exp(-x))


def _silu(x):
    return x * _sigmoid(x)


def _softplus(x):
    return jnp.maximum(x, 0.0) + jnp.log1p(jnp.exp(-jnp.abs(x)))


def _gelu_tanh(x):
    return x * (0.5 * (1.0 + jnp.tanh(0.7978845608028654 * (x + 0.044715 * (x * x * x)))))


def _bdot(a, b):
    return jnp.dot(a.astype(BF16), b.astype(BF16), preferred_element_type=F32)


def _bdot_nt(a, b):
    return lax.dot_general(a.astype(BF16), b.astype(BF16), (((1,), (1,)), ((), ())),
                           preferred_element_type=F32)


def _bdot_tn(a, b):
    return lax.dot_general(a.astype(BF16), b.astype(BF16), (((0,), (0,)), ((), ())),
                           preferred_element_type=F32)


def _as_pieces(a):
    return list(a) if isinstance(a, (list, tuple)) else [a]


def _piece_tile(pieces, pref):
    return _tile(math.gcd(*[p.shape[0] for p in pieces]), pref)


def _piece_ends(pieces, tm):
    ends, tot = [], 0
    for p in pieces:
        tot += p.shape[0] // tm
        ends.append(tot)
    return tuple(ends)


def _piece_specs(pieces, tm, cols, row_of, col_of):
    specs, start = [], 0
    for p in pieces:
        n = p.shape[0] // tm
        specs.append(pl.BlockSpec(
            (tm, cols), lambda *g, s=start, n=n: (jnp.clip(row_of(*g) - s, 0, n - 1), col_of(*g))))
        start += n
    return specs


def _piece_value(refs, tile, ends):
    val = refs[-1][...]
    for k in range(len(refs) - 2, -1, -1):
        val = jnp.where(tile < ends[k], refs[k][...], val)
    return val


def _rmsnorm_body(*refs, nx, ends, both):
    x = _piece_value(refs[:nx], pl.program_id(0), ends)
    y = x * lax.rsqrt(jnp.mean(x * x, axis=-1, keepdims=True) + EPS) * refs[nx][...]
    refs[nx + 1][...] = y.astype(refs[nx + 1].dtype)
    if both:
        refs[nx + 2][...] = y.T.astype(refs[nx + 2].dtype)


def _addnorm_body(x_ref, pt_ref, g_ref, o_ref):
    x = x_ref[...] + pt_ref[...].T
    y = x * lax.rsqrt(jnp.mean(x * x, axis=-1, keepdims=True) + EPS)
    o_ref[...] = (y * g_ref[...]).astype(o_ref.dtype)


def _add_body(x_ref, pt_ref, o_ref):
    o_ref[...] = x_ref[...] + pt_ref[...].T


def rmsnorm(x, g, out_dtype, tm=256, with_transpose=False):
    xs = _as_pieces(x)
    d = xs[0].shape[1]
    m = sum(p.shape[0] for p in xs)
    tm = _piece_tile(xs, tm)
    row = pl.BlockSpec((tm, d), lambda i: (i, 0))
    outs, shapes = [row], [jax.ShapeDtypeStruct((m, d), out_dtype)]
    if with_transpose:
        outs.append(pl.BlockSpec((d, tm), lambda i: (0, i)))
        shapes.append(jax.ShapeDtypeStruct((d, m), out_dtype))
    res = pl.pallas_call(
        functools.partial(_rmsnorm_body, nx=len(xs), ends=_piece_ends(xs, tm), both=with_transpose),
        grid=(m // tm,),
        in_specs=_piece_specs(xs, tm, d, lambda i: i, lambda i: 0) + [pl.BlockSpec((1, d), lambda i: (0, 0))],
        out_specs=outs, out_shape=shapes,
        compiler_params=_params(("parallel",), 40), name="rmsnorm",
    )(*xs, g.reshape(1, d))
    return res if with_transpose else res[0]


def add_rows(x, pt, g, row0, nrows, tm=256):
    d = x.shape[1]
    tm = _tile(math.gcd(row0, nrows) if row0 else nrows, tm)
    r0 = row0 // tm
    ins = [pl.BlockSpec((tm, d), lambda i: (r0 + i, 0)), pl.BlockSpec((d, tm), lambda i: (0, r0 + i))]
    if g is None:
        body, args = _add_body, (x, pt)
    else:
        body, args = _addnorm_body, (x, pt, g.reshape(1, d))
        ins.append(pl.BlockSpec((1, d), lambda i: (0, 0)))
    return pl.pallas_call(
        body, grid=(nrows // tm,), in_specs=ins, out_specs=pl.BlockSpec((tm, d), lambda i: (i, 0)),
        out_shape=jax.ShapeDtypeStruct((nrows, d), F32),
        compiler_params=_params(("parallel",), 40), name="add_norm",
    )(*args)


def _transpose_cast_body(x_ref, o_ref):
    o_ref[...] = x_ref[...].T.astype(o_ref.dtype)


def transpose_cast(x, dtype, tr=512, tc=1024):
    r, c = x.shape
    tr, tc = _tile(r, tr), _tile(c, tc)
    return pl.pallas_call(
        _transpose_cast_body, grid=(r // tr, c // tc),
        in_specs=[pl.BlockSpec((tr, tc), lambda i, j: (i, j))],
        out_specs=pl.BlockSpec((tc, tr), lambda i, j: (j, i)),
        out_shape=jax.ShapeDtypeStruct((c, r), dtype),
        compiler_params=_params(("parallel", "parallel"), 32), name="transpose_cast",
    )(x)


def _mm_body(*refs, na, nr, a_ends, r_ends, w_rows_out, cast_w):
    w_ref = refs[na]
    i = pl.program_id(1)
    if cast_w:
        o_ref, w_bf = refs[-2], refs[-1]

        @pl.when(i == 0)
        def _():
            w_bf[...] = w_ref[...].astype(BF16)

        w = w_bf[...]
    else:
        o_ref, w = refs[-1], w_ref[...]
    a = _piece_value(refs[:na], i, a_ends).astype(BF16)
    acc = _bdot_nt(a, w) if w_rows_out else jnp.dot(a, w, preferred_element_type=F32)
    if nr:
        acc = _piece_value(refs[na + 1:na + 1 + nr], i, r_ends) + acc
    o_ref[...] = acc.astype(o_ref.dtype)


def matmul(a, w, res=None, out_dtype=F32, tm=512, tn=1024, w_rows_out=False, w_rows=None):
    a_p = _as_pieces(a)
    r_p = _as_pieces(res) if res is not None else []
    k = a_p[0].shape[1]
    m = sum(p.shape[0] for p in a_p)
    n = w_rows[1] if w_rows else w.shape[0] if w_rows_out else w.shape[1]
    tm, tn = _piece_tile(a_p + r_p, tm), _tile(n, tn)
    row_of, zero = (lambda j, i: i), (lambda j, i: 0)
    scratch = []
    if w_rows:
        assert w_rows_out and w_rows[0] % SUBLANES == 0
        w_spec = pl.BlockSpec((pl.Element(tn), pl.Element(k)),
                              lambda j, i: (pl.multiple_of(w_rows[0] + j * tn, SUBLANES), 0))
        scratch = [pltpu.VMEM((tn, k), BF16)]
    elif w_rows_out:
        w_spec = pl.BlockSpec((tn, k), lambda j, i: (j, 0))
    else:
        w_spec = pl.BlockSpec((k, tn), lambda j, i: (0, j))
    ins = _piece_specs(a_p, tm, k, row_of, zero) + [w_spec] + _piece_specs(r_p, tm, tn, row_of, lambda j, i: j)
    return pl.pallas_call(
        functools.partial(_mm_body, na=len(a_p), nr=len(r_p), a_ends=_piece_ends(a_p, tm),
                          r_ends=_piece_ends(r_p, tm), w_rows_out=w_rows_out, cast_w=bool(w_rows)),
        grid=(n // tn, m // tm),
        in_specs=ins, out_specs=pl.BlockSpec((tm, tn), lambda j, i: (i, j)),
        out_shape=jax.ShapeDtypeStruct((m, n), out_dtype), scratch_shapes=scratch,
        compiler_params=_params(("parallel", "arbitrary" if w_rows else "parallel"), 58 if w_rows else 48),
        name="matmul",
    )(*a_p, w, *r_p)


def _mix_body(*refs, n1, n2, ends1, ends2):
    w1_ref, w2_ref = refs[n1], refs[n1 + 1 + n2]
    g1_ref, g2_ref, o_ref = refs[-3], refs[-2], refs[-1]
    i = pl.program_id(1)
    y1 = jnp.dot(_piece_value(refs[:n1], i, ends1), w1_ref[...], preferred_element_type=F32)
    y2 = jnp.dot(_piece_value(refs[n1 + 1:n1 + 1 + n2], i, ends2), w2_ref[...], preferred_element_type=F32)
    o_ref[...] = (_sigmoid(g1_ref[...]) * y1 + _sigmoid(g2_ref[...]) * y2).astype(o_ref.dtype)


def gated_mix(a1, w1, a2, w2, gates, tm=512, tn=1024):
    p1, p2 = _as_pieces(a1), _as_pieces(a2)
    k1, k2 = p1[0].shape[1], p2[0].shape[1]
    m, n = gates.shape[0], w1.shape[1]
    tm, tn = _piece_tile(p1 + p2, tm), _tile(n, tn)
    nj = n // tn
    row_of, zero = (lambda j, i: i), (lambda j, i: 0)
    return pl.pallas_call(
        functools.partial(_mix_body, n1=len(p1), n2=len(p2), ends1=_piece_ends(p1, tm), ends2=_piece_ends(p2, tm)),
        grid=(nj, m // tm),
        in_specs=(_piece_specs(p1, tm, k1, row_of, zero) + [pl.BlockSpec((k1, tn), lambda j, i: (0, j))]
                  + _piece_specs(p2, tm, k2, row_of, zero) + [pl.BlockSpec((k2, tn), lambda j, i: (0, j))]
                  + [pl.BlockSpec((tm, tn), lambda j, i: (i, j)), pl.BlockSpec((tm, tn), lambda j, i: (i, j + nj))]),
        out_specs=pl.BlockSpec((tm, tn), lambda j, i: (i, j)),
        out_shape=jax.ShapeDtypeStruct((m, n), BF16),
        compiler_params=_params(("parallel", "parallel"), 48), name="gated_mix",
    )(*p1, w1, *p2, w2, gates, gates)


def _conv_silu_rows(x_ref, st_ref, w_ref, b_ref, buf, cv, c, ci):
    k = CONV_W - 1

    @pl.when(ci == 0)
    def _():
        buf[SUBLANES - k:SUBLANES, :] = st_ref[0]

    buf[SUBLANES:SUBLANES + c, :] = x_ref[...]
    base = SUBLANES - k
    acc = buf[base:base + c, :] * w_ref[0:1, :]
    for j in range(1, CONV_W):
        acc = acc + buf[base + j:base + j + c, :] * w_ref[j:j + 1, :]
    if b_ref is not None:
        acc = acc + b_ref[...]
    cv[...] = _silu(acc)
    buf[0:SUBLANES, :] = buf[c:c + SUBLANES, :]


def _conv_silu_packed(x_ref, st_ref, w_ref, b_ref, buf, ov, cv, c, ls):
    k = CONV_W - 1
    ov[...] = jnp.zeros_like(ov)
    for j in range(c // ls):
        ov[SUBLANES + j * ls - k:SUBLANES + j * ls, :] = st_ref[j]
    buf[0:SUBLANES, :] = jnp.zeros((SUBLANES, buf.shape[1]), F32)
    buf[SUBLANES:SUBLANES + c, :] = x_ref[...]
    base = SUBLANES - k
    in_seq = lax.broadcasted_iota(jnp.int32, (c, buf.shape[1]), 0) & (ls - 1)
    acc = None
    for j in range(CONV_W):
        win = buf[base + j:base + j + c, :]
        if j < k:
            win = jnp.where(in_seq + j < k, ov[base + j:base + j + c, :], win)
        term = win * w_ref[j:j + 1, :]
        acc = term if acc is None else acc + term
    if b_ref is not None:
        acc = acc + b_ref[...]
    cv[...] = _silu(acc)


def _chunk_cumsum(g, c):
    tt = lax.broadcasted_iota(jnp.int32, (c, c), 0)
    ss = lax.broadcasted_iota(jnp.int32, (c, c), 1)
    tril = jnp.where(tt >= ss, 1.0, 0.0).astype(BF16)
    hi = g.astype(BF16)
    r1 = g - hi.astype(F32)
    mid = r1.astype(BF16)
    lo = (r1 - mid.astype(F32)).astype(BF16)
    dot = functools.partial(jnp.dot, preferred_element_type=F32)
    return dot(tril, hi) + dot(tril, mid) + dot(tril, lo)


def _rows_as_lanes(x, c):
    if c < LANES:
        x = jnp.concatenate([x, jnp.zeros((LANES - c, LANES), F32)], axis=0)
    return x.T


def _live_rows(c, seg):
    lo, hi = seg
    if (lo, hi) == (0, c):
        return None
    row = lax.broadcasted_iota(jnp.int32, (c, LANES), 0)
    return jnp.logical_and(row >= lo, row < hi)


def _by_sequence(vals, segs, c, width):
    out = vals[-1]
    if len(vals) > 1:
        row = lax.broadcasted_iota(jnp.int32, (c, width), 0)
        for s in range(len(vals) - 2, -1, -1):
            out = jnp.where(row < segs[s][1], vals[s], out)
    return out


def _dn_body(x_ref, z_ref, sm_ref, cst_ref, cw_ref, h0_ref, pa_ref, ng_ref, o_ref, ht_ref, h_scr, buf, cv, *ov,
             c, segs, nh, dk, nsq):
    ci = pl.program_id(1)

    @pl.when(ci == 0)
    def _():
        h_scr[...] = h0_ref[...]

    if len(segs) > 1:
        _conv_silu_packed(x_ref, cst_ref, cw_ref, None, buf, ov[0], cv, c, segs[0][1] - segs[0][0])
    else:
        _conv_silu_rows(x_ref, cst_ref, cw_ref, None, buf, cv, c, ci)
    w = nh * dk
    sm = sm_ref[...]
    pa = pa_ref[...]
    beta_raw = _sigmoid(sm)
    g_raw = -jnp.exp(pa[0:1]) * _softplus(sm + pa[1:2])
    beta_all, gc_all, gct, eg_all = [], [], [], []
    for seg in segs:
        live = _live_rows(c, seg)
        beta_all.append(beta_raw if live is None else jnp.where(live, beta_raw, 0.0))
        gc_all.append(_chunk_cumsum(g_raw if live is None else jnp.where(live, g_raw, 0.0), c))
        gct.append(_rows_as_lanes(gc_all[-1], c))
        eg_all.append(jnp.exp(gc_all[-1]))
    tt = lax.broadcasted_iota(jnp.int32, (c, c), 0)
    ss = lax.broadcasted_iota(jnp.int32, (c, c), 1)
    strict = tt > ss
    incl = tt >= ss
    scale = dk ** -0.5
    hs = range(nh)
    units = [(s, h) for s in range(len(segs)) for h in hs]
    us = range(len(units))

    q, k = [], []
    for h in hs:
        qh = cv[:, h * dk:(h + 1) * dk]
        kh = cv[:, w + h * dk:w + (h + 1) * dk]
        q.append(qh * lax.rsqrt(jnp.sum(qh * qh, axis=-1, keepdims=True) + EPS) * scale)
        k.append(kh * lax.rsqrt(jnp.sum(kh * kh, axis=-1, keepdims=True) + EPS))
    kk = [_bdot_nt(k[h], k[h]) for h in hs]
    qk = [_bdot_nt(q[h], k[h]) for h in hs]
    hh = [h_scr[s, h] for s, h in units]
    qh0 = [_bdot(q[h], hh[u]) for u, (s, h) in enumerate(units)]
    gcol = [gc_all[s][:, nh + h:nh + h + 1] for s, h in units]
    diff = [gcol[u] - gct[s][nh + h:nh + h + 1, 0:c] for u, (s, h) in enumerate(units)]
    bcol = [beta_all[s][:, h:h + 1] for s, h in units]
    egc = [eg_all[s][:, nh + h:nh + h + 1] for s, h in units]
    glast = [gc_all[s][c - 1:c, nh + h:nh + h + 1] for s, h in units]
    qm = [-(bcol[u] * kk[h] * jnp.exp(jnp.where(strict, diff[u], NEG))) for u, (s, h) in enumerate(units)]
    pw = [_bdot(qm[u], qm[u]) for u in us] if nsq else None
    for it in range(nsq):
        t = [_bdot(qm[u], pw[u]) for u in us]
        nxt = [_bdot(pw[u], pw[u]) for u in us] if it + 1 < nsq else None
        qm = [qm[u] + pw[u] + t[u] for u in us]
        pw = nxt
    rhs = [jnp.concatenate([cv[:, 2 * w + h * dk:2 * w + (h + 1) * dk] * bcol[u], k[h] * (bcol[u] * egc[u])], axis=1)
           for u, (s, h) in enumerate(units)]
    sol = [rhs[u] + _bdot(qm[u], rhs[u]) for u in us]
    wv = [sol[u][:, :dk] - _bdot(sol[u][:, dk:], hh[u]) for u in us]
    o = [qh0[u] * egc[u] + _bdot(qk[h] * jnp.exp(jnp.where(incl, diff[u], NEG)), wv[u])
         for u, (s, h) in enumerate(units)]
    hn = [jnp.exp(glast[u]) * hh[u] + _bdot_tn(k[h] * jnp.exp(glast[u] - gcol[u]), wv[u])
          for u, (s, h) in enumerate(units)]
    for u, (s, h) in enumerate(units):
        h_scr[s, h] = hn[u]
    for h in hs:
        oh = _by_sequence([o[s * nh + h] for s in range(len(segs))], segs, c, dk)
        oh = oh * lax.rsqrt(jnp.mean(oh * oh, axis=-1, keepdims=True) + EPS) * ng_ref[...]
        o_ref[:, h * dk:(h + 1) * dk] = (oh * _silu(z_ref[:, h * dk:(h + 1) * dk])).astype(o_ref.dtype)

    @pl.when(ci == pl.num_programs(1) - 1)
    def _():
        ht_ref[...] = h_scr[...]


def _num_squarings(segs):
    n, longest = 0, max(hi - lo for lo, hi in segs)
    while (2 << n) < longest:
        n += 1
    return n


def deltanet(seg_a, seg_d, row0, nb, nc, c, segs, conv0, conv_w, h0, pa, ng, nh, dk):
    w = nh * dk
    r0 = row0 // c
    ns = len(segs)
    st = pl.BlockSpec((ns, nh, dk, dk), lambda s, i: (s, 0, 0, 0))
    scratch = [pltpu.VMEM((ns, nh, dk, dk), F32), pltpu.VMEM((c + SUBLANES, 3 * w), F32),
               pltpu.VMEM((c, 3 * w), F32)]
    if ns > 1:
        scratch.append(pltpu.VMEM((c + SUBLANES, 3 * w), F32))
    return pl.pallas_call(
        functools.partial(_dn_body, c=c, segs=segs, nh=nh, dk=dk, nsq=_num_squarings(segs)), grid=(nb, nc),
        in_specs=[pl.BlockSpec((c, 3 * w), lambda s, i: (r0 + s * nc + i, 0)),
                  pl.BlockSpec((c, w), lambda s, i: (r0 + s * nc + i, 3)),
                  pl.BlockSpec((c, LANES), lambda s, i: (r0 + s * nc + i, 0)),
                  pl.BlockSpec((ns, CONV_W - 1, 3 * w), lambda s, i: (s, 0, 0)),
                  pl.BlockSpec((CONV_W, 3 * w), lambda s, i: (0, 0)),
                  st,
                  pl.BlockSpec((SUBLANES, LANES), lambda s, i: (0, 0)),
                  pl.BlockSpec((1, dk), lambda s, i: (0, 0))],
        out_specs=[pl.BlockSpec((c, w), lambda s, i: (s * nc + i, 0)), st],
        out_shape=[jax.ShapeDtypeStruct((nb * nc * c, w), BF16),
                   jax.ShapeDtypeStruct((nb * ns, nh, dk, dk), F32)],
        scratch_shapes=scratch,
        compiler_params=_params(("parallel", "arbitrary"), 40), name="deltanet",
    )(seg_a, seg_a, seg_d, conv0, conv_w, h0, pa, ng.reshape(1, dk))


def _ssd_body(x_ref, z_ref, sm_ref, cst_ref, cw_ref, cb_ref, h0_ref, pa_ref, ng_ref, y_ref, ht_ref,
              h_scr, buf, cv, y_scr, *ov, c, segs, ng, nr, hp, ns, off):
    ci = pl.program_id(1)

    @pl.when(ci == 0)
    def _():
        h_scr[...] = h0_ref[...]

    if len(segs) > 1:
        _conv_silu_packed(x_ref, cst_ref, cw_ref, cb_ref, buf, ov[0], cv, c, segs[0][1] - segs[0][0])
    else:
        _conv_silu_rows(x_ref, cst_ref, cw_ref, cb_ref, buf, cv, c, ci)
    gw = nr * hp
    inner = ng * gw
    shift = hp.bit_length() - 1
    sm = sm_ref[...]
    pa = pa_ref[...]
    dt_raw = _softplus(sm + pa[1:2])
    dt_all, gc_all, gct, eg_all = [], [], [], []
    for seg in segs:
        live = _live_rows(c, seg)
        dt_all.append(dt_raw if live is None else jnp.where(live, dt_raw, 0.0))
        gc_all.append(_chunk_cumsum(dt_all[-1] * (-jnp.exp(pa[0:1])), c))
        gct.append(_rows_as_lanes(gc_all[-1], c))
        eg_all.append(jnp.exp(gc_all[-1]))
    tt = lax.broadcasted_iota(jnp.int32, (c, c), 0)
    ss = lax.broadcasted_iota(jnp.int32, (c, c), 1)
    incl = tt >= ss
    lane_head = lax.broadcasted_iota(jnp.int32, (c, gw), 1) >> shift
    lane_head1 = lax.broadcasted_iota(jnp.int32, (1, gw), 1) >> shift
    row_head = lax.broadcasted_iota(jnp.int32, (gw, ns), 0) >> shift
    gs = range(ng)
    units = [(s, g) for s in range(len(segs)) for g in gs]
    us = range(len(units))

    bg = [cv[:, inner + g * ns:inner + (g + 1) * ns] for g in gs]
    cg = [cv[:, inner + (ng + g) * ns:inner + (ng + g + 1) * ns] for g in gs]
    xg = [cv[:, g * gw:(g + 1) * gw] for g in gs]
    hg = [h_scr[s, g] for s, g in units]
    cb = [_bdot_nt(cg[g], bg[g]) for g in gs]
    ch = [_bdot_nt(cg[g], hg[u]) for u, (s, g) in enumerate(units)]
    xdt, xkd, yg, hdec = [], [], [], []
    for u, (s, g) in enumerate(units):
        dtb = jnp.zeros((c, gw), F32)
        egb = jnp.zeros((c, gw), F32)
        kdb = jnp.zeros((c, gw), F32)
        skip = jnp.zeros((1, gw), F32)
        hd = jnp.zeros((gw, ns), F32)
        for r in range(nr):
            col = off + g * nr + r
            gcol = gc_all[s][:, col:col + 1]
            glast = gc_all[s][c - 1:c, col:col + 1]
            seg = lane_head == r
            dtb = jnp.where(seg, dt_all[s][:, col:col + 1], dtb)
            egb = jnp.where(seg, eg_all[s][:, col:col + 1], egb)
            kdb = jnp.where(seg, jnp.exp(glast - gcol), kdb)
            skip = jnp.where(lane_head1 == r, pa[2:3, col:col + 1], skip)
            hd = jnp.where(row_head == r, jnp.exp(glast), hd)
        xdt.append(xg[g] * dtb)
        xkd.append(xdt[u] * kdb)
        yg.append(ch[u] * egb + skip * xg[g])
        hdec.append(hd)
    for r in range(nr):
        part = []
        for u, (s, g) in enumerate(units):
            col = off + g * nr + r
            lm = jnp.exp(jnp.where(incl, gc_all[s][:, col:col + 1] - gct[s][col:col + 1, 0:c], NEG))
            part.append(_bdot(cb[g] * lm, jnp.where(lane_head == r, xdt[u], 0.0)))
        yg = [yg[u] + part[u] for u in us]
    hn = [hdec[u] * hg[u] + _bdot_tn(xkd[u], bg[g]) for u, (s, g) in enumerate(units)]
    for u, (s, g) in enumerate(units):
        h_scr[s, g] = hn[u]
    for g in gs:
        yv = _by_sequence([yg[s * ng + g] for s in range(len(segs))], segs, c, gw)
        y_scr[:, g * gw:(g + 1) * gw] = yv * _silu(z_ref[:, g * gw:(g + 1) * gw])

    y = y_scr[...]
    y = y * lax.rsqrt(jnp.mean(y * y, axis=-1, keepdims=True) + EPS) * ng_ref[...]
    y_ref[...] = y.astype(y_ref.dtype)

    @pl.when(ci == pl.num_programs(1) - 1)
    def _():
        ht_ref[...] = h_scr[...]


def ssd(seg_x, seg_z, seg_d, row0, nb, nc, c, segs, conv0, conv_w, conv_b, h0, pa, ngain, ng, nr, hp, ns, off):
    inner = ng * nr * hp
    sconv = inner + 2 * ng * ns
    r0 = row0 // c
    nseq = len(segs)
    glob = lambda s, i: (r0 + s * nc + i, 0)
    st = pl.BlockSpec((nseq, ng, nr * hp, ns), lambda s, i: (s, 0, 0, 0))
    scratch = [pltpu.VMEM((nseq, ng, nr * hp, ns), F32), pltpu.VMEM((c + SUBLANES, sconv), F32),
               pltpu.VMEM((c, sconv), F32), pltpu.VMEM((c, inner), F32)]
    if nseq > 1:
        scratch.append(pltpu.VMEM((c + SUBLANES, sconv), F32))
    return pl.pallas_call(
        functools.partial(_ssd_body, c=c, segs=segs, ng=ng, nr=nr, hp=hp, ns=ns, off=off), grid=(nb, nc),
        in_specs=[pl.BlockSpec((c, sconv), glob),
                  pl.BlockSpec((c, inner), glob),
                  pl.BlockSpec((c, LANES), glob),
                  pl.BlockSpec((nseq, CONV_W - 1, sconv), lambda s, i: (s, 0, 0)),
                  pl.BlockSpec((CONV_W, sconv), lambda s, i: (0, 0)),
                  pl.BlockSpec((1, sconv), lambda s, i: (0, 0)),
                  st,
                  pl.BlockSpec((SUBLANES, LANES), lambda s, i: (0, 0)),
                  pl.BlockSpec((1, inner), lambda s, i: (0, 0))],
        out_specs=[pl.BlockSpec((c, inner), lambda s, i: (s * nc + i, 0)), st],
        out_shape=[jax.ShapeDtypeStruct((nb * nc * c, inner), BF16),
                   jax.ShapeDtypeStruct((nb * nseq, ng, nr * hp, ns), F32)],
        scratch_shapes=scratch,
        compiler_params=_params(("parallel", "arbitrary"), 40), name="ssd",
    )(seg_x, seg_z, seg_d, conv0, conv_w, conv_b.reshape(1, sconv), h0, pa, ngain.reshape(1, inner))


def _ca_body(q_ref, k_ref, v_ref, o_ref, *, nh, dh, nseq, ls):
    scale = dh ** -0.5
    rows = q_ref.shape[0]
    cols = [slice(h * dh, (h + 1) * dh) for h in range(nh)]
    if len(k_ref.shape) == 4:
        mem = k_ref.shape[1]
        qs = jnp.concatenate([q_ref[:, cols[h]] for h in range(nh)], axis=0)
        col_head = lax.rem(lax.broadcasted_iota(jnp.int32, (nh * rows, mem * nh), 1), nh)
        row_head = lax.div(lax.broadcasted_iota(jnp.int32, (nh * rows, mem * nh), 0), rows)
        own = col_head == row_head
        o = []
        for b in range(nseq):
            s = jnp.where(own, _bdot_nt(qs, k_ref[b].reshape(mem * nh, dh)) * scale, NEG)
            e = jnp.exp(s - jnp.max(s, axis=-1, keepdims=True))
            p = e / jnp.sum(e, axis=-1, keepdims=True)
            ob = _bdot(p, v_ref[b].reshape(mem * nh, dh))
            o += [ob[h * rows:(h + 1) * rows] for h in range(nh)]
    else:
        units = [(b, h) for b in range(nseq) for h in range(nh)]
        s = [_bdot_nt(q_ref[:, cols[h]], k_ref[b, :, cols[h]]) * scale for b, h in units]
        e = [jnp.exp(x - jnp.max(x, axis=-1, keepdims=True)) for x in s]
        p = [x / jnp.sum(x, axis=-1, keepdims=True) for x in e]
        o = [_bdot(p[i], v_ref[b, :, cols[h]]) for i, (b, h) in enumerate(units)]
    row = lax.broadcasted_iota(jnp.int32, (rows, dh), 0)
    for h in range(nh):
        val = o[(nseq - 1) * nh + h]
        for b in range(nseq - 2, -1, -1):
            val = jnp.where(row < (b + 1) * ls, o[b * nh + h], val)
        o_ref[:, cols[h]] = val.astype(o_ref.dtype)


def cross_attention(q2d, row0, nrows, tl, nseq, mem_k, mem_v, nh, dh):
    r0 = row0 // tl
    wd = nh * dh
    per = nrows // tl * nseq // mem_k.shape[0]
    if mem_k.ndim == 4:
        kv = pl.BlockSpec((nseq,) + mem_k.shape[1:], lambda i: (i // per, 0, 0, 0))
    else:
        kv = pl.BlockSpec((nseq,) + mem_k.shape[1:], lambda i: (i // per, 0, 0))
    return pl.pallas_call(
        functools.partial(_ca_body, nh=nh, dh=dh, nseq=nseq, ls=tl // nseq), grid=(nrows // tl,),
        in_specs=[pl.BlockSpec((tl, wd), lambda i: (r0 + i, 0)), kv, kv],
        out_specs=pl.BlockSpec((tl, wd), lambda i: (i, 0)),
        out_shape=jax.ShapeDtypeStruct((nrows, wd), BF16),
        compiler_params=_params(("parallel",), 32), name="cross_attention",
    )(q2d, mem_k, mem_v)


def _extract_top(x, n):
    out = []
    for it in range(n):
        m = jnp.max(x, axis=0, keepdims=True)
        out.append(m)
        if it + 1 < n:
            x = jnp.where(x == m, -jnp.inf, x)
    return out


def _pair_candidates(v1, v2, topk):
    v1s = jnp.concatenate(v1, axis=0)
    v2s = jnp.concatenate(v2, axis=0)
    pieces = []
    a = 0
    while a < topk:
        nb = topk // (a + 1)
        if nb == 1 and a % SUBLANES == 0:
            pieces.append(v1s[a:] + v2[0])
            break
        pieces.append(v1[a] + v2s[:min(-(-nb // SUBLANES) * SUBLANES, topk)])
        a += 1
    return jnp.concatenate(pieces, axis=0)


def _peer_route_body(q_ref, keys_ref, s1_ref, e1_ref, s2_ref, e2_ref, thr_ref, *, nh, dq, topk):
    for h in range(nh):
        s1 = _bdot_nt(keys_ref[2 * h], q_ref[:, (2 * h) * dq:(2 * h + 1) * dq])
        s2 = _bdot_nt(keys_ref[2 * h + 1], q_ref[:, (2 * h + 1) * dq:(2 * h + 2) * dq])
        v1 = _extract_top(s1, topk)
        v2 = _extract_top(s2, topk)
        tops = _extract_top(_pair_candidates(v1, v2, topk), topk)
        zsum = jnp.exp(tops[0] - tops[0])
        for a in range(1, topk):
            zsum = zsum + jnp.exp(tops[a] - tops[0])
        s1_ref[h] = s1
        s2_ref[h] = s2
        e1_ref[h] = jnp.exp(s1 - v1[0]) / zsum
        e2_ref[h] = jnp.exp(s2 - v2[0])
        thr_ref[h:h + 1, :] = tops[topk - 1]


def peer_route(q, keys, nh, nk, dq, tm=256):
    m = q.shape[0]
    tm = _tile(m, tm)
    big = pl.BlockSpec((nh, nk, tm), lambda i: (0, 0, i))
    shape = jax.ShapeDtypeStruct((nh, nk, m), F32)
    return pl.pallas_call(
        functools.partial(_peer_route_body, nh=nh, dq=dq, topk=PEER_TOPK), grid=(m // tm,),
        in_specs=[pl.BlockSpec((tm, 2 * nh * dq), lambda i: (i, 0)),
                  pl.BlockSpec((2 * nh, nk, dq), lambda i: (0, 0, 0))],
        out_specs=[big, big, big, big, pl.BlockSpec((nh, tm), lambda i: (0, i))],
        out_shape=[shape, shape, shape, shape, jax.ShapeDtypeStruct((nh, m), F32)],
        compiler_params=_params(("parallel",), 32), name="peer_route",
    )(q, keys)


def _peer_body(hft_ref, u_ref, vt_ref, s1_ref, e1_ref, s2_ref, e2_ref, thr_ref, ot_ref, st0, st1, act0, act1,
               *, nh, nk, nrow, dchunk):
    ei = pl.program_id(1)
    d, tm = hft_ref.shape
    ne = nk // nrow

    @pl.when(ei == 0)
    def _():
        ot_ref[...] = jnp.zeros_like(ot_ref)

    pieces = nrow
    prow = nrow * nk // pieces

    def step(a, do_scores=True, do_gates=True, do_values=True):
        st_new, st_old = (st0, st1) if a == 0 else (st1, st0)
        act_old, act_new = (act0, act1) if a == 0 else (act1, act0)

        def scores(p):
            rows = slice(p * prow, (p + 1) * prow)
            st_new[rows, :] = jnp.dot(u_ref[rows, :], hft_ref[...], preferred_element_type=F32)

        def apply_values(j):
            ds = slice(j * dchunk, (j + 1) * dchunk)
            ot_ref[ds, :] += jnp.dot(vt_ref[ds, :], act_old[...], preferred_element_type=F32)

        def gate_block(r, tb):
            ls = slice(tb * LANES, (tb + 1) * LANES)
            rows = slice(r * nk, (r + 1) * nk)
            gate = jnp.zeros((nk, LANES), F32)
            for h in range(nh):
                hit = (s2_ref[h, :, ls] + s1_ref[r, h:h + 1, ls]) >= thr_ref[h:h + 1, ls]
                gate = gate + jnp.where(hit, e2_ref[h, :, ls] * e1_ref[r, h:h + 1, ls], 0.0)
            act_new[rows, ls] = (_gelu_tanh(st_old[rows, ls]) * gate).astype(BF16)

        blocks = [(r, tb) for r in range(nrow) for tb in range(tm // LANES)]
        nd = d // dchunk
        per = -(-len(blocks) // nd)
        every = max(nd // pieces, 1)
        for j in range(max(nd, pieces * every)):
            if do_scores and j % every == 0 and j // every < pieces:
                scores(j // every)
            if do_values and j < nd:
                apply_values(j)
            if do_gates:
                for r, tb in blocks[j * per:(j + 1) * per]:
                    gate_block(r, tb)

    steady = jnp.logical_and(ei >= 2, ei < ne)
    for a in range(2):
        pl.when(jnp.logical_and(steady, ei % 2 == a))(functools.partial(step, a))
    for e in sorted({0, 1, ne, ne + 1}):
        if not 2 <= e < ne:
            pl.when(ei == e)(functools.partial(step, e % 2, e < ne, 1 <= e <= ne, e >= 2))


def peer_experts(hft, u, vt, s1g, e1g, s2, e2, thr, nh, nk, tm=512, nrow=4):
    d, m = hft.shape
    tm = _tile(m, tm)
    et = nrow * nk
    once = pl.Buffered(1)
    ne = nk // nrow
    tile = lambda e, lag: jnp.clip(e - lag, 0, ne - 1)
    return pl.pallas_call(
        functools.partial(_peer_body, nh=nh, nk=nk, nrow=nrow, dchunk=_tile(d, 512)), grid=(m // tm, ne + 2),
        in_specs=[pl.BlockSpec((d, tm), lambda t, e: (0, t), pipeline_mode=once),
                  pl.BlockSpec((et, d), lambda t, e: (tile(e, 0), 0)),
                  pl.BlockSpec((d, et), lambda t, e: (0, tile(e, 2))),
                  pl.BlockSpec((nrow, nh, tm), lambda t, e: (tile(e, 1), 0, t)),
                  pl.BlockSpec((nrow, nh, tm), lambda t, e: (tile(e, 1), 0, t)),
                  pl.BlockSpec((nh, nk, tm), lambda t, e: (0, 0, t), pipeline_mode=once),
                  pl.BlockSpec((nh, nk, tm), lambda t, e: (0, 0, t), pipeline_mode=once),
                  pl.BlockSpec((nh, tm), lambda t, e: (0, t))],
        out_specs=pl.BlockSpec((d, tm), lambda t, e: (0, t)),
        out_shape=jax.ShapeDtypeStruct((d, m), F32),
        scratch_shapes=[pltpu.VMEM((et, tm), F32), pltpu.VMEM((et, tm), F32),
                        pltpu.VMEM((et, tm), BF16), pltpu.VMEM((et, tm), BF16)],
        compiler_params=_params(("parallel", "arbitrary"), 56), name="peer_experts",
    )(hft, u, vt, s1g, e1g, s2, e2, thr)


def _param_tile(pieces):
    out = jnp.zeros((SUBLANES, LANES), F32)
    for row, off, vec in pieces:
        out = out.at[row, off:off + vec.shape[0]].set(vec.astype(F32))
    return out


def kernel(x_prompt, x_sample, cache_mem_k, cache_mem_v, state_dn_conv, state_dn_rec, state_ssm_conv,
           state_ssm_rec, mem_prompt, norm_mix_g, w_in, dn_conv_w, dn_a_log, dn_dt_bias, dn_norm_g, dn_w_out,
           ssm_conv_w, ssm_conv_b, ssm_a_log, ssm_dt_bias, ssm_d, ssm_norm_g, ssm_w_out, w_o, norm_ca_g,
           ca_w_q, ca_w_k, ca_w_v, ca_w_o, norm_ffn_g, peer_w_q, peer_keys, peer_u, peer_v, final_norm_g):
    depth = w_in.shape[0]
    bp, lp, d = x_prompt.shape
    bs, ls, _ = x_sample.shape
    _, _, dnh, dk, _ = state_dn_rec.shape
    _, _, sh, hp, ns = state_ssm_rec.shape
    dnw = dnh * dk
    inner = sh * hp
    sconv = state_ssm_conv.shape[-1]
    sg = (sconv - inner) // (2 * ns)
    sr = sh // sg
    _, _, mem, cah, cad = cache_mem_k.shape
    caw = cah * cad
    _, pnh, _, nk, dq = peer_keys.shape
    packed = SUBLANES % ls == 0 and ls >= CONV_W - 1 and bs % (SUBLANES // ls) == 0
    lrow = ls if packed else -(-ls // SUBLANES) * SUBLANES
    cs = SUBLANES if packed else lrow
    nseq = cs // lrow
    s_segs = tuple((j * lrow, j * lrow + ls) for j in range(nseq))
    assert lp % CHUNK == 0 and cs <= CHUNK and dk == LANES and ns == LANES and nk == LANES and dq == LANES
    assert 2 * dnh + sh <= LANES and hp & (hp - 1) == 0
    mp, ms = bp * lp, bs * lrow
    ssm_off = 2 * dnh

    x = [x_prompt.reshape(mp, d), jnp.pad(x_sample, ((0, 0), (0, lrow - ls), (0, 0))).reshape(ms, d)]

    outs = [[] for _ in range(10)]
    for l in range(depth):
        wt = w_in[l].T
        o1 = 3 * dnw + dnw
        o2 = o1 + 2 * dnh
        o3 = o2 + inner
        o4 = o3 + sconv
        o5 = o4 + sh
        w_d = jnp.concatenate([wt[o1:o2], wt[o4:o5], jnp.zeros((LANES - 2 * dnh - sh, d), F32)], axis=0)

        hn = rmsnorm(x, norm_mix_g[l], BF16)
        def proj(start, n):
            if start % SUBLANES:
                return matmul(hn, wt[start:start + n].astype(BF16), w_rows_out=True)
            return matmul(hn, wt, w_rows_out=True, w_rows=(start, n))

        seg_a = proj(0, o1)
        seg_z = proj(o2, o3 - o2)
        seg_x = proj(o3, o4 - o3)
        seg_c = proj(o5, 2 * d)
        seg_d = matmul(hn, w_d, w_rows_out=True, w_rows=(0, LANES))

        dn_pa = _param_tile([(0, dnh, dn_a_log[l]), (1, dnh, dn_dt_bias[l])])
        ssm_pa = _param_tile([(0, ssm_off, ssm_a_log[l]), (1, ssm_off, ssm_dt_bias[l]), (2, ssm_off, ssm_d[l])])
        groups = (
            (0, bp, lp // CHUNK, CHUNK, ((0, CHUNK),), jnp.zeros((bp, CONV_W - 1, 3 * dnw), F32),
             jnp.zeros((bp, dnh, dk, dk), F32), jnp.zeros((bp, CONV_W - 1, sconv), F32),
             jnp.zeros((bp, sg, sr * hp, ns), F32)),
            (mp, bs // nseq, 1, cs, s_segs, state_dn_conv[l], state_dn_rec[l],
             state_ssm_conv[l], state_ssm_rec[l].reshape(bs, sg, sr * hp, ns)),
        )
        o_dn, o_ssm, dn_h, ssm_h = [], [], [], []
        for row0, nb, nc, c, segs, dc0, dh0, sc0, sh0 in groups:
            od, hd = deltanet(seg_a, seg_d, row0, nb, nc, c, segs, dc0, dn_conv_w[l], dh0, dn_pa, dn_norm_g[l],
                              dnh, dk)
            osd, hsd = ssd(seg_x, seg_z, seg_d, row0, nb, nc, c, segs, sc0, ssm_conv_w[l], ssm_conv_b[l], sh0, ssm_pa,
                           ssm_norm_g[l], sg, sr, hp, ns, ssm_off)
            o_dn.append(od)
            o_ssm.append(osd)
            dn_h.append(hd)
            ssm_h.append(hsd.reshape(-1, sh, hp, ns))

        mixed = gated_mix(o_dn, dn_w_out[l].astype(BF16), o_ssm, ssm_w_out[l].astype(BF16), seg_c)
        x1 = matmul(mixed, w_o[l].astype(BF16), res=x)

        hc = rmsnorm(x1, norm_ca_g[l], BF16)
        qc = matmul(hc, ca_w_q[l].astype(BF16))
        memp = mem_prompt.reshape(bp * mem, d)
        mk = matmul(memp, ca_w_k[l].astype(BF16))
        mv = matmul(memp, ca_w_v[l].astype(BF16))
        oc = [cross_attention(qc, 0, mp, _tile(lp, 512), 1, mk.reshape(bp, mem, caw), mv.reshape(bp, mem, caw),
                              cah, cad),
              cross_attention(qc, mp, ms, cs, nseq, cache_mem_k[l], cache_mem_v[l], cah, cad)]
        x2 = matmul(oc, ca_w_o[l].astype(BF16), res=x1)

        hf, hft = rmsnorm(x2, norm_ffn_g[l], BF16, with_transpose=True)
        pq = matmul(hf, peer_w_q[l].astype(BF16))
        s1, e1, s2, e2, thr = peer_route(pq, peer_keys[l].reshape(2 * pnh, nk, dq).astype(BF16), pnh, nk, dq)
        pot = peer_experts(hft, peer_u[l].astype(BF16), transpose_cast(peer_v[l], BF16),
                           s1.transpose(1, 0, 2), e1.transpose(1, 0, 2), s2, e2, thr, pnh, nk)
        if l == depth - 1:
            y_prompt = add_rows(x2, pot, final_norm_g, 0, mp).reshape(bp, lp, d)
            y_sample = add_rows(x2, pot, final_norm_g, mp, ms).reshape(bs, lrow, d)[:, :ls]
        else:
            x = add_rows(x2, pot, None, 0, mp + ms)

        def tail(seg, row0, nb, sl, nvalid, lo, hi, state0):
            take = min(CONV_W - 1, nvalid)
            rows = row0 + np.arange(nb)[:, None] * sl + np.arange(nvalid - take, nvalid)[None, :]
            u = jnp.take(seg, rows.reshape(-1), axis=0)[:, lo:hi].reshape(nb, take, hi - lo)
            if take < CONV_W - 1:
                u = jnp.concatenate([state0[:, take:], u], axis=1)
            return u

        outs[0].append(mk.reshape(bp, mem, cah, cad))
        outs[1].append(mv.reshape(bp, mem, cah, cad))
        outs[2].append(tail(seg_a, 0, bp, lp, lp, 0, 3 * dnw, groups[0][5]))
        outs[3].append(dn_h[0])
        outs[4].append(tail(seg_x, 0, bp, lp, lp, 0, sconv, groups[0][7]))
        outs[5].append(ssm_h[0])
        outs[6].append(tail(seg_a, mp, bs, lrow, ls, 0, 3 * dnw, state_dn_conv[l]))
        outs[7].append(dn_h[1])
        outs[8].append(tail(seg_x, mp, bs, lrow, ls, 0, sconv, state_ssm_conv[l]))
        outs[9].append(ssm_h[1])

    return (y_prompt, y_sample) + tuple(jnp.stack(o) for o in outs)
```

```python
import functools
import math

import jax
import jax.numpy as jnp
import numpy as np
from jax import lax
from jax.experimental import pallas as pl
from jax.experimental.pallas import tpu as pltpu

F32 = jnp.float32
BF16 = jnp.bfloat16

EPS = 1e-6
CONV_W = 4
CHUNK = 64
PEER_TOPK = 16
LANES = 128
SUBLANES = 8
NEG = -1e30
MIB = 1 << 20


def _params(sem, vmem_mib):
    return pltpu.CompilerParams(dimension_semantics=sem, vmem_limit_bytes=vmem_mib * MIB)


def _tile(n, pref):
    t = min(pref, n)
    while n % t:
        t //= 2
    return t


def _sigmoid(x):
    return 1.0 / (1.0 + jnp.exp(-x))


def _silu(x):
    return x * _sigmoid(x)


def _softplus(x):
    return jnp.maximum(x, 0.0) + jnp.log1p(jnp.exp(-jnp.abs(x)))


def _gelu_tanh(x):
    return x * (0.5 * (1.0 + jnp.tanh(0.7978845608028654 * (x + 0.044715 * (x * x * x)))))


def _bdot(a, b):
    return jnp.dot(a.astype(BF16), b.astype(BF16), preferred_element_type=F32)


def _bdot_nt(a, b):
    return lax.dot_general(a.astype(BF16), b.astype(BF16), (((1,), (1,)), ((), ())),
                           preferred_element_type=F32)


def _bdot_tn(a, b):
    return lax.dot_general(a.astype(BF16), b.astype(BF16), (((0,), (0,)), ((), ())),
                           preferred_element_type=F32)


def _as_pieces(a):
    return list(a) if isinstance(a, (list, tuple)) else [a]


def _piece_tile(pieces, pref):
    return _tile(math.gcd(*[p.shape[0] for p in pieces]), pref)


def _piece_ends(pieces, tm):
    ends, tot = [], 0
    for p in pieces:
        tot += p.shape[0] // tm
        ends.append(tot)
    return tuple(ends)


def _piece_specs(pieces, tm, cols, row_of, col_of):
    specs, start = [], 0
    for p in pieces:
        n = p.shape[0] // tm
        specs.append(pl.BlockSpec(
            (tm, cols), lambda *g, s=start, n=n: (jnp.clip(row_of(*g) - s, 0, n - 1), col_of(*g))))
        start += n
    return specs


def _piece_value(refs, tile, ends):
    val = refs[-1][...]
    for k in range(len(refs) - 2, -1, -1):
        val = jnp.where(tile < ends[k], refs[k][...], val)
    return val


def _rmsnorm_body(*refs, nx, ends, both):
    x = _piece_value(refs[:nx], pl.program_id(0), ends)
    y = x * lax.rsqrt(jnp.mean(x * x, axis=-1, keepdims=True) + EPS) * refs[nx][...]
    refs[nx + 1][...] = y.astype(refs[nx + 1].dtype)
    if both:
        refs[nx + 2][...] = y.T.astype(refs[nx + 2].dtype)


def _addnorm_body(x_ref, pt_ref, g_ref, o_ref):
    x = x_ref[...] + pt_ref[...].T
    y = x * lax.rsqrt(jnp.mean(x * x, axis=-1, keepdims=True) + EPS)
    o_ref[...] = (y * g_ref[...]).astype(o_ref.dtype)


def _add_body(x_ref, pt_ref, o_ref):
    o_ref[...] = x_ref[...] + pt_ref[...].T


def rmsnorm(x, g, out_dtype, tm=256, with_transpose=False):
    xs = _as_pieces(x)
    d = xs[0].shape[1]
    m = sum(p.shape[0] for p in xs)
    tm = _piece_tile(xs, tm)
    row = pl.BlockSpec((tm, d), lambda i: (i, 0))
    outs, shapes = [row], [jax.ShapeDtypeStruct((m, d), out_dtype)]
    if with_transpose:
        outs.append(pl.BlockSpec((d, tm), lambda i: (0, i)))
        shapes.append(jax.ShapeDtypeStruct((d, m), out_dtype))
    res = pl.pallas_call(
        functools.partial(_rmsnorm_body, nx=len(xs), ends=_piece_ends(xs, tm), both=with_transpose),
        grid=(m // tm,),
        in_specs=_piece_specs(xs, tm, d, lambda i: i, lambda i: 0) + [pl.BlockSpec((1, d), lambda i: (0, 0))],
        out_specs=outs, out_shape=shapes,
        compiler_params=_params(("parallel",), 40), name="rmsnorm",
    )(*xs, g.reshape(1, d))
    return res if with_transpose else res[0]


def add_rows(x, pt, g, row0, nrows, tm=256):
    d = x.shape[1]
    tm = _tile(math.gcd(row0, nrows) if row0 else nrows, tm)
    r0 = row0 // tm
    ins = [pl.BlockSpec((tm, d), lambda i: (r0 + i, 0)), pl.BlockSpec((d, tm), lambda i: (0, r0 + i))]
    if g is None:
        body, args = _add_body, (x, pt)
    else:
        body, args = _addnorm_body, (x, pt, g.reshape(1, d))
        ins.append(pl.BlockSpec((1, d), lambda i: (0, 0)))
    return pl.pallas_call(
        body, grid=(nrows // tm,), in_specs=ins, out_specs=pl.BlockSpec((tm, d), lambda i: (i, 0)),
        out_shape=jax.ShapeDtypeStruct((nrows, d), F32),
        compiler_params=_params(("parallel",), 40), name="add_norm",
    )(*args)


def _transpose_cast_body(x_ref, o_ref):
    o_ref[...] = x_ref[...].T.astype(o_ref.dtype)


def transpose_cast(x, dtype, tr=512, tc=1024):
    r, c = x.shape
    tr, tc = _tile(r, tr), _tile(c, tc)
    return pl.pallas_call(
        _transpose_cast_body, grid=(r // tr, c // tc),
        in_specs=[pl.BlockSpec((tr, tc), lambda i, j: (i, j))],
        out_specs=pl.BlockSpec((tc, tr), lambda i, j: (j, i)),
        out_shape=jax.ShapeDtypeStruct((c, r), dtype),
        compiler_params=_params(("parallel", "parallel"), 32), name="transpose_cast",
    )(x)


def _mm_body(*refs, na, nr, a_ends, r_ends, w_rows_out, cast_w):
    w_ref = refs[na]
    i = pl.program_id(1)
    if cast_w:
        o_ref, w_bf = refs[-2], refs[-1]

        @pl.when(i == 0)
        def _():
            w_bf[...] = w_ref[...].astype(BF16)

        w = w_bf[...]
    else:
        o_ref, w = refs[-1], w_ref[...]
    a = _piece_value(refs[:na], i, a_ends).astype(BF16)
    acc = _bdot_nt(a, w) if w_rows_out else jnp.dot(a, w, preferred_element_type=F32)
    if nr:
        acc = _piece_value(refs[na + 1:na + 1 + nr], i, r_ends) + acc
    o_ref[...] = acc.astype(o_ref.dtype)


def matmul(a, w, res=None, out_dtype=F32, tm=512, tn=1024, w_rows_out=False, w_rows=None):
    a_p = _as_pieces(a)
    r_p = _as_pieces(res) if res is not None else []
    k = a_p[0].shape[1]
    m = sum(p.shape[0] for p in a_p)
    n = w_rows[1] if w_rows else w.shape[0] if w_rows_out else w.shape[1]
    tm, tn = _piece_tile(a_p + r_p, tm), _tile(n, tn)
    row_of, zero = (lambda j, i: i), (lambda j, i: 0)
    scratch = []
    if w_rows:
        assert w_rows_out and w_rows[0] % SUBLANES == 0
        w_spec = pl.BlockSpec((pl.Element(tn), pl.Element(k)),
                              lambda j, i: (pl.multiple_of(w_rows[0] + j * tn, SUBLANES), 0))
        scratch = [pltpu.VMEM((tn, k), BF16)]
    elif w_rows_out:
        w_spec = pl.BlockSpec((tn, k), lambda j, i: (j, 0))
    else:
        w_spec = pl.BlockSpec((k, tn), lambda j, i: (0, j))
    ins = _piece_specs(a_p, tm, k, row_of, zero) + [w_spec] + _piece_specs(r_p, tm, tn, row_of, lambda j, i: j)
    return pl.pallas_call(
        functools.partial(_mm_body, na=len(a_p), nr=len(r_p), a_ends=_piece_ends(a_p, tm),
                          r_ends=_piece_ends(r_p, tm), w_rows_out=w_rows_out, cast_w=bool(w_rows)),
        grid=(n // tn, m // tm),
        in_specs=ins, out_specs=pl.BlockSpec((tm, tn), lambda j, i: (i, j)),
        out_shape=jax.ShapeDtypeStruct((m, n), out_dtype), scratch_shapes=scratch,
        compiler_params=_params(("parallel", "arbitrary" if w_rows else "parallel"), 58 if w_rows else 48),
        name="matmul",
    )(*a_p, w, *r_p)


def _mix_body(*refs, n1, n2, ends1, ends2):
    w1_ref, w2_ref = refs[n1], refs[n1 + 1 + n2]
    g1_ref, g2_ref, o_ref = refs[-3], refs[-2], refs[-1]
    i = pl.program_id(1)
    y1 = jnp.dot(_piece_value(refs[:n1], i, ends1), w1_ref[...], preferred_element_type=F32)
    y2 = jnp.dot(_piece_value(refs[n1 + 1:n1 + 1 + n2], i, ends2), w2_ref[...], preferred_element_type=F32)
    o_ref[...] = (_sigmoid(g1_ref[...]) * y1 + _sigmoid(g2_ref[...]) * y2).astype(o_ref.dtype)


def gated_mix(a1, w1, a2, w2, gates, tm=512, tn=1024):
    p1, p2 = _as_pieces(a1), _as_pieces(a2)
    k1, k2 = p1[0].shape[1], p2[0].shape[1]
    m, n = gates.shape[0], w1.shape[1]
    tm, tn = _piece_tile(p1 + p2, tm), _tile(n, tn)
    nj = n // tn
    row_of, zero = (lambda j, i: i), (lambda j, i: 0)
    return pl.pallas_call(
        functools.partial(_mix_body, n1=len(p1), n2=len(p2), ends1=_piece_ends(p1, tm), ends2=_piece_ends(p2, tm)),
        grid=(nj, m // tm),
        in_specs=(_piece_specs(p1, tm, k1, row_of, zero) + [pl.BlockSpec((k1, tn), lambda j, i: (0, j))]
                  + _piece_specs(p2, tm, k2, row_of, zero) + [pl.BlockSpec((k2, tn), lambda j, i: (0, j))]
                  + [pl.BlockSpec((tm, tn), lambda j, i: (i, j)), pl.BlockSpec((tm, tn), lambda j, i: (i, j + nj))]),
        out_specs=pl.BlockSpec((tm, tn), lambda j, i: (i, j)),
        out_shape=jax.ShapeDtypeStruct((m, n), BF16),
        compiler_params=_params(("parallel", "parallel"), 48), name="gated_mix",
    )(*p1, w1, *p2, w2, gates, gates)


def _conv_silu_rows(x_ref, st_ref, w_ref, b_ref, buf, cv, c, ci):
    k = CONV_W - 1

    @pl.when(ci == 0)
    def _():
        buf[SUBLANES - k:SUBLANES, :] = st_ref[0]

    buf[SUBLANES:SUBLANES + c, :] = x_ref[...]
    base = SUBLANES - k
    acc = buf[base:base + c, :] * w_ref[0:1, :]
    for j in range(1, CONV_W):
        acc = acc + buf[base + j:base + j + c, :] * w_ref[j:j + 1, :]
    if b_ref is not None:
        acc = acc + b_ref[...]
    cv[...] = _silu(acc)
    buf[0:SUBLANES, :] = buf[c:c + SUBLANES, :]


def _conv_silu_packed(x_ref, st_ref, w_ref, b_ref, buf, ov, cv, c, ls):
    k = CONV_W - 1
    ov[...] = jnp.zeros_like(ov)
    for j in range(c // ls):
        ov[SUBLANES + j * ls - k:SUBLANES + j * ls, :] = st_ref[j]
    buf[0:SUBLANES, :] = jnp.zeros((SUBLANES, buf.shape[1]), F32)
    buf[SUBLANES:SUBLANES + c, :] = x_ref[...]
    base = SUBLANES - k
    in_seq = lax.broadcasted_iota(jnp.int32, (c, buf.shape[1]), 0) & (ls - 1)
    acc = None
    for j in range(CONV_W):
        win = buf[base + j:base + j + c, :]
        if j < k:
            win = jnp.where(in_seq + j < k, ov[base + j:base + j + c, :], win)
        term = win * w_ref[j:j + 1, :]
        acc = term if acc is None else acc + term
    if b_ref is not None:
        acc = acc + b_ref[...]
    cv[...] = _silu(acc)


def _chunk_cumsum(g, c):
    tt = lax.broadcasted_iota(jnp.int32, (c, c), 0)
    ss = lax.broadcasted_iota(jnp.int32, (c, c), 1)
    tril = jnp.where(tt >= ss, 1.0, 0.0).astype(BF16)
    hi = g.astype(BF16)
    r1 = g - hi.astype(F32)
    mid = r1.astype(BF16)
    lo = (r1 - mid.astype(F32)).astype(BF16)
    dot = functools.partial(jnp.dot, preferred_element_type=F32)
    return dot(tril, hi) + dot(tril, mid) + dot(tril, lo)


def _rows_as_lanes(x, c):
    if c < LANES:
        x = jnp.concatenate([x, jnp.zeros((LANES - c, LANES), F32)], axis=0)
    return x.T


def _live_rows(c, seg):
    lo, hi = seg
    if (lo, hi) == (0, c):
        return None
    row = lax.broadcasted_iota(jnp.int32, (c, LANES), 0)
    return jnp.logical_and(row >= lo, row < hi)


def _by_sequence(vals, segs, c, width):
    out = vals[-1]
    if len(vals) > 1:
        row = lax.broadcasted_iota(jnp.int32, (c, width), 0)
        for s in range(len(vals) - 2, -1, -1):
            out = jnp.where(row < segs[s][1], vals[s], out)
    return out


def _dn_body(x_ref, z_ref, sm_ref, cst_ref, cw_ref, h0_ref, pa_ref, ng_ref, o_ref, ht_ref, h_scr, buf, cv, *ov,
             c, segs, nh, dk, nsq):
    ci = pl.program_id(1)

    @pl.when(ci == 0)
    def _():
        h_scr[...] = h0_ref[...]

    if len(segs) > 1:
        _conv_silu_packed(x_ref, cst_ref, cw_ref, None, buf, ov[0], cv, c, segs[0][1] - segs[0][0])
    else:
        _conv_silu_rows(x_ref, cst_ref, cw_ref, None, buf, cv, c, ci)
    w = nh * dk
    sm = sm_ref[...]
    pa = pa_ref[...]
    beta_raw = _sigmoid(sm)
    g_raw = -jnp.exp(pa[0:1]) * _softplus(sm + pa[1:2])
    beta_all, gc_all, gct, eg_all = [], [], [], []
    for seg in segs:
        live = _live_rows(c, seg)
        beta_all.append(beta_raw if live is None else jnp.where(live, beta_raw, 0.0))
        gc_all.append(_chunk_cumsum(g_raw if live is None else jnp.where(live, g_raw, 0.0), c))
        gct.append(_rows_as_lanes(gc_all[-1], c))
        eg_all.append(jnp.exp(gc_all[-1]))
    tt = lax.broadcasted_iota(jnp.int32, (c, c), 0)
    ss = lax.broadcasted_iota(jnp.int32, (c, c), 1)
    strict = tt > ss
    incl = tt >= ss
    scale = dk ** -0.5
    hs = range(nh)
    units = [(s, h) for s in range(len(segs)) for h in hs]
    us = range(len(units))

    q, k = [], []
    for h in hs:
        qh = cv[:, h * dk:(h + 1) * dk]
        kh = cv[:, w + h * dk:w + (h + 1) * dk]
        q.append(qh * lax.rsqrt(jnp.sum(qh * qh, axis=-1, keepdims=True) + EPS) * scale)
        k.append(kh * lax.rsqrt(jnp.sum(kh * kh, axis=-1, keepdims=True) + EPS))
    kk = [_bdot_nt(k[h], k[h]) for h in hs]
    qk = [_bdot_nt(q[h], k[h]) for h in hs]
    hh = [h_scr[s, h] for s, h in units]
    qh0 = [_bdot(q[h], hh[u]) for u, (s, h) in enumerate(units)]
    gcol = [gc_all[s][:, nh + h:nh + h + 1] for s, h in units]
    diff = [gcol[u] - gct[s][nh + h:nh + h + 1, 0:c] for u, (s, h) in enumerate(units)]
    bcol = [beta_all[s][:, h:h + 1] for s, h in units]
    egc = [eg_all[s][:, nh + h:nh + h + 1] for s, h in units]
    glast = [gc_all[s][c - 1:c, nh + h:nh + h + 1] for s, h in units]
    qm = [-(bcol[u] * kk[h] * jnp.exp(jnp.where(strict, diff[u], NEG))) for u, (s, h) in enumerate(units)]
    pw = [_bdot(qm[u], qm[u]) for u in us] if nsq else None
    for it in range(nsq):
        t = [_bdot(qm[u], pw[u]) for u in us]
        nxt = [_bdot(pw[u], pw[u]) for u in us] if it + 1 < nsq else None
        qm = [qm[u] + pw[u] + t[u] for u in us]
        pw = nxt
    rhs = [jnp.concatenate([cv[:, 2 * w + h * dk:2 * w + (h + 1) * dk] * bcol[u], k[h] * (bcol[u] * egc[u])], axis=1)
           for u, (s, h) in enumerate(units)]
    sol = [rhs[u] + _bdot(qm[u], rhs[u]) for u in us]
    wv = [sol[u][:, :dk] - _bdot(sol[u][:, dk:], hh[u]) for u in us]
    o = [qh0[u] * egc[u] + _bdot(qk[h] * jnp.exp(jnp.where(incl, diff[u], NEG)), wv[u])
         for u, (s, h) in enumerate(units)]
    hn = [jnp.exp(glast[u]) * hh[u] + _bdot_tn(k[h] * jnp.exp(glast[u] - gcol[u]), wv[u])
          for u, (s, h) in enumerate(units)]
    for u, (s, h) in enumerate(units):
        h_scr[s, h] = hn[u]
    for h in hs:
        oh = _by_sequence([o[s * nh + h] for s in range(len(segs))], segs, c, dk)
        oh = oh * lax.rsqrt(jnp.mean(oh * oh, axis=-1, keepdims=True) + EPS) * ng_ref[...]
        o_ref[:, h * dk:(h + 1) * dk] = (oh * _silu(z_ref[:, h * dk:(h + 1) * dk])).astype(o_ref.dtype)

    @pl.when(ci == pl.num_programs(1) - 1)
    def _():
        ht_ref[...] = h_scr[...]


def _num_squarings(segs):
    n, longest = 0, max(hi - lo for lo, hi in segs)
    while (2 << n) < longest:
        n += 1
    return n


def deltanet(seg_a, seg_d, row0, nb, nc, c, segs, conv0, conv_w, h0, pa, ng, nh, dk):
    w = nh * dk
    r0 = row0 // c
    ns = len(segs)
    st = pl.BlockSpec((ns, nh, dk, dk), lambda s, i: (s, 0, 0, 0))
    scratch = [pltpu.VMEM((ns, nh, dk, dk), F32), pltpu.VMEM((c + SUBLANES, 3 * w), F32),
               pltpu.VMEM((c, 3 * w), F32)]
    if ns > 1:
        scratch.append(pltpu.VMEM((c + SUBLANES, 3 * w), F32))
    return pl.pallas_call(
        functools.partial(_dn_body, c=c, segs=segs, nh=nh, dk=dk, nsq=_num_squarings(segs)), grid=(nb, nc),
        in_specs=[pl.BlockSpec((c, 3 * w), lambda s, i: (r0 + s * nc + i, 0)),
                  pl.BlockSpec((c, w), lambda s, i: (r0 + s * nc + i, 3)),
                  pl.BlockSpec((c, LANES), lambda s, i: (r0 + s * nc + i, 0)),
                  pl.BlockSpec((ns, CONV_W - 1, 3 * w), lambda s, i: (s, 0, 0)),
                  pl.BlockSpec((CONV_W, 3 * w), lambda s, i: (0, 0)),
                  st,
                  pl.BlockSpec((SUBLANES, LANES), lambda s, i: (0, 0)),
                  pl.BlockSpec((1, dk), lambda s, i: (0, 0))],
        out_specs=[pl.BlockSpec((c, w), lambda s, i: (s * nc + i, 0)), st],
        out_shape=[jax.ShapeDtypeStruct((nb * nc * c, w), BF16),
                   jax.ShapeDtypeStruct((nb * ns, nh, dk, dk), F32)],
        scratch_shapes=scratch,
        compiler_params=_params(("parallel", "arbitrary"), 40), name="deltanet",
    )(seg_a, seg_a, seg_d, conv0, conv_w, h0, pa, ng.reshape(1, dk))


def _ssd_body(x_ref, z_ref, sm_ref, cst_ref, cw_ref, cb_ref, h0_ref, pa_ref, ng_ref, y_ref, ht_ref,
              h_scr, buf, cv, y_scr, *ov, c, segs, ng, nr, hp, ns, off):
    ci = pl.program_id(1)

    @pl.when(ci == 0)
    def _():
        h_scr[...] = h0_ref[...]

    if len(segs) > 1:
        _conv_silu_packed(x_ref, cst_ref, cw_ref, cb_ref, buf, ov[0], cv, c, segs[0][1] - segs[0][0])
    else:
        _conv_silu_rows(x_ref, cst_ref, cw_ref, cb_ref, buf, cv, c, ci)
    gw = nr * hp
    inner = ng * gw
    shift = hp.bit_length() - 1
    sm = sm_ref[...]
    pa = pa_ref[...]
    dt_raw = _softplus(sm + pa[1:2])
    dt_all, gc_all, gct, eg_all = [], [], [], []
    for seg in segs:
        live = _live_rows(c, seg)
        dt_all.append(dt_raw if live is None else jnp.where(live, dt_raw, 0.0))
        gc_all.append(_chunk_cumsum(dt_all[-1] * (-jnp.exp(pa[0:1])), c))
        gct.append(_rows_as_lanes(gc_all[-1], c))
        eg_all.append(jnp.exp(gc_all[-1]))
    tt = lax.broadcasted_iota(jnp.int32, (c, c), 0)
    ss = lax.broadcasted_iota(jnp.int32, (c, c), 1)
    incl = tt >= ss
    lane_head = lax.broadcasted_iota(jnp.int32, (c, gw), 1) >> shift
    lane_head1 = lax.broadcasted_iota(jnp.int32, (1, gw), 1) >> shift
    row_head = lax.broadcasted_iota(jnp.int32, (gw, ns), 0) >> shift
    gs = range(ng)
    units = [(s, g) for s in range(len(segs)) for g in gs]
    us = range(len(units))

    bg = [cv[:, inner + g * ns:inner + (g + 1) * ns] for g in gs]
    cg = [cv[:, inner + (ng + g) * ns:inner + (ng + g + 1) * ns] for g in gs]
    xg = [cv[:, g * gw:(g + 1) * gw] for g in gs]
    hg = [h_scr[s, g] for s, g in units]
    cb = [_bdot_nt(cg[g], bg[g]) for g in gs]
    ch = [_bdot_nt(cg[g], hg[u]) for u, (s, g) in enumerate(units)]
    xdt, xkd, yg, hdec = [], [], [], []
    for u, (s, g) in enumerate(units):
        dtb = jnp.zeros((c, gw), F32)
        egb = jnp.zeros((c, gw), F32)
        kdb = jnp.zeros((c, gw), F32)
        skip = jnp.zeros((1, gw), F32)
        hd = jnp.zeros((gw, ns), F32)
        for r in range(nr):
            col = off + g * nr + r
            gcol = gc_all[s][:, col:col + 1]
            glast = gc_all[s][c - 1:c, col:col + 1]
            seg = lane_head == r
            dtb = jnp.where(seg, dt_all[s][:, col:col + 1], dtb)
            egb = jnp.where(seg, eg_all[s][:, col:col + 1], egb)
            kdb = jnp.where(seg, jnp.exp(glast - gcol), kdb)
            skip = jnp.where(lane_head1 == r, pa[2:3, col:col + 1], skip)
            hd = jnp.where(row_head == r, jnp.exp(glast), hd)
        xdt.append(xg[g] * dtb)
        xkd.append(xdt[u] * kdb)
        yg.append(ch[u] * egb + skip * xg[g])
        hdec.append(hd)
    for r in range(nr):
        part = []
        for u, (s, g) in enumerate(units):
            col = off + g * nr + r
            lm = jnp.exp(jnp.where(incl, gc_all[s][:, col:col + 1] - gct[s][col:col + 1, 0:c], NEG))
            part.append(_bdot(cb[g] * lm, jnp.where(lane_head == r, xdt[u], 0.0)))
        yg = [yg[u] + part[u] for u in us]
    hn = [hdec[u] * hg[u] + _bdot_tn(xkd[u], bg[g]) for u, (s, g) in enumerate(units)]
    for u, (s, g) in enumerate(units):
        h_scr[s, g] = hn[u]
    for g in gs:
        yv = _by_sequence([yg[s * ng + g] for s in range(len(segs))], segs, c, gw)
        y_scr[:, g * gw:(g + 1) * gw] = yv * _silu(z_ref[:, g * gw:(g + 1) * gw])

    y = y_scr[...]
    y = y * lax.rsqrt(jnp.mean(y * y, axis=-1, keepdims=True) + EPS) * ng_ref[...]
    y_ref[...] = y.astype(y_ref.dtype)

    @pl.when(ci == pl.num_programs(1) - 1)
    def _():
        ht_ref[...] = h_scr[...]


def ssd(seg_x, seg_z, seg_d, row0, nb, nc, c, segs, conv0, conv_w, conv_b, h0, pa, ngain, ng, nr, hp, ns, off):
    inner = ng * nr * hp
    sconv = inner + 2 * ng * ns
    r0 = row0 // c
    nseq = len(segs)
    glob = lambda s, i: (r0 + s * nc + i, 0)
    st = pl.BlockSpec((nseq, ng, nr * hp, ns), lambda s, i: (s, 0, 0, 0))
    scratch = [pltpu.VMEM((nseq, ng, nr * hp, ns), F32), pltpu.VMEM((c + SUBLANES, sconv), F32),
               pltpu.VMEM((c, sconv), F32), pltpu.VMEM((c, inner), F32)]
    if nseq > 1:
        scratch.append(pltpu.VMEM((c + SUBLANES, sconv), F32))
    return pl.pallas_call(
        functools.partial(_ssd_body, c=c, segs=segs, ng=ng, nr=nr, hp=hp, ns=ns, off=off), grid=(nb, nc),
        in_specs=[pl.BlockSpec((c, sconv), glob),
                  pl.BlockSpec((c, inner), glob),
                  pl.BlockSpec((c, LANES), glob),
                  pl.BlockSpec((nseq, CONV_W - 1, sconv), lambda s, i: (s, 0, 0)),
                  pl.BlockSpec((CONV_W, sconv), lambda s, i: (0, 0)),
                  pl.BlockSpec((1, sconv), lambda s, i: (0, 0)),
                  st,
                  pl.BlockSpec((SUBLANES, LANES), lambda s, i: (0, 0)),
                  pl.BlockSpec((1, inner), lambda s, i: (0, 0))],
        out_specs=[pl.BlockSpec((c, inner), lambda s, i: (s * nc + i, 0)), st],
        out_shape=[jax.ShapeDtypeStruct((nb * nc * c, inner), BF16),
                   jax.ShapeDtypeStruct((nb * nseq, ng, nr * hp, ns), F32)],
        scratch_shapes=scratch,
        compiler_params=_params(("parallel", "arbitrary"), 40), name="ssd",
    )(seg_x, seg_z, seg_d, conv0, conv_w, conv_b.reshape(1, sconv), h0, pa, ngain.reshape(1, inner))


def _ca_body(q_ref, k_ref, v_ref, o_ref, *, nh, dh, nseq, ls):
    scale = dh ** -0.5
    rows = q_ref.shape[0]
    cols = [slice(h * dh, (h + 1) * dh) for h in range(nh)]
    if len(k_ref.shape) == 4:
        mem = k_ref.shape[1]
        qs = jnp.concatenate([q_ref[:, cols[h]] for h in range(nh)], axis=0)
        col_head = lax.rem(lax.broadcasted_iota(jnp.int32, (nh * rows, mem * nh), 1), nh)
        row_head = lax.div(lax.broadcasted_iota(jnp.int32, (nh * rows, mem * nh), 0), rows)
        own = col_head == row_head
        o = []
        for b in range(nseq):
            s = jnp.where(own, _bdot_nt(qs, k_ref[b].reshape(mem * nh, dh)) * scale, NEG)
            e = jnp.exp(s - jnp.max(s, axis=-1, keepdims=True))
            p = e / jnp.sum(e, axis=-1, keepdims=True)
            ob = _bdot(p, v_ref[b].reshape(mem * nh, dh))
            o += [ob[h * rows:(h + 1) * rows] for h in range(nh)]
    else:
        units = [(b, h) for b in range(nseq) for h in range(nh)]
        s = [_bdot_nt(q_ref[:, cols[h]], k_ref[b, :, cols[h]]) * scale for b, h in units]
        e = [jnp.exp(x - jnp.max(x, axis=-1, keepdims=True)) for x in s]
        p = [x / jnp.sum(x, axis=-1, keepdims=True) for x in e]
        o = [_bdot(p[i], v_ref[b, :, cols[h]]) for i, (b, h) in enumerate(units)]
    row = lax.broadcasted_iota(jnp.int32, (rows, dh), 0)
    for h in range(nh):
        val = o[(nseq - 1) * nh + h]
        for b in range(nseq - 2, -1, -1):
            val = jnp.where(row < (b + 1) * ls, o[b * nh + h], val)
        o_ref[:, cols[h]] = val.astype(o_ref.dtype)


def cross_attention(q2d, row0, nrows, tl, nseq, mem_k, mem_v, nh, dh):
    r0 = row0 // tl
    wd = nh * dh
    per = nrows // tl * nseq // mem_k.shape[0]
    if mem_k.ndim == 4:
        kv = pl.BlockSpec((nseq,) + mem_k.shape[1:], lambda i: (i // per, 0, 0, 0))
    else:
        kv = pl.BlockSpec((nseq,) + mem_k.shape[1:], lambda i: (i // per, 0, 0))
    return pl.pallas_call(
        functools.partial(_ca_body, nh=nh, dh=dh, nseq=nseq, ls=tl // nseq), grid=(nrows // tl,),
        in_specs=[pl.BlockSpec((tl, wd), lambda i: (r0 + i, 0)), kv, kv],
        out_specs=pl.BlockSpec((tl, wd), lambda i: (i, 0)),
        out_shape=jax.ShapeDtypeStruct((nrows, wd), BF16),
        compiler_params=_params(("parallel",), 32), name="cross_attention",
    )(q2d, mem_k, mem_v)


def _extract_top(x, n):
    row = lax.broadcasted_iota(jnp.int32, x.shape, 0).astype(F32)
    out = []
    for it in range(n):
        m = jnp.max(x, axis=0, keepdims=True)
        out.append(m)
        if it + 1 < n:
            first = jnp.min(jnp.where(x == m, row, float(x.shape[0])), axis=0, keepdims=True)
            x = jnp.where(row == first, -jnp.inf, x)
    return out


def _pair_candidates(v1, v2, topk):
    v1s = jnp.concatenate(v1, axis=0)
    v2s = jnp.concatenate(v2, axis=0)
    pieces = []
    a = 0
    while a < topk:
        nb = topk // (a + 1)
        if nb == 1 and a % SUBLANES == 0:
            pieces.append(v1s[a:] + v2[0])
            break
        pieces.append(v1[a] + v2s[:min(-(-nb // SUBLANES) * SUBLANES, topk)])
        a += 1
    return jnp.concatenate(pieces, axis=0)


def _peer_route_body(q_ref, keys_ref, s1_ref, e1_ref, s2_ref, e2_ref, thr_ref, *, nh, dq, topk):
    for h in range(nh):
        s1 = _bdot_nt(keys_ref[2 * h], q_ref[:, (2 * h) * dq:(2 * h + 1) * dq])
        s2 = _bdot_nt(keys_ref[2 * h + 1], q_ref[:, (2 * h + 1) * dq:(2 * h + 2) * dq])
        v1 = _extract_top(s1, topk)
        v2 = _extract_top(s2, topk)
        tops = _extract_top(_pair_candidates(v1, v2, topk), topk)
        zsum = jnp.exp(tops[0] - tops[0])
        for a in range(1, topk):
            zsum = zsum + jnp.exp(tops[a] - tops[0])
        s1_ref[h] = s1
        s2_ref[h] = s2
        e1_ref[h] = jnp.exp(s1 - v1[0]) / zsum
        e2_ref[h] = jnp.exp(s2 - v2[0])
        thr_ref[h:h + 1, :] = tops[topk - 1]


def peer_route(q, keys, nh, nk, dq, tm=256):
    m = q.shape[0]
    tm = _tile(m, tm)
    big = pl.BlockSpec((nh, nk, tm), lambda i: (0, 0, i))
    shape = jax.ShapeDtypeStruct((nh, nk, m), F32)
    return pl.pallas_call(
        functools.partial(_peer_route_body, nh=nh, dq=dq, topk=PEER_TOPK), grid=(m // tm,),
        in_specs=[pl.BlockSpec((tm, 2 * nh * dq), lambda i: (i, 0)),
                  pl.BlockSpec((2 * nh, nk, dq), lambda i: (0, 0, 0))],
        out_specs=[big, big, big, big, pl.BlockSpec((nh, tm), lambda i: (0, i))],
        out_shape=[shape, shape, shape, shape, jax.ShapeDtypeStruct((nh, m), F32)],
        compiler_params=_params(("parallel",), 32), name="peer_route",
    )(q, keys)


def _peer_body(hft_ref, u_ref, vt_ref, s1_ref, e1_ref, s2_ref, e2_ref, thr_ref, ot_ref, st0, st1, act0, act1,
               *, nh, nk, nrow, dchunk):
    ei = pl.program_id(1)
    d, tm = hft_ref.shape
    ne = nk // nrow

    @pl.when(ei == 0)
    def _():
        ot_ref[...] = jnp.zeros_like(ot_ref)

    pieces = d // (nrow * nk)
    kchunk = d // pieces

    def step(a, do_scores=True, do_gates=True, do_values=True):
        st_new, st_old = (st0, st1) if a == 0 else (st1, st0)
        act_old, act_new = (act0, act1) if a == 0 else (act1, act0)

        def scores(p):
            ks = slice(p * kchunk, (p + 1) * kchunk)
            part = jnp.dot(u_ref[:, ks], hft_ref[ks, :], preferred_element_type=F32)
            if p == 0:
                st_new[...] = part
            else:
                st_new[...] += part

        def apply_values(j):
            ds = slice(j * dchunk, (j + 1) * dchunk)
            ot_ref[ds, :] += jnp.dot(vt_ref[ds, :], act_old[...], preferred_element_type=F32)

        def gate_block(r, tb):
            ls = slice(tb * LANES, (tb + 1) * LANES)
            rows = slice(r * nk, (r + 1) * nk)
            gate = jnp.zeros((nk, LANES), F32)
            for h in range(nh):
                hit = (s2_ref[h, :, ls] + s1_ref[r, h:h + 1, ls]) >= thr_ref[h:h + 1, ls]
                gate = gate + jnp.where(hit, e2_ref[h, :, ls] * e1_ref[r, h:h + 1, ls], 0.0)
            act_new[rows, ls] = (_gelu_tanh(st_old[rows, ls]) * gate).astype(BF16)

        blocks = [(r, tb) for r in range(nrow) for tb in range(tm // LANES)]
        nd = d // dchunk
        per = -(-len(blocks) // nd)
        every = max(nd // pieces, 1)
        for j in range(max(nd, pieces * every)):
            if do_scores and j % every == 0 and j // every < pieces:
                scores(j // every)
            if do_values and j < nd:
                apply_values(j)
            if do_gates:
                for r, tb in blocks[j * per:(j + 1) * per]:
                    gate_block(r, tb)

    steady = jnp.logical_and(ei >= 2, ei < ne)
    for a in range(2):
        pl.when(jnp.logical_and(steady, ei % 2 == a))(functools.partial(step, a))
    for e in sorted({0, 1, ne, ne + 1}):
        if not 2 <= e < ne:
            pl.when(ei == e)(functools.partial(step, e % 2, e < ne, 1 <= e <= ne, e >= 2))


def peer_experts(hft, u, vt, s1g, e1g, s2, e2, thr, nh, nk, tm=512, nrow=4):
    d, m = hft.shape
    tm = _tile(m, tm)
    et = nrow * nk
    once = pl.Buffered(1)
    ne = nk // nrow
    tile = lambda e, lag: jnp.clip(e - lag, 0, ne - 1)
    return pl.pallas_call(
        functools.partial(_peer_body, nh=nh, nk=nk, nrow=nrow, dchunk=_tile(d, 512)), grid=(m // tm, ne + 2),
        in_specs=[pl.BlockSpec((d, tm), lambda t, e: (0, t), pipeline_mode=once),
                  pl.BlockSpec((et, d), lambda t, e: (tile(e, 0), 0)),
                  pl.BlockSpec((d, et), lambda t, e: (0, tile(e, 2))),
                  pl.BlockSpec((nrow, nh, tm), lambda t, e: (tile(e, 1), 0, t)),
                  pl.BlockSpec((nrow, nh, tm), lambda t, e: (tile(e, 1), 0, t)),
                  pl.BlockSpec((nh, nk, tm), lambda t, e: (0, 0, t), pipeline_mode=once),
                  pl.BlockSpec((nh, nk, tm), lambda t, e: (0, 0, t), pipeline_mode=once),
                  pl.BlockSpec((nh, tm), lambda t, e: (0, t))],
        out_specs=pl.BlockSpec((d, tm), lambda t, e: (0, t)),
        out_shape=jax.ShapeDtypeStruct((d, m), F32),
        scratch_shapes=[pltpu.VMEM((et, tm), F32), pltpu.VMEM((et, tm), F32),
                        pltpu.VMEM((et, tm), BF16), pltpu.VMEM((et, tm), BF16)],
        compiler_params=_params(("parallel", "arbitrary"), 56), name="peer_experts",
    )(hft, u, vt, s1g, e1g, s2, e2, thr)


def _param_tile(pieces):
    out = jnp.zeros((SUBLANES, LANES), F32)
    for row, off, vec in pieces:
        out = out.at[row, off:off + vec.shape[0]].set(vec.astype(F32))
    return out


def kernel(x_prompt, x_sample, cache_mem_k, cache_mem_v, state_dn_conv, state_dn_rec, state_ssm_conv,
           state_ssm_rec, mem_prompt, norm_mix_g, w_in, dn_conv_w, dn_a_log, dn_dt_bias, dn_norm_g, dn_w_out,
           ssm_conv_w, ssm_conv_b, ssm_a_log, ssm_dt_bias, ssm_d, ssm_norm_g, ssm_w_out, w_o, norm_ca_g,
           ca_w_q, ca_w_k, ca_w_v, ca_w_o, norm_ffn_g, peer_w_q, peer_keys, peer_u, peer_v, final_norm_g):
    depth = w_in.shape[0]
    bp, lp, d = x_prompt.shape
    bs, ls, _ = x_sample.shape
    _, _, dnh, dk, _ = state_dn_rec.shape
    _, _, sh, hp, ns = state_ssm_rec.shape
    dnw = dnh * dk
    inner = sh * hp
    sconv = state_ssm_conv.shape[-1]
    sg = (sconv - inner) // (2 * ns)
    sr = sh // sg
    _, _, mem, cah, cad = cache_mem_k.shape
    caw = cah * cad
    _, pnh, _, nk, dq = peer_keys.shape
    packed = SUBLANES % ls == 0 and ls >= CONV_W - 1 and bs % (SUBLANES // ls) == 0
    lrow = ls if packed else -(-ls // SUBLANES) * SUBLANES
    cs = SUBLANES if packed else lrow
    nseq = cs // lrow
    s_segs = tuple((j * lrow, j * lrow + ls) for j in range(nseq))
    assert lp % CHUNK == 0 and cs <= CHUNK and dk == LANES and ns == LANES and nk == LANES and dq == LANES
    assert 2 * dnh + sh <= LANES and hp & (hp - 1) == 0
    mp, ms = bp * lp, bs * lrow
    ssm_off = 2 * dnh

    x = [x_prompt.reshape(mp, d), jnp.pad(x_sample, ((0, 0), (0, lrow - ls), (0, 0))).reshape(ms, d)]

    outs = [[] for _ in range(10)]
    for l in range(depth):
        wt = w_in[l].T
        o1 = 3 * dnw + dnw
        o2 = o1 + 2 * dnh
        o3 = o2 + inner
        o4 = o3 + sconv
        o5 = o4 + sh
        w_d = jnp.concatenate([wt[o1:o2], wt[o4:o5], jnp.zeros((LANES - 2 * dnh - sh, d), F32)], axis=0)

        hn = rmsnorm(x, norm_mix_g[l], BF16)
        def proj(start, n):
            if start % SUBLANES:
                return matmul(hn, wt[start:start + n].astype(BF16), w_rows_out=True)
            return matmul(hn, wt, w_rows_out=True, w_rows=(start, n))

        seg_a = proj(0, o1)
        seg_z = proj(o2, o3 - o2)
        seg_x = proj(o3, o4 - o3)
        seg_c = proj(o5, 2 * d)
        seg_d = matmul(hn, w_d, w_rows_out=True, w_rows=(0, LANES))

        dn_pa = _param_tile([(0, dnh, dn_a_log[l]), (1, dnh, dn_dt_bias[l])])
        ssm_pa = _param_tile([(0, ssm_off, ssm_a_log[l]), (1, ssm_off, ssm_dt_bias[l]), (2, ssm_off, ssm_d[l])])
        groups = (
            (0, bp, lp // CHUNK, CHUNK, ((0, CHUNK),), jnp.zeros((bp, CONV_W - 1, 3 * dnw), F32),
             jnp.zeros((bp, dnh, dk, dk), F32), jnp.zeros((bp, CONV_W - 1, sconv), F32),
             jnp.zeros((bp, sg, sr * hp, ns), F32)),
            (mp, bs // nseq, 1, cs, s_segs, state_dn_conv[l], state_dn_rec[l],
             state_ssm_conv[l], state_ssm_rec[l].reshape(bs, sg, sr * hp, ns)),
        )
        o_dn, o_ssm, dn_h, ssm_h = [], [], [], []
        for row0, nb, nc, c, segs, dc0, dh0, sc0, sh0 in groups:
            od, hd = deltanet(seg_a, seg_d, row0, nb, nc, c, segs, dc0, dn_conv_w[l], dh0, dn_pa, dn_norm_g[l],
                              dnh, dk)
            osd, hsd = ssd(seg_x, seg_z, seg_d, row0, nb, nc, c, segs, sc0, ssm_conv_w[l], ssm_conv_b[l], sh0, ssm_pa,
                           ssm_norm_g[l], sg, sr, hp, ns, ssm_off)
            o_dn.append(od)
            o_ssm.append(osd)
            dn_h.append(hd)
            ssm_h.append(hsd.reshape(-1, sh, hp, ns))

        mixed = gated_mix(o_dn, dn_w_out[l].astype(BF16), o_ssm, ssm_w_out[l].astype(BF16), seg_c)
        x1 = matmul(mixed, w_o[l].astype(BF16), res=x)

        hc = rmsnorm(x1, norm_ca_g[l], BF16)
        qc = matmul(hc, ca_w_q[l].astype(BF16))
        memp = mem_prompt.reshape(bp * mem, d)
        mk = matmul(memp, ca_w_k[l].astype(BF16))
        mv = matmul(memp, ca_w_v[l].astype(BF16))
        oc = [cross_attention(qc, 0, mp, _tile(lp, 512), 1, mk.reshape(bp, mem, caw), mv.reshape(bp, mem, caw),
                              cah, cad),
              cross_attention(qc, mp, ms, cs, nseq, cache_mem_k[l], cache_mem_v[l], cah, cad)]
        x2 = matmul(oc, ca_w_o[l].astype(BF16), res=x1)

        hf, hft = rmsnorm(x2, norm_ffn_g[l], BF16, with_transpose=True)
        pq = matmul(hf, peer_w_q[l].astype(BF16))
        s1, e1, s2, e2, thr = peer_route(pq, peer_keys[l].reshape(2 * pnh, nk, dq).astype(BF16), pnh, nk, dq)
        pot = peer_experts(hft, peer_u[l].astype(BF16), transpose_cast(peer_v[l], BF16),
                           s1.transpose(1, 0, 2), e1.transpose(1, 0, 2), s2, e2, thr, pnh, nk)
        if l == depth - 1:
            y_prompt = add_rows(x2, pot, final_norm_g, 0, mp).reshape(bp, lp, d)
            y_sample = add_rows(x2, pot, final_norm_g, mp, ms).reshape(bs, lrow, d)[:, :ls]
        else:
            x = add_rows(x2, pot, None, 0, mp + ms)

        def tail(seg, row0, nb, sl, nvalid, lo, hi, state0):
            take = min(CONV_W - 1, nvalid)
            rows = row0 + np.arange(nb)[:, None] * sl + np.arange(nvalid - take, nvalid)[None, :]
            u = jnp.take(seg, rows.reshape(-1), axis=0)[:, lo:hi].reshape(nb, take, hi - lo)
            if take < CONV_W - 1:
                u = jnp.concatenate([state0[:, take:], u], axis=1)
            return u

        outs[0].append(mk.reshape(bp, mem, cah, cad))
        outs[1].append(mv.reshape(bp, mem, cah, cad))
        outs[2].append(tail(seg_a, 0, bp, lp, lp, 0, 3 * dnw, groups[0][5]))
        outs[3].append(dn_h[0])
        outs[4].append(tail(seg_x, 0, bp, lp, lp, 0, sconv, groups[0][7]))
        outs[5].append(ssm_h[0])
        outs[6].append(tail(seg_a, mp, bs, lrow, ls, 0, 3 * dnw, state_dn_conv[l]))
        outs[7].append(dn_h[1])
        outs[8].append(tail(seg_x, mp, bs, lrow, ls, 0, sconv, state_ssm_conv[l]))
        outs[9].append(ssm_h[1])

    return (y_prompt, y_sample) + tuple(jnp.stack(o) for o in outs)
```
